```python
import math
import jax
import jax.numpy as jnp
from jax import lax
import numpy as np

D_MODEL = 1024
BATCH = 4
SEQ = 4096
DEPTH = 2

EPS = 1e-6
ROPE_THETA = 500000.0
Q_BLOCK = 128
D_FF = 2816
NEG_BIG = -1e30
FORCE_SCORE = 1e9

A_HEADS = 6
A_Q_RANK = 192
A_KV_RANK = 128
A_NOPE = 64
A_ROPE = 32
A_V = 64

B_HEADS = 6
B_KV_GROUPS = 2
B_HPG = B_HEADS // B_KV_GROUPS
B_DH = 64
B_ROT = B_DH // 4
CMP_LEN = 32
CMP_STRIDE = 16
SLC_LEN = 64
SLC_TOPN = 16
WINDOW = 512

C_HEADS = 4
C_DH = 32
C_ROT = C_DH // 4

MIX_WIDTH = A_HEADS * A_V + B_HEADS * B_DH + C_HEADS * 2 * C_DH
A_COLS = A_Q_RANK + A_KV_RANK + A_ROPE
B_COLS = B_HEADS * B_DH + 6 * B_KV_GROUPS * B_DH + 3 * B_HEADS
C_COLS = 3 * C_HEADS * 2 * C_DH
IN_COLS = A_COLS + B_COLS + C_COLS

kernel_name = 'hybrid_mla_nsa_diffattn_macaron'


def rmsnorm(x, g):
    xf = x.astype(jnp.float32)
    y = xf * lax.rsqrt(jnp.mean(xf * xf, axis=-1, keepdims=True) + EPS)
    return (y * g.astype(jnp.float32)).astype(x.dtype)


def rope_tables(positions, rot_dim):
    inv_freq = 1.0 / (ROPE_THETA ** (jnp.arange(0, rot_dim, 2, dtype=jnp.float32) / rot_dim))
    ang = positions.astype(jnp.float32)[..., None] * inv_freq
    return (jnp.cos(ang), jnp.sin(ang))


def apply_rope(x, cs):
    cos, sin = cs
    B, T, r2 = cos.shape
    shape = (B,) + (1,) * (x.ndim - 3) + (T, r2)
    cos = cos.reshape(shape)
    sin = sin.reshape(shape)
    xf = x.astype(jnp.float32)
    x1, x2 = xf[..., :r2], xf[..., r2:]
    return jnp.concatenate([x1 * cos - x2 * sin, x1 * sin + x2 * cos], axis=-1).astype(x.dtype)


def partial_rope(x, cs):
    r = 2 * cs[0].shape[-1]
    return jnp.concatenate([apply_rope(x[..., :r], cs), x[..., r:]], axis=-1)


def swiglu(x, wg, wu, wd):
    return (jax.nn.silu(x @ wg) * (x @ wu)) @ wd


def causal_block_attention(q, k, v, scale):
    B, H, T, dk = q.shape
    nb = T // Q_BLOCK
    q_blocks = jnp.moveaxis(q.reshape(B, H, nb, Q_BLOCK, dk), 2, 0)
    kpos = jnp.arange(T)

    def one_block(args):
        qi, i = args
        s = jnp.einsum('bhqd,bhkd->bhqk', qi, k, preferred_element_type=jnp.float32) * scale
        qpos = i * Q_BLOCK + jnp.arange(Q_BLOCK)
        s = jnp.where(kpos[None, :] <= qpos[:, None], s, -jnp.inf)
        p = jax.nn.softmax(s, axis=-1)
        return jnp.einsum('bhqk,bhkd->bhqd', p.astype(v.dtype), v)

    o = lax.map(one_block, (q_blocks, jnp.arange(nb)))
    return jnp.moveaxis(o, 0, 2).reshape(B, H, T, v.shape[-1])


def mla_mixer(h, cs_a, q_norm, kv_norm, w_uq, w_ukv):
    B, T, _ = h.shape
    c_q, c_kv, k_pe = jnp.split(h, [A_Q_RANK, A_Q_RANK + A_KV_RANK], axis=-1)
    q = (rmsnorm(c_q, q_norm) @ w_uq).reshape(B, T, A_HEADS, A_NOPE + A_ROPE).transpose(0, 2, 1, 3)
    kv = (rmsnorm(c_kv, kv_norm) @ w_ukv).reshape(B, T, A_HEADS, A_NOPE + A_V).transpose(0, 2, 1, 3)
    q = jnp.concatenate([q[..., :A_NOPE], apply_rope(q[..., A_NOPE:], cs_a)], axis=-1)
    k_pe = apply_rope(k_pe[:, None], cs_a)
    k = jnp.concatenate([kv[..., :A_NOPE], jnp.broadcast_to(k_pe, (B, A_HEADS, T, A_ROPE))], axis=-1)
    v = kv[..., A_NOPE:]
    o = causal_block_attention(q, k, v, (A_NOPE + A_ROPE) ** -0.5)
    return o.transpose(0, 2, 1, 3).reshape(B, T, A_HEADS * A_V)


def nsa_mixer(h, cs_b, pe_k, pe_v, phi_k1, phi_k2, phi_v1, phi_v2):
    B, T, _ = h.shape
    G, Hg, DH = B_KV_GROUPS, B_HPG, B_DH
    qw, kvw = B_HEADS * DH, G * DH
    offs = [qw + j * kvw for j in range(7)]
    q, kc, vc, ks, vs, kw, vw, gate_logits = jnp.split(h, offs, axis=-1)

    def split_kv(t):
        return t.reshape(B, T, G, DH).transpose(0, 2, 1, 3)

    q = partial_rope(q.reshape(B, T, G, Hg, DH).transpose(0, 2, 3, 1, 4), cs_b)
    kc, ks, kw = [partial_rope(split_kv(t), cs_b) for t in (kc, ks, kw)]
    vc, vs, vw = [split_kv(t) for t in (vc, vs, vw)]
    scale = DH ** -0.5
    tpos = jnp.arange(T)
    nb = T // Q_BLOCK

    n_cmp = (T - CMP_LEN) // CMP_STRIDE + 1
    cmp_idx = jnp.arange(n_cmp)[:, None] * CMP_STRIDE + jnp.arange(CMP_LEN)[None, :]

    def compress(t, pe, w1, w2):
        blk = (t[:, :, cmp_idx] + pe).reshape(B, G, n_cmp, CMP_LEN * DH)
        return jax.nn.silu(blk @ w1) @ w2

    k_cmp = compress(kc, pe_k, phi_k1, phi_k2)
    v_cmp = compress(vc, pe_v, phi_v1, phi_v2)
    s_cmp = jnp.einsum('bghtd,bgcd->bghtc', q, k_cmp, preferred_element_type=jnp.float32) * scale
    cmp_mask = cmp_idx[:, -1][None, :] <= tpos[:, None]
    p_cmp = jax.nn.softmax(jnp.where(cmp_mask, s_cmp, NEG_BIG), axis=-1) * cmp_mask
    o_cmp = jnp.einsum('bghtc,bgcd->bghtd', p_cmp.astype(v_cmp.dtype), v_cmp)

    n_slc = T // SLC_LEN
    top_n = min(SLC_TOPN, n_slc)
    blk_start = jnp.arange(n_slc) * SLC_LEN
    overlap = ((cmp_idx[:, 0][:, None] < blk_start[None, :] + SLC_LEN)
               & (cmp_idx[:, -1][:, None] >= blk_start[None, :])).astype(jnp.float32)
    imp = jnp.einsum('bghtc,cm->bgtm', p_cmp, overlap)
    cur = (tpos // SLC_LEN)[:, None]
    m = jnp.arange(n_slc)[None, :]
    forced = (m == 0) | (m == cur) | (m == cur - 1)
    score = jnp.where(forced, FORCE_SCORE, jnp.where(m <= cur, imp, -1.0))
    sel = lax.top_k(score, top_n)[1]

    ks_blk = ks.reshape(B, G, n_slc, SLC_LEN, DH)
    vs_blk = vs.reshape(B, G, n_slc, SLC_LEN, DH)
    q_blocks = jnp.moveaxis(q.reshape(B, G, Hg, nb, Q_BLOCK, DH), 3, 0)
    sel_blocks = jnp.moveaxis(sel.reshape(B, G, nb, Q_BLOCK, top_n), 2, 0)
    gather_blocks = jax.vmap(jax.vmap(lambda blocks, ids: blocks[ids]))

    def slc_block(args):
        qi, si, i = args
        kg = gather_blocks(ks_blk, si).reshape(B, G, Q_BLOCK, top_n * SLC_LEN, DH)
        vg = gather_blocks(vs_blk, si).reshape(B, G, Q_BLOCK, top_n * SLC_LEN, DH)
        kpos = (si[..., None] * SLC_LEN + jnp.arange(SLC_LEN)).reshape(B, G, Q_BLOCK, top_n * SLC_LEN)
        qpos = i * Q_BLOCK + jnp.arange(Q_BLOCK)
        valid = kpos <= qpos[:, None]
        s = jnp.einsum('bghqd,bgqkd->bghqk', qi, kg, preferred_element_type=jnp.float32) * scale
        p = jax.nn.softmax(jnp.where(valid[:, :, None], s, -jnp.inf), axis=-1)
        return jnp.einsum('bghqk,bgqkd->bghqd', p.astype(vg.dtype), vg)

    o_slc = lax.map(slc_block, (q_blocks, sel_blocks, jnp.arange(nb)))
    o_slc = jnp.moveaxis(o_slc, 0, 3).reshape(B, G, Hg, T, DH)

    kw_pad = jnp.pad(kw, ((0, 0), (0, 0), (WINDOW, 0), (0, 0)))
    vw_pad = jnp.pad(vw, ((0, 0), (0, 0), (WINDOW, 0), (0, 0)))
    band = jnp.arange(nb)[:, None] * Q_BLOCK + jnp.arange(WINDOW + Q_BLOCK)[None, :]
    kband = kw_pad[:, :, band]
    vband = vw_pad[:, :, band]
    kpos_w = band - WINDOW
    qpos_w = tpos.reshape(nb, Q_BLOCK)
    dist = qpos_w[:, :, None] - kpos_w[:, None, :]
    wmask = (dist >= 0) & (dist < WINDOW) & (kpos_w[:, None, :] >= 0)
    qwb = q.reshape(B, G, Hg, nb, Q_BLOCK, DH)
    s_w = jnp.einsum('bghnqd,bgnkd->bghnqk', qwb, kband, preferred_element_type=jnp.float32) * scale
    p_w = jax.nn.softmax(jnp.where(wmask, s_w, -jnp.inf), axis=-1)
    o_win = jnp.einsum('bghnqk,bgnkd->bghnqd', p_w.astype(vband.dtype), vband).reshape(B, G, Hg, T, DH)

    g = jax.nn.sigmoid(gate_logits.astype(jnp.float32)).reshape(B, T, G, Hg, 3)
    g = g.transpose(0, 2, 3, 1, 4).astype(q.dtype)
    o = g[..., 0:1] * o_cmp + g[..., 1:2] * o_slc + g[..., 2:3] * o_win
    return o.transpose(0, 3, 1, 2, 4).reshape(B, T, B_HEADS * DH)


def diff_mixer(h, cs_c, lq1, lk1, lq2, lk2, sub_norm, layer_idx):
    B, T, _ = h.shape
    q, k, v = jnp.split(h, 3, axis=-1)
    q = partial_rope(q.reshape(B, T, C_HEADS, 2, C_DH).transpose(0, 2, 3, 1, 4), cs_c)
    k = partial_rope(k.reshape(B, T, C_HEADS, 2, C_DH).transpose(0, 2, 3, 1, 4), cs_c)
    v = v.reshape(B, T, C_HEADS, 2 * C_DH).transpose(0, 2, 1, 3)
    lam_init = 0.8 - 0.6 * math.exp(-0.3 * layer_idx)
    f32 = jnp.float32
    lam = (jnp.exp(jnp.sum(lq1.astype(f32) * lk1.astype(f32)))
           - jnp.exp(jnp.sum(lq2.astype(f32) * lk2.astype(f32))) + lam_init)
    scale = C_DH ** -0.5
    o1 = causal_block_attention(q[:, :, 0], k[:, :, 0], v, scale)
    o2 = causal_block_attention(q[:, :, 1], k[:, :, 1], v, scale)
    o = o1.astype(f32) - lam * o2.astype(f32)
    o = (rmsnorm(o, sub_norm) * (1.0 - lam_init)).astype(h.dtype)
    return o.transpose(0, 2, 1, 3).reshape(B, T, C_HEADS * 2 * C_DH)


def setup_inputs(seed: int = 0) -> dict:
    key = jax.random.key(seed)
    keys = iter(jax.random.split(key, 40))
    f32 = jnp.float32
    L = DEPTH

    def nrm(shape, fan_in):
        return jax.random.normal(next(keys), shape, f32) * (fan_in ** -0.5)

    def gain(shape):
        return 1.0 + 0.02 * jax.random.normal(next(keys), shape, f32)

    def small(shape, s):
        return s * jax.random.normal(next(keys), shape, f32)

    x = jax.random.normal(next(keys), (BATCH, SEQ, D_MODEL), f32)
    offset = jax.random.randint(next(keys), (BATCH, 1), 0, 1024, dtype=jnp.int32)
    positions = offset + jnp.arange(SEQ, dtype=jnp.int32)[None, :]
    return {
        'x': x,
        'positions': positions,
        'ffn1_norm': gain((L, D_MODEL)),
        'ffn1_wg': nrm((L, D_MODEL, D_FF), D_MODEL),
        'ffn1_wu': nrm((L, D_MODEL, D_FF), D_MODEL),
        'ffn1_wd': nrm((L, D_FF, D_MODEL), D_FF),
        'mix_norm': gain((L, D_MODEL)),
        'w_in': nrm((L, D_MODEL, IN_COLS), D_MODEL),
        'mla_q_norm': gain((L, A_Q_RANK)),
        'mla_kv_norm': gain((L, A_KV_RANK)),
        'mla_w_uq': nrm((L, A_Q_RANK, A_HEADS * (A_NOPE + A_ROPE)), A_Q_RANK),
        'mla_w_ukv': nrm((L, A_KV_RANK, A_HEADS * (A_NOPE + A_V)), A_KV_RANK),
        'nsa_pe_k': small((L, CMP_LEN, B_DH), 0.1),
        'nsa_pe_v': small((L, CMP_LEN, B_DH), 0.1),
        'nsa_phi_k1': nrm((L, CMP_LEN * B_DH, B_DH), CMP_LEN * B_DH),
        'nsa_phi_k2': nrm((L, B_DH, B_DH), B_DH),
        'nsa_phi_v1': nrm((L, CMP_LEN * B_DH, B_DH), CMP_LEN * B_DH),
        'nsa_phi_v2': nrm((L, B_DH, B_DH), B_DH),
        'diff_lq1': small((L, C_DH), 0.1),
        'diff_lk1': small((L, C_DH), 0.1),
        'diff_lq2': small((L, C_DH), 0.1),
        'diff_lk2': small((L, C_DH), 0.1),
        'diff_sub_norm': gain((L, 2 * C_DH)),
        'w_out': nrm((L, MIX_WIDTH, D_MODEL), MIX_WIDTH),
        'ffn2_norm': gain((L, D_MODEL)),
        'ffn2_wg': nrm((L, D_MODEL, D_FF), D_MODEL),
        'ffn2_wu': nrm((L, D_MODEL, D_FF), D_MODEL),
        'ffn2_wd': nrm((L, D_FF, D_MODEL), D_FF),
        'final_norm': gain((D_MODEL,)),
    }


def reference(x, positions, ffn1_norm, ffn1_wg, ffn1_wu, ffn1_wd, mix_norm, w_in,
              mla_q_norm, mla_kv_norm, mla_w_uq, mla_w_ukv,
              nsa_pe_k, nsa_pe_v, nsa_phi_k1, nsa_phi_k2, nsa_phi_v1, nsa_phi_v2,
              diff_lq1, diff_lk1, diff_lq2, diff_lk2, diff_sub_norm, w_out,
              ffn2_norm, ffn2_wg, ffn2_wu, ffn2_wd, final_norm):
    cs_a = rope_tables(positions, A_ROPE)
    cs_b = rope_tables(positions, B_ROT)
    cs_c = rope_tables(positions, C_ROT)
    for l in range(DEPTH):
        x = x + 0.5 * swiglu(rmsnorm(x, ffn1_norm[l]), ffn1_wg[l], ffn1_wu[l], ffn1_wd[l])
        h = rmsnorm(x, mix_norm[l]) @ w_in[l]
        h_a, h_b, h_c = jnp.split(h, [A_COLS, A_COLS + B_COLS], axis=-1)
        o_a = mla_mixer(h_a, cs_a, mla_q_norm[l], mla_kv_norm[l], mla_w_uq[l], mla_w_ukv[l])
        o_b = nsa_mixer(h_b, cs_b, nsa_pe_k[l], nsa_pe_v[l], nsa_phi_k1[l], nsa_phi_k2[l],
                        nsa_phi_v1[l], nsa_phi_v2[l])
        o_c = diff_mixer(h_c, cs_c, diff_lq1[l], diff_lk1[l], diff_lq2[l], diff_lk2[l],
                         diff_sub_norm[l], l)
        o = jnp.concatenate([o_a, o_b, o_c], axis=-1)
        x = x + o @ w_out[l]
        x = x + 0.5 * swiglu(rmsnorm(x, ffn2_norm[l]), ffn2_wg[l], ffn2_wu[l], ffn2_wd[l])
    return rmsnorm(x, final_norm)
```

```python
import functools
import math

import jax
import jax.numpy as jnp
from jax import lax
from jax.experimental import pallas as pl
from jax.experimental.pallas import tpu as pltpu

F32 = jnp.float32
BF16 = jnp.bfloat16

D_MODEL = 1024
BATCH = 4
SEQ = 4096
DEPTH = 2
N_TOK = BATCH * SEQ
EPS = 1e-6
ROPE_THETA = 500000.0
D_FF = 2816

A_HEADS = 6
A_Q_RANK = 192
A_KV_RANK = 128
A_NOPE = 64
A_ROPE = 32
A_V = 64

B_HEADS = 6
B_KV_GROUPS = 2
B_HPG = 3
B_DH = 64
B_ROT = 16
CMP_LEN = 32
CMP_STRIDE = 16
N_CMP = (SEQ - CMP_LEN) // CMP_STRIDE + 1
SLC_LEN = 64
N_SLC = SEQ // SLC_LEN
SLC_TOPN = 16
WINDOW = 512

C_HEADS = 4
C_DH = 32
C_ROT = 8

A_COLS = A_Q_RANK + A_KV_RANK + A_ROPE
B_COLS = B_HEADS * B_DH + 6 * B_KV_GROUPS * B_DH + 3 * B_HEADS
C_COLS = 3 * C_HEADS * 2 * C_DH

LANE = 128
MASKED = -1e30
SEL_BIG = 1e9

TM = 512
TF = 256
NFC = D_FF // TF
TQ = 256
TK = 256
NQ = SEQ // TQ

CE_A = 0
CE_BQ = 512
CE_KVC = 1280
CE_KS = 1536
CE_VS = 1792
CE_KW = 2048
CE_VW = 2304
CE_GATE = 2560
CE_CQ = 2816
CE_CKV = 3328
CE = 3840

VMEM_LIMIT = 56 * 1024 * 1024

NT_DIMS = (((1,), (1,)), ((), ()))


def _const_spec(shape):
    nd = len(shape)
    return pl.BlockSpec(shape, lambda *_: (0,) * nd, pipeline_mode=pl.Buffered(1))


def _rms(x, g, n):
    ms = jnp.sum(x * x, axis=-1, keepdims=True) * (1.0 / n)
    return x * lax.rsqrt(ms + EPS) * g


def _dot(a, b):
    return jnp.dot(a, b, preferred_element_type=F32)


def _dot_nt(a, b):
    return lax.dot_general(a, b, NT_DIMS, preferred_element_type=F32)


def _ffn_kernel(*refs, mix, final):
    it = iter(refs)
    x_ref = next(it)
    if mix:
        oa_ref, ob_ref, oc_ref, wo_ref = next(it), next(it), next(it), next(it)
    g_ref, wgu_ref, wd_ref = next(it), next(it), next(it)
    if final:
        fg_ref = next(it)
    o_ref = next(it)
    acc_ref = next(it)

    x = x_ref[...]
    if mix:
        x = x + _dot(oa_ref[...], wo_ref[0:768, :])
        x = x + _dot(ob_ref[...], wo_ref[768:1536, :])
        x = x + _dot(oc_ref[...], wo_ref[1536:2048, :])
    xn = _rms(x, g_ref[...], D_MODEL).astype(BF16)
    for c in range(NFC):
        gu = _dot(xn, wgu_ref[:, c * 2 * TF:(c + 1) * 2 * TF])
        g = gu[:, :TF]
        u = gu[:, TF:]
        h = (g * jax.nn.sigmoid(g) * u).astype(BF16)
        d = _dot(h, wd_ref[c * TF:(c + 1) * TF, :])
        if c == 0:
            acc_ref[...] = d
        else:
            acc_ref[...] += d
    y = x + 0.5 * acc_ref[...]
    if final:
        y = _rms(y, fg_ref[...], D_MODEL)
    o_ref[...] = y


def _ffn_call(x, gain, wgu, wd, mix=None, final_gain=None):
    tok = lambda w: pl.BlockSpec((TM, w), lambda i: (i, 0))
    args = [x]
    specs = [tok(D_MODEL)]
    if mix is not None:
        oa, ob, oc, wo = mix
        args += [oa, ob, oc, wo]
        specs += [tok(768), tok(768), tok(512), _const_spec((2048, D_MODEL))]
    args += [gain, wgu, wd]
    specs += [_const_spec((1, D_MODEL)), _const_spec((D_MODEL, 2 * D_FF)),
              _const_spec((D_FF, D_MODEL))]
    if final_gain is not None:
        args.append(final_gain)
        specs.append(_const_spec((1, D_MODEL)))
    kern = functools.partial(_ffn_kernel, mix=mix is not None, final=final_gain is not None)
    return pl.pallas_call(
        kern,
        grid=(N_TOK // TM,),
        in_specs=specs,
        out_specs=tok(D_MODEL),
        out_shape=jax.ShapeDtypeStruct((N_TOK, D_MODEL), F32),
        scratch_shapes=[pltpu.VMEM((TM, D_MODEL), F32)],
        compiler_params=pltpu.CompilerParams(
            dimension_semantics=("arbitrary",), vmem_limit_bytes=VMEM_LIMIT),
        name="ffn",
    )(*args)


ROPE_HALF = (16, 8, 4)


def _proj_kernel(x_ref, g_ref, w_ref, rope_ref, qn_ref, kvn_ref, wuq_ref, wukv_ref,
                 qa_ref, ka_ref, va_ref, qb_ref, kvc_ref, ks_ref, vs_ref, kw_ref, vw_ref,
                 gb_ref, qc_ref, kvd_ref):
    xn = _rms(x_ref[...], g_ref[...], D_MODEL).astype(BF16)
    lane = lax.broadcasted_iota(jnp.int32, (TM, LANE), 1)
    second = (
        (lane >= 80) & (lane < 96),
        (lane >= 8) & (lane < 16),
        ((lane >= 4) & (lane < 8)) | ((lane >= 36) & (lane < 40)),
    )

    def rope(v, typ):
        cos = rope_ref[:, 2 * typ * LANE:(2 * typ + 1) * LANE]
        sin = rope_ref[:, (2 * typ + 1) * LANE:(2 * typ + 2) * LANE]
        r = ROPE_HALF[typ]
        partner = jnp.where(second[typ], pltpu.roll(v, r, 1), pltpu.roll(v, LANE - r, 1))
        return v * cos + partner * sin

    def blk(v, b):
        return v[:, b * LANE:(b + 1) * LANE]

    ha = _dot(xn, w_ref[:, CE_A:CE_A + 512])
    cqn = _rms(ha[:, 0:256], qn_ref[...], A_Q_RANK).astype(BF16)
    ckvn = _rms(ha[:, 256:384], kvn_ref[...], A_KV_RANK).astype(BF16)
    kpe = rope(ha[:, 384:512], 0)
    q = _dot(cqn, wuq_ref[...])
    kv = _dot(ckvn, wukv_ref[...])
    scale_a = (A_NOPE + A_ROPE) ** -0.5
    for h in range(A_HEADS):
        qa_ref[:, h * LANE:(h + 1) * LANE] = (rope(blk(q, h), 0) * scale_a).astype(BF16)
        ka_ref[:, h * LANE:(h + 1) * LANE] = (blk(kv, h) + kpe).astype(BF16)
    va_ref[...] = kv[:, 768:1536].astype(BF16)

    hq = _dot(xn, w_ref[:, CE_BQ:CE_BQ + 768])
    scale_b = B_DH ** -0.5
    for b in range(B_HEADS):
        qb_ref[:, b * LANE:(b + 1) * LANE] = (rope(blk(hq, b), 1) * scale_b).astype(BF16)
    hk = _dot(xn, w_ref[:, CE_KVC:CE_KVC + 256])
    for g in range(B_KV_GROUPS):
        kvc_ref[g] = rope(blk(hk, g), 1)
    tpos = (pl.program_id(0) % (SEQ // TM)) * TM + lax.broadcasted_iota(jnp.int32, (TM, LANE), 0)
    onehot = ((lane - 64) == (tpos // SLC_LEN)).astype(F32)
    hk = _dot(xn, w_ref[:, CE_KS:CE_KS + 256])
    for g in range(B_KV_GROUPS):
        ks_ref[:, g * LANE:(g + 1) * LANE] = (rope(blk(hk, g), 1) + onehot).astype(BF16)
    vs_ref[...] = _dot(xn, w_ref[:, CE_VS:CE_VS + 256]).astype(BF16)
    hk = _dot(xn, w_ref[:, CE_KW:CE_KW + 256])
    for g in range(B_KV_GROUPS):
        kw_ref[:, g * LANE:(g + 1) * LANE] = rope(blk(hk, g), 1).astype(BF16)
    vw_ref[...] = _dot(xn, w_ref[:, CE_VW:CE_VW + 256]).astype(BF16)
    gb_ref[...] = jax.nn.sigmoid(_dot(xn, w_ref[:, CE_GATE:CE_GATE + 256]))

    hc = _dot(xn, w_ref[:, CE_CQ:CE_CQ + 512])
    scale_c = C_DH ** -0.5
    for h in range(C_HEADS):
        qc_ref[:, h * LANE:(h + 1) * LANE] = (rope(blk(hc, h), 2) * scale_c).astype(BF16)
    hc = _dot(xn, w_ref[:, CE_CKV:CE_CKV + 512])
    for h in range(C_HEADS):
        kvd_ref[:, h * LANE:(h + 1) * LANE] = rope(blk(hc, h), 2).astype(BF16)


def _proj_call(x, gain, w_e, rope_tab, qn, kvn, wuq, wukv):
    tok = lambda w: pl.BlockSpec((TM, w), lambda i: (i, 0))
    bf = lambda w: jax.ShapeDtypeStruct((N_TOK, w), BF16)
    return pl.pallas_call(
        _proj_kernel,
        grid=(N_TOK // TM,),
        in_specs=[tok(D_MODEL), _const_spec((1, D_MODEL)), _const_spec((D_MODEL, CE)),
                  tok(6 * LANE), _const_spec((1, 256)), _const_spec((1, LANE)),
                  _const_spec((256, 768)), _const_spec((LANE, 1536))],
        out_specs=[tok(768), tok(768), tok(768), tok(768),
                   pl.BlockSpec((B_KV_GROUPS, TM, LANE), lambda i: (0, i, 0)),
                   tok(256), tok(256), tok(256), tok(256), tok(256), tok(512), tok(512)],
        out_shape=[bf(768), bf(768), bf(768), bf(768),
                   jax.ShapeDtypeStruct((B_KV_GROUPS, N_TOK, LANE), F32),
                   bf(256), bf(256), bf(256), bf(256),
                   jax.ShapeDtypeStruct((N_TOK, 256), F32), bf(512), bf(512)],
        compiler_params=pltpu.CompilerParams(
            dimension_semantics=("arbitrary",), vmem_limit_bytes=VMEM_LIMIT),
        name="proj",
    )(x, gain, w_e, rope_tab, qn, kvn, wuq, wukv)


def _first_chunk(q, k, v, mask, m_ref, l_ref, acc_ref):
    s = _dot_nt(q, k)
    if mask is not None:
        s = jnp.where(mask, s, MASKED)
    m = jnp.max(s, axis=-1, keepdims=True)
    p = jnp.exp(s - m)
    m_ref[...] = m
    l_ref[...] = jnp.sum(p, axis=-1, keepdims=True)
    acc_ref[...] = _dot(p.astype(BF16), v)


def _next_chunk(q, k, v, mask, m_ref, l_ref, acc_ref):
    s = _dot_nt(q, k)
    if mask is not None:
        s = jnp.where(mask, s, MASKED)
    m_old = m_ref[...]
    m = jnp.maximum(m_old, jnp.max(s, axis=-1, keepdims=True))
    alpha = jnp.exp(m_old - m)
    p = jnp.exp(s - m)
    m_ref[...] = m
    l_ref[...] = alpha * l_ref[...] + jnp.sum(p, axis=-1, keepdims=True)
    acc_ref[...] = alpha * acc_ref[...] + _dot(p.astype(BF16), v)


def _causal_attention(q, k_ref, v_ref, i, nstack, m_ref, l_ref, acc_ref):
    rows = nstack * TQ
    r = lax.broadcasted_iota(jnp.int32, (rows, TK), 0) & (TQ - 1)
    c = lax.broadcasted_iota(jnp.int32, (rows, TK), 1)
    diag = pl.multiple_of(i * TK, TK)
    _first_chunk(q, k_ref[pl.ds(diag, TK), :], v_ref[pl.ds(diag, TK), :], c <= r,
                 m_ref, l_ref, acc_ref)

    def body(j, carry):
        off = pl.multiple_of(j * TK, TK)
        _next_chunk(q, k_ref[pl.ds(off, TK), :], v_ref[pl.ds(off, TK), :], None,
                    m_ref, l_ref, acc_ref)
        return carry

    lax.fori_loop(0, i, body, 0)
    return acc_ref[...] / l_ref[...]


def _stats_scratch(rows):
    return [pltpu.VMEM((rows, 1), F32), pltpu.VMEM((rows, 1), F32), pltpu.VMEM((rows, LANE), F32)]


def _mla_kernel(q_ref, k_ref, v_ref, o_ref, m_ref, l_ref, acc_ref):
    i = pl.program_id(2)
    o = _causal_attention(q_ref[...], k_ref, v_ref, i, 1, m_ref, l_ref, acc_ref)
    o_ref[...] = o.astype(BF16)


def _mla_call(qa, ka, va):
    qspec = pl.BlockSpec((None, TQ, LANE), lambda b, h, i: (b, i, h))
    kspec = pl.BlockSpec((None, SEQ, LANE), lambda b, h, i: (b, 0, h))
    return pl.pallas_call(
        _mla_kernel,
        grid=(BATCH, A_HEADS, NQ),
        in_specs=[qspec, kspec, kspec],
        out_specs=qspec,
        out_shape=jax.ShapeDtypeStruct((BATCH, SEQ, A_HEADS * LANE), BF16),
        scratch_shapes=_stats_scratch(TQ),
        compiler_params=pltpu.CompilerParams(
            dimension_semantics=("arbitrary",) * 3, vmem_limit_bytes=VMEM_LIMIT),
        name="mla_attn",
    )(qa, ka, va)


def _diff_kernel(q_ref, kv_ref, lqk_ref, sn_ref, o_ref, m_ref, l_ref, acc_ref, *, lam_init):
    i = pl.program_id(2)
    q = q_ref[...]
    lane = lax.broadcasted_iota(jnp.int32, (TQ, LANE), 1)
    zero = jnp.zeros_like(q)
    q2 = jnp.concatenate([jnp.where(lane < C_DH, q, zero),
                          jnp.where((lane >= C_DH) & (lane < 2 * C_DH), q, zero)], axis=0)
    o = _causal_attention(q2, kv_ref, kv_ref, i, 2, m_ref, l_ref, acc_ref)
    lqk = lqk_ref[...]
    lam = (jnp.exp(jnp.sum(lqk[0:1] * lqk[1:2], axis=-1, keepdims=True))
           - jnp.exp(jnp.sum(lqk[2:3] * lqk[3:4], axis=-1, keepdims=True)) + lam_init)
    d = o[0:TQ] - lam * o[TQ:2 * TQ]
    dv = jnp.where(lane >= 64, d, 0.0)
    ms = jnp.sum(dv * dv, axis=-1, keepdims=True) * (1.0 / (2 * C_DH))
    y = dv * lax.rsqrt(ms + EPS) * sn_ref[...] * (1.0 - lam_init)
    o_ref[...] = y.astype(BF16)


def _diff_call(qc, kvd, lqk, sn, lam_init):
    qspec = pl.BlockSpec((None, TQ, LANE), lambda b, h, i: (b, i, h))
    kspec = pl.BlockSpec((None, SEQ, LANE), lambda b, h, i: (b, 0, h))
    return pl.pallas_call(
        functools.partial(_diff_kernel, lam_init=lam_init),
        grid=(BATCH, C_HEADS, NQ),
        in_specs=[qspec, kspec, _const_spec((8, LANE)), _const_spec((1, LANE))],
        out_specs=qspec,
        out_shape=jax.ShapeDtypeStruct((BATCH, SEQ, C_HEADS * LANE), BF16),
        scratch_shapes=_stats_scratch(2 * TQ),
        compiler_params=pltpu.CompilerParams(
            dimension_semantics=("arbitrary",) * 3, vmem_limit_bytes=VMEM_LIMIT),
        name="diff_attn",
    )(qc, kvd, lqk, sn)


def _nsa_kernel(q_ref, x16_ref, ks_ref, vs_ref, kw_ref, vw_ref, gate_ref,
                petop_ref, pebot_ref, w1top_ref, w1bot_ref, w2k_ref, w2v_ref, ovt_ref,
                o_ref, kc_s, vc_s, m_s, l_s, acc_s, m_w, l_w, acc_w):
    i = pl.program_id(2)
    rows = B_HPG * TQ

    @pl.when(i == 0)
    def _():
        x = x16_ref[...]
        yt = _dot((x + petop_ref[...]).astype(BF16), w1top_ref[...])
        yb = _dot((x + pebot_ref[...]).astype(BF16), w1bot_ref[...])
        pre = yt + pltpu.roll(yb, SEQ // CMP_STRIDE - 1, 0)
        act = (pre * jax.nn.sigmoid(pre)).astype(BF16)
        kc_s[...] = _dot(act, w2k_ref[...]).astype(BF16)
        vc_s[...] = _dot(act, w2v_ref[...]).astype(BF16)

    q = q_ref[...]
    q3 = jnp.concatenate([q[:, h * LANE:(h + 1) * LANE] for h in range(B_HPG)], axis=0)

    s = _dot_nt(q3, kc_s[...])
    t_row = i * TQ + (lax.broadcasted_iota(jnp.int32, (rows, 256), 0) & (TQ - 1))
    c_col = lax.broadcasted_iota(jnp.int32, (rows, 256), 1)
    valid = (c_col * CMP_STRIDE + (CMP_LEN - 1)) <= t_row
    smax = jnp.max(jnp.where(valid, s, MASKED), axis=-1, keepdims=True)
    e = jnp.where(valid, jnp.exp(s - smax), 0.0)
    den = jnp.sum(e, axis=-1, keepdims=True)
    p = e / jnp.where(den > 0.0, den, 1.0)
    o_cmp = _dot(p.astype(BF16), vc_s[...])

    psum = p[0:TQ] + p[TQ:2 * TQ] + p[2 * TQ:3 * TQ]
    p_hi = psum.astype(BF16)
    p_lo = (psum - p_hi.astype(F32)).astype(BF16)
    ovt = ovt_ref[...]
    imp = (_dot_nt(ovt, p_hi) + _dot_nt(ovt, p_lo))[64:128]
    m_idx = lax.broadcasted_iota(jnp.int32, (N_SLC, TQ), 0)
    cur = (i * TQ + lax.broadcasted_iota(jnp.int32, (N_SLC, TQ), 1)) // SLC_LEN
    forced = (m_idx == 0) | (m_idx == cur) | (m_idx == cur - 1)
    score = jnp.where(forced, 1e9, jnp.where(m_idx <= cur, imp, -1.0))
    rank = jnp.zeros((N_SLC, TQ), F32)
    for mp in range(N_SLC):
        row = score[mp:mp + 1, :]
        ge = jnp.where(row >= score, 1.0, 0.0)
        gt = jnp.where(row > score, 1.0, 0.0)
        rank = rank + jnp.where(m_idx > mp, ge, gt)
    sel = (rank < float(SLC_TOPN)).astype(F32)
    sel_full = jnp.concatenate([jnp.ones((64, TQ), F32), sel], axis=0)
    bias = ((sel_full.T - 1.0) * SEL_BIG).astype(BF16)
    q_sel = q3 + jnp.concatenate([bias] * B_HPG, axis=0)

    o_slc = _causal_attention(q_sel, ks_ref, vs_ref, i, B_HPG, m_s, l_s, acc_s)

    r = lax.broadcasted_iota(jnp.int32, (rows, TK), 0) & (TQ - 1)
    c = lax.broadcasted_iota(jnp.int32, (rows, TK), 1)
    diag = pl.multiple_of(i * TK, TK)
    _first_chunk(q3, kw_ref[pl.ds(diag, TK), :], vw_ref[pl.ds(diag, TK), :], c <= r,
                 m_w, l_w, acc_w)

    @pl.when(i >= 1)
    def _():
        off = pl.multiple_of((i - 1) * TK, TK)
        _next_chunk(q3, kw_ref[pl.ds(off, TK), :], vw_ref[pl.ds(off, TK), :], None,
                    m_w, l_w, acc_w)

    @pl.when(i >= 2)
    def _():
        off = pl.multiple_of((i - 2) * TK, TK)
        _next_chunk(q3, kw_ref[pl.ds(off, TK), :], vw_ref[pl.ds(off, TK), :],
                    (r + 2 * TK - c) < WINDOW, m_w, l_w, acc_w)

    o_win = acc_w[...] / l_w[...]

    gate = gate_ref[...]
    for h in range(B_HPG):
        sl = slice(h * TQ, (h + 1) * TQ)
        o = (gate[:, 3 * h:3 * h + 1] * o_cmp[sl]
             + gate[:, 3 * h + 1:3 * h + 2] * o_slc[sl]
             + gate[:, 3 * h + 2:3 * h + 3] * o_win[sl])
        o_ref[:, h * LANE:(h + 1) * LANE] = o.astype(BF16)


def _nsa_call(qb, x16, ks, vs, kw, vw, gb, petop, pebot, w1top, w1bot, w2k, w2v, ovt):
    gw = B_HPG * LANE
    qspec = pl.BlockSpec((None, TQ, gw), lambda b, g, i: (b, i, g))
    kspec = pl.BlockSpec((None, SEQ, LANE), lambda b, g, i: (b, 0, g))
    rows = B_HPG * TQ
    return pl.pallas_call(
        _nsa_kernel,
        grid=(BATCH, B_KV_GROUPS, NQ),
        in_specs=[qspec,
                  pl.BlockSpec((None, None, SEQ // 16, 16 * LANE), lambda b, g, i: (g, b, 0, 0)),
                  kspec, kspec, kspec, kspec,
                  pl.BlockSpec((None, TQ, LANE), lambda b, g, i: (b, i, g)),
                  _const_spec((1, 16 * LANE)), _const_spec((1, 16 * LANE)),
                  _const_spec((16 * LANE, LANE)), _const_spec((16 * LANE, LANE)),
                  _const_spec((LANE, LANE)), _const_spec((LANE, LANE)),
                  _const_spec((LANE, 256))],
        out_specs=qspec,
        out_shape=jax.ShapeDtypeStruct((BATCH, SEQ, B_HEADS * LANE), BF16),
        scratch_shapes=[pltpu.VMEM((256, LANE), BF16), pltpu.VMEM((256, LANE), BF16)]
        + _stats_scratch(rows) + _stats_scratch(rows),
        compiler_params=pltpu.CompilerParams(
            dimension_semantics=("arbitrary",) * 3, vmem_limit_bytes=VMEM_LIMIT),
        name="nsa_attn",
    )(qb, x16, ks, vs, kw, vw, gb, petop, pebot, w1top, w1bot, w2k, w2v, ovt)


def _expand_w_in(w):
    a = w[:, :A_COLS]
    b = w[:, A_COLS:A_COLS + B_COLS]
    c = w[:, A_COLS + B_COLS:]
    z = lambda n: jnp.zeros((D_MODEL, n), w.dtype)
    parts = [a[:, 0:192], z(64), a[:, 192:320], z(64), a[:, 320:352], z(32)]
    for g in range(B_KV_GROUPS):
        for hg in range(B_HPG):
            o = g * 192 + hg * 64
            parts += [b[:, o:o + 64], z(64)]
    for g in range(B_KV_GROUPS):
        parts += [b[:, 384 + g * 64:448 + g * 64], b[:, 512 + g * 64:576 + g * 64]]
    for base in (640, 768, 896, 1024):
        for g in range(B_KV_GROUPS):
            parts += [b[:, base + g * 64:base + g * 64 + 64], z(64)]
    for g in range(B_KV_GROUPS):
        parts += [b[:, 1152 + g * 9:1161 + g * 9], z(119)]
    for h in range(C_HEADS):
        parts += [c[:, h * 64:(h + 1) * 64], z(64)]
    for h in range(C_HEADS):
        parts += [c[:, 256 + h * 64:320 + h * 64], c[:, 512 + h * 64:576 + h * 64]]
    return jnp.concatenate(parts, axis=1).astype(BF16)


def _expand_w_out(w):
    z = jnp.zeros((64, D_MODEL), w.dtype)
    parts = []
    for h in range(A_HEADS + B_HEADS):
        parts += [w[h * 64:(h + 1) * 64], z]
    for h in range(C_HEADS):
        parts += [z, w[768 + h * 64:832 + h * 64]]
    return jnp.concatenate(parts, axis=0).astype(BF16)


def _rope_tables(positions):
    pos = positions.astype(F32).reshape(N_TOK, 1)

    def cs(rot):
        inv = 1.0 / (ROPE_THETA ** (jnp.arange(0, rot, 2, dtype=F32) / rot))
        ang = pos * inv
        return jnp.cos(ang), jnp.sin(ang)

    one = lambda n: jnp.ones((N_TOK, n), F32)
    zero = lambda n: jnp.zeros((N_TOK, n), F32)
    ca, sa = cs(A_ROPE)
    cb, sb = cs(B_ROT)
    cc, sc = cs(C_ROT)
    tabs = [
        jnp.concatenate([one(64), ca, ca, one(32)], axis=1),
        jnp.concatenate([zero(64), -sa, sa, zero(32)], axis=1),
        jnp.concatenate([cb, cb, one(112)], axis=1),
        jnp.concatenate([-sb, sb, zero(112)], axis=1),
        jnp.concatenate([cc, cc, one(24), cc, cc, one(24), one(64)], axis=1),
        jnp.concatenate([-sc, sc, zero(24), -sc, sc, zero(24), zero(64)], axis=1),
    ]
    return jnp.concatenate(tabs, axis=1)


def _overlap_t():
    c = jnp.arange(256)[None, :]
    m = jnp.arange(N_SLC)[:, None]
    ov = ((c * CMP_STRIDE < m * SLC_LEN + SLC_LEN)
          & (c * CMP_STRIDE + CMP_LEN - 1 >= m * SLC_LEN) & (c < N_CMP))
    return jnp.concatenate([jnp.zeros((64, 256), BF16), ov.astype(BF16)], axis=0)


def _pad_lanes(v, width):
    return jnp.pad(v.reshape(1, -1), ((0, 0), (0, width - v.shape[-1])))


def kernel(x, positions, ffn1_norm, ffn1_wg, ffn1_wu, ffn1_wd, mix_norm, w_in, mla_q_norm, mla_kv_norm, mla_w_uq, mla_w_ukv, nsa_pe_k, nsa_pe_v, nsa_phi_k1, nsa_phi_k2, nsa_phi_v1, nsa_phi_v2, diff_lq1, diff_lk1, diff_lq2, diff_lk2, diff_sub_norm, w_out, ffn2_norm, ffn2_wg, ffn2_wu, ffn2_wd, final_norm):
    xf = x.reshape(N_TOK, D_MODEL)
    rope_tab = _rope_tables(positions)
    ovt = _overlap_t()

    def ffn_weights(wg, wu, wd):
        wgu = jnp.stack([wg.reshape(D_MODEL, NFC, TF), wu.reshape(D_MODEL, NFC, TF)], axis=2)
        return wgu.reshape(D_MODEL, 2 * D_FF).astype(BF16), wd.astype(BF16)

    for l in range(DEPTH):
        wgu1, wd1 = ffn_weights(ffn1_wg[l], ffn1_wu[l], ffn1_wd[l])
        wgu2, wd2 = ffn_weights(ffn2_wg[l], ffn2_wu[l], ffn2_wd[l])
        w_e = _expand_w_in(w_in[l])
        w_o = _expand_w_out(w_out[l])
        wuq = jnp.pad(mla_w_uq[l].reshape(A_Q_RANK, A_HEADS, A_NOPE + A_ROPE),
                      ((0, 256 - A_Q_RANK), (0, 0), (0, LANE - A_NOPE - A_ROPE)))
        wuq = wuq.reshape(256, A_HEADS * LANE).astype(BF16)
        wkv = mla_w_ukv[l].reshape(A_KV_RANK, A_HEADS, A_NOPE + A_V)
        padv = ((0, 0), (0, 0), (0, LANE - 64))
        wukv = jnp.concatenate(
            [jnp.pad(wkv[:, :, :A_NOPE], padv).reshape(A_KV_RANK, A_HEADS * LANE),
             jnp.pad(wkv[:, :, A_NOPE:], padv).reshape(A_KV_RANK, A_HEADS * LANE)],
            axis=1).astype(BF16)
        qn = _pad_lanes(mla_q_norm[l], 256)
        kvn = _pad_lanes(mla_kv_norm[l], LANE)

        k1 = nsa_phi_k1[l].reshape(CMP_LEN, B_DH, B_DH)
        v1 = nsa_phi_v1[l].reshape(CMP_LEN, B_DH, B_DH)
        z3 = jnp.zeros_like(k1)
        w1 = jnp.concatenate([jnp.concatenate([k1, z3], axis=2),
                              jnp.concatenate([z3, v1], axis=2)], axis=1)
        w1top = w1[:16].reshape(16 * LANE, LANE).astype(BF16)
        w1bot = w1[16:].reshape(16 * LANE, LANE).astype(BF16)
        pe = jnp.concatenate([nsa_pe_k[l], nsa_pe_v[l]], axis=1)
        petop = pe[:16].reshape(1, 16 * LANE)
        pebot = pe[16:].reshape(1, 16 * LANE)
        w2k = jnp.zeros((LANE, LANE), F32).at[0:64, 0:64].set(nsa_phi_k2[l]).astype(BF16)
        w2v = jnp.zeros((LANE, LANE), F32).at[64:128, 0:64].set(nsa_phi_v2[l]).astype(BF16)

        lqk = jnp.pad(jnp.stack([diff_lq1[l], diff_lk1[l], diff_lq2[l], diff_lk2[l]], axis=0),
                      ((0, 4), (0, LANE - C_DH)))
        sn = jnp.pad(diff_sub_norm[l].reshape(1, 2 * C_DH), ((0, 0), (64, 0)))
        lam_init = 0.8 - 0.6 * math.exp(-0.3 * l)

        xf = _ffn_call(xf, ffn1_norm[l].reshape(1, D_MODEL), wgu1, wd1)
        (qa, ka, va, qb, kvc, ks, vs, kw, vw, gb, qc, kvd) = _proj_call(
            xf, mix_norm[l].reshape(1, D_MODEL), w_e, rope_tab, qn, kvn, wuq, wukv)
        b3 = lambda t: t.reshape(BATCH, SEQ, t.shape[-1])
        oa = _mla_call(b3(qa), b3(ka), b3(va))
        x16 = kvc.reshape(B_KV_GROUPS, BATCH, SEQ // 16, 16 * LANE)
        ob = _nsa_call(b3(qb), x16, b3(ks), b3(vs), b3(kw), b3(vw), b3(gb),
                       petop, pebot, w1top, w1bot, w2k, w2v, ovt)
        oc = _diff_call(b3(qc), b3(kvd), lqk, sn, lam_init)
        flat = lambda t: t.reshape(N_TOK, t.shape[-1])
        fg = final_norm.reshape(1, D_MODEL) if l == DEPTH - 1 else None
        xf = _ffn_call(xf, ffn2_norm[l].reshape(1, D_MODEL), wgu2, wd2,
                       mix=(flat(oa), flat(ob), flat(oc), w_o), final_gain=fg)
    return xf.reshape(BATCH, SEQ, D_MODEL)
```

```python
import functools
import math

import jax
import jax.numpy as jnp
from jax import lax
from jax.experimental import pallas as pl
from jax.experimental.pallas import tpu as pltpu

F32 = jnp.float32
BF16 = jnp.bfloat16

D_MODEL = 1024
BATCH = 4
SEQ = 4096
DEPTH = 2
N_TOK = BATCH * SEQ
EPS = 1e-6
ROPE_THETA = 500000.0
D_FF = 2816

A_HEADS = 6
A_Q_RANK = 192
A_KV_RANK = 128
A_NOPE = 64
A_ROPE = 32
A_V = 64

B_HEADS = 6
B_KV_GROUPS = 2
B_HPG = 3
B_DH = 64
B_ROT = 16
CMP_LEN = 32
CMP_STRIDE = 16
N_CMP = (SEQ - CMP_LEN) // CMP_STRIDE + 1
N_CMP_PAD = SEQ // CMP_STRIDE
SLC_LEN = 64
N_SLC = SEQ // SLC_LEN
SLC_TOPN = 16
WINDOW = 512

C_HEADS = 4
C_DH = 32
C_ROT = 8

A_COLS = A_Q_RANK + A_KV_RANK + A_ROPE
B_COLS = B_HEADS * B_DH + 6 * B_KV_GROUPS * B_DH + 3 * B_HEADS
C_COLS = 3 * C_HEADS * 2 * C_DH

LANE = 128
VROW = 64
MASKED = -1e30
SEL_BIG = 1e9
LOG2E = 1.0 / math.log(2.0)

TM = 512
TF = 256
NFC = D_FF // TF
TQ = 512
TK = 512
NQ = SEQ // TQ
NCH = N_TOK // TK
GATE_ROWS = 16

CE_A = 0
CE_BQ = 512
CE_KVC = 1280
CE_KS = 1536
CE_VS = 1792
CE_KW = 2048
CE_VW = 2304
CE_GATE = 2560
CE_CQ = 2816
CE_CKV = 3328
CE = 3840

VMEM_LIMIT = 56 * 1024 * 1024


def _const_spec(shape):
    nd = len(shape)
    return pl.BlockSpec(shape, lambda *_: (0,) * nd, pipeline_mode=pl.Buffered(1))


def _rms(x, g, n):
    ms = jnp.sum(x * x, axis=-1, keepdims=True) * (1.0 / n)
    return x * lax.rsqrt(ms + EPS) * g


def _dot(a, b):
    return jnp.dot(a, b, preferred_element_type=F32)


def _ffn_kernel(*refs, mix, final):
    it = iter(refs)
    x_ref = next(it)
    if mix:
        oa_ref, ob_ref, oc_ref, wo_ref = next(it), next(it), next(it), next(it)
    g_ref, wgu_ref, wd_ref = next(it), next(it), next(it)
    if final:
        fg_ref = next(it)
    o_ref = next(it)
    acc_ref = next(it)

    x = x_ref[...]
    if mix:
        x = x + _dot(oa_ref[...], wo_ref[0:768, :])
        x = x + _dot(ob_ref[...], wo_ref[768:1536, :])
        x = x + _dot(oc_ref[...], wo_ref[1536:2048, :])
    xn = _rms(x, g_ref[...], D_MODEL).astype(BF16)
    for c in range(NFC):
        gu = _dot(xn, wgu_ref[:, c * 2 * TF:(c + 1) * 2 * TF])
        g = gu[:, :TF]
        u = gu[:, TF:]
        h = (g * jax.nn.sigmoid(g) * u).astype(BF16)
        d = _dot(h, wd_ref[c * TF:(c + 1) * TF, :])
        if c == 0:
            acc_ref[...] = d
        else:
            acc_ref[...] += d
    y = x + 0.5 * acc_ref[...]
    if final:
        y = _rms(y, fg_ref[...], D_MODEL)
    o_ref[...] = y


def _ffn_call(x, gain, wgu, wd, mix=None, final_gain=None):
    tok = lambda w: pl.BlockSpec((TM, w), lambda i: (i, 0))
    args = [x]
    specs = [tok(D_MODEL)]
    if mix is not None:
        oa, ob, oc, wo = mix
        args += [oa, ob, oc, wo]
        specs += [tok(768), tok(768), tok(512), _const_spec((2048, D_MODEL))]
    args += [gain, wgu, wd]
    specs += [_const_spec((1, D_MODEL)), _const_spec((D_MODEL, 2 * D_FF)),
              _const_spec((D_FF, D_MODEL))]
    if final_gain is not None:
        args.append(final_gain)
        specs.append(_const_spec((1, D_MODEL)))
    kern = functools.partial(_ffn_kernel, mix=mix is not None, final=final_gain is not None)
    return pl.pallas_call(
        kern,
        grid=(N_TOK // TM,),
        in_specs=specs,
        out_specs=tok(D_MODEL),
        out_shape=jax.ShapeDtypeStruct((N_TOK, D_MODEL), F32),
        scratch_shapes=[pltpu.VMEM((TM, D_MODEL), F32)],
        compiler_params=pltpu.CompilerParams(
            dimension_semantics=("arbitrary",), vmem_limit_bytes=VMEM_LIMIT),
        name="ffn",
    )(*args)


ROPE_HALF = (16, 8, 4)


def _proj_kernel(x_ref, g_ref, w_ref, rope_ref, qn_ref, kvn_ref, wuq_ref, wukv_ref,
                 qat_ref, ka_ref, vat_ref, qbt_ref, kvc_ref, ks_ref, vst_ref, kw_ref, vwt_ref,
                 gt_ref, qct_ref, kd_ref, vct_ref):
    xn = _rms(x_ref[...], g_ref[...], D_MODEL).astype(BF16)
    lane = lax.broadcasted_iota(jnp.int32, (TM, LANE), 1)
    second = (
        (lane >= 80) & (lane < 96),
        (lane >= 8) & (lane < 16),
        ((lane >= 4) & (lane < 8)) | ((lane >= 36) & (lane < 40)),
    )
    ones_row = (lane == VROW).astype(F32)

    def rope(v, typ):
        cos = rope_ref[:, 2 * typ * LANE:(2 * typ + 1) * LANE]
        sin = rope_ref[:, (2 * typ + 1) * LANE:(2 * typ + 2) * LANE]
        r = ROPE_HALF[typ]
        partner = jnp.where(second[typ], pltpu.roll(v, r, 1), pltpu.roll(v, LANE - r, 1))
        return v * cos + partner * sin

    def blk(v, b):
        return v[:, b * LANE:(b + 1) * LANE]

    def tr(v):
        return v.T.astype(BF16)

    ha = _dot(xn, w_ref[:, CE_A:CE_A + 512])
    cqn = _rms(ha[:, 0:256], qn_ref[...], A_Q_RANK).astype(BF16)
    ckvn = _rms(ha[:, 256:384], kvn_ref[...], A_KV_RANK).astype(BF16)
    kpe = rope(ha[:, 384:512], 0)
    q = _dot(cqn, wuq_ref[...])
    kv = _dot(ckvn, wukv_ref[...])
    scale_a = (A_NOPE + A_ROPE) ** -0.5 * LOG2E
    for h in range(A_HEADS):
        qat_ref[h, 0] = tr(rope(blk(q, h), 0) * scale_a)
        ka_ref[:, h * LANE:(h + 1) * LANE] = (blk(kv, h) + kpe).astype(BF16)
        vat_ref[h, 0] = tr(blk(kv, A_HEADS + h) + ones_row)

    hq = _dot(xn, w_ref[:, CE_BQ:CE_BQ + 768])
    scale_b = B_DH ** -0.5 * LOG2E
    for b in range(B_HEADS):
        qbt_ref[b, 0] = tr(rope(blk(hq, b), 1) * scale_b)
    hk = _dot(xn, w_ref[:, CE_KVC:CE_KVC + 256])
    for g in range(B_KV_GROUPS):
        kvc_ref[g] = rope(blk(hk, g), 1)
    tpos = (pl.program_id(0) % (SEQ // TM)) * TM + lax.broadcasted_iota(jnp.int32, (TM, LANE), 0)
    onehot = ((lane - 64) == (tpos // SLC_LEN)).astype(F32)
    hk = _dot(xn, w_ref[:, CE_KS:CE_KS + 256])
    for g in range(B_KV_GROUPS):
        ks_ref[:, g * LANE:(g + 1) * LANE] = (rope(blk(hk, g), 1) + onehot).astype(BF16)
    hk = _dot(xn, w_ref[:, CE_VS:CE_VS + 256])
    for g in range(B_KV_GROUPS):
        vst_ref[g, 0] = tr(blk(hk, g) + ones_row)
    hk = _dot(xn, w_ref[:, CE_KW:CE_KW + 256])
    for g in range(B_KV_GROUPS):
        kw_ref[:, g * LANE:(g + 1) * LANE] = rope(blk(hk, g), 1).astype(BF16)
    hk = _dot(xn, w_ref[:, CE_VW:CE_VW + 256])
    for g in range(B_KV_GROUPS):
        vwt_ref[g, 0] = tr(blk(hk, g) + ones_row)
    hk = jax.nn.sigmoid(_dot(xn, w_ref[:, CE_GATE:CE_GATE + 256]))
    for g in range(B_KV_GROUPS):
        gt_ref[g, 0] = blk(hk, g).T[0:GATE_ROWS, :]

    hc = _dot(xn, w_ref[:, CE_CQ:CE_CQ + 512])
    scale_c = C_DH ** -0.5 * LOG2E
    for h in range(C_HEADS):
        qct_ref[h, 0] = tr(rope(blk(hc, h), 2) * scale_c)
    hc = _dot(xn, w_ref[:, CE_CKV:CE_CKV + 512])
    for h in range(C_HEADS):
        kv_blk = rope(blk(hc, h), 2)
        kd_ref[:, h * LANE:(h + 1) * LANE] = kv_blk.astype(BF16)
        v_lo = jnp.where(lane < 64, pltpu.roll(kv_blk, 64, 1), ones_row)
        vct_ref[h, 0] = tr(v_lo)


def _proj_call(x, gain, w_e, rope_tab, qn, kvn, wuq, wukv):
    tok = lambda w: pl.BlockSpec((TM, w), lambda i: (i, 0))
    bf = lambda w: jax.ShapeDtypeStruct((N_TOK, w), BF16)
    tspec = lambda nb, rows: pl.BlockSpec((nb, 1, rows, TM), lambda i: (0, i, 0, 0))
    tshape = lambda nb, rows, dt: jax.ShapeDtypeStruct((nb, NCH, rows, TM), dt)
    return pl.pallas_call(
        _proj_kernel,
        grid=(N_TOK // TM,),
        in_specs=[tok(D_MODEL), _const_spec((1, D_MODEL)), _const_spec((D_MODEL, CE)),
                  tok(6 * LANE), _const_spec((1, 256)), _const_spec((1, LANE)),
                  _const_spec((256, 768)), _const_spec((LANE, 1536))],
        out_specs=[tspec(6, LANE), tok(768), tspec(6, LANE), tspec(6, LANE),
                   pl.BlockSpec((B_KV_GROUPS, TM, LANE), lambda i: (0, i, 0)),
                   tok(256), tspec(2, LANE), tok(256), tspec(2, LANE),
                   tspec(2, GATE_ROWS), tspec(4, LANE), tok(512), tspec(4, LANE)],
        out_shape=[tshape(6, LANE, BF16), bf(768), tshape(6, LANE, BF16), tshape(6, LANE, BF16),
                   jax.ShapeDtypeStruct((B_KV_GROUPS, N_TOK, LANE), F32),
                   bf(256), tshape(2, LANE, BF16), bf(256), tshape(2, LANE, BF16),
                   tshape(2, GATE_ROWS, F32), tshape(4, LANE, BF16), bf(512),
                   tshape(4, LANE, BF16)],
        compiler_params=pltpu.CompilerParams(
            dimension_semantics=("arbitrary",), vmem_limit_bytes=VMEM_LIMIT),
        name="proj",
    )(x, gain, w_e, rope_tab, qn, kvn, wuq, wukv)


def _first_chunk(qt, k, vt, mask, m_ref, acc_ref):
    s = _dot(k, qt)
    if mask is not None:
        s = jnp.where(mask, s, MASKED)
    m = jnp.max(s, axis=0, keepdims=True)
    p = jnp.exp2(s - m).astype(BF16)
    m_ref[...] = m
    acc_ref[...] = _dot(vt, p)


def _next_chunk(qt, k, vt, mask, m_ref, acc_ref):
    s = _dot(k, qt)
    if mask is not None:
        s = jnp.where(mask, s, MASKED)
    m_old = m_ref[...]
    m = jnp.maximum(m_old, jnp.max(s, axis=0, keepdims=True))
    alpha = jnp.exp2(m_old - m)
    p = jnp.exp2(s - m).astype(BF16)
    m_ref[...] = m
    acc_ref[...] = alpha * acc_ref[...] + _dot(vt, p)


def _causal_attention(qt, k_ref, vt_ref, i, m_ref, acc_ref):
    r = qt.shape[1]
    kk = lax.broadcasted_iota(jnp.int32, (TK, r), 0)
    tt = lax.broadcasted_iota(jnp.int32, (TK, r), 1) & (TQ - 1)
    diag = pl.multiple_of(i * TK, TK)
    _first_chunk(qt, k_ref[pl.ds(diag, TK), :], vt_ref[i], kk <= tt, m_ref, acc_ref)

    def body(j, carry):
        off = pl.multiple_of(j * TK, TK)
        _next_chunk(qt, k_ref[pl.ds(off, TK), :], vt_ref[j], None, m_ref, acc_ref)
        return carry

    lax.fori_loop(0, i, body, 0)
    return acc_ref[...]


def _normalise(acc):
    row = lax.broadcasted_iota(jnp.int32, acc.shape, 0)
    return jnp.where(row < VROW, acc / acc[VROW:VROW + 1, :], 0.0)


def _stats_scratch(r):
    return [pltpu.VMEM((1, r), F32), pltpu.VMEM((LANE, r), F32)]


def _qt_spec(nb):
    return pl.BlockSpec((nb, None, LANE, TQ), lambda b, h, i: (h, b * NQ + i, 0, 0))


_K_SPEC = pl.BlockSpec((None, SEQ, LANE), lambda b, h, i: (b, 0, h))
_VT_SPEC = pl.BlockSpec((None, None, NQ, LANE, TK), lambda b, h, i: (h, b, 0, 0, 0))


def _attn_params():
    return pltpu.CompilerParams(dimension_semantics=("arbitrary",) * 3,
                                vmem_limit_bytes=VMEM_LIMIT)


def _mla_kernel(qt_ref, k_ref, vt_ref, o_ref, m_ref, acc_ref):
    i = pl.program_id(2)
    acc = _causal_attention(qt_ref[0], k_ref, vt_ref, i, m_ref, acc_ref)
    o_ref[...] = _normalise(acc).T.astype(BF16)


def _mla_call(qat, ka, vat):
    return pl.pallas_call(
        _mla_kernel,
        grid=(BATCH, A_HEADS, NQ),
        in_specs=[_qt_spec(1), _K_SPEC, _VT_SPEC],
        out_specs=pl.BlockSpec((None, TQ, LANE), lambda b, h, i: (b, i, h)),
        out_shape=jax.ShapeDtypeStruct((BATCH, SEQ, A_HEADS * LANE), BF16),
        scratch_shapes=_stats_scratch(TQ),
        compiler_params=_attn_params(),
        name="mla_attn",
    )(qat, ka, vat)


def _diff_kernel(qt_ref, k_ref, vt_ref, lqk_ref, sn_ref, o_ref, m_ref, acc_ref, *, lam_init):
    i = pl.program_id(2)
    qt = qt_ref[0]
    row = lax.broadcasted_iota(jnp.int32, (LANE, TQ), 0)
    zero = jnp.zeros_like(qt)
    q2 = jnp.concatenate([jnp.where(row < C_DH, qt, zero),
                          jnp.where((row >= C_DH) & (row < 2 * C_DH), qt, zero)], axis=1)
    o = _normalise(_causal_attention(q2, k_ref, vt_ref, i, m_ref, acc_ref))
    lqk = lqk_ref[...]
    lam = (jnp.exp(jnp.sum(lqk[0:1] * lqk[1:2], axis=-1, keepdims=True))
           - jnp.exp(jnp.sum(lqk[2:3] * lqk[3:4], axis=-1, keepdims=True)) + lam_init)
    d = (o[:, 0:TQ] - lam * o[:, TQ:2 * TQ]).T
    ms = jnp.sum(d * d, axis=-1, keepdims=True) * (1.0 / (2 * C_DH))
    y = d * lax.rsqrt(ms + EPS) * sn_ref[...] * (1.0 - lam_init)
    o_ref[...] = y.astype(BF16)


def _diff_call(qct, kd, vct, lqk, sn, lam_init):
    return pl.pallas_call(
        functools.partial(_diff_kernel, lam_init=lam_init),
        grid=(BATCH, C_HEADS, NQ),
        in_specs=[_qt_spec(1), _K_SPEC, _VT_SPEC, _const_spec((8, LANE)), _const_spec((1, LANE))],
        out_specs=pl.BlockSpec((None, TQ, LANE), lambda b, h, i: (b, i, h)),
        out_shape=jax.ShapeDtypeStruct((BATCH, SEQ, C_HEADS * LANE), BF16),
        scratch_shapes=_stats_scratch(2 * TQ),
        compiler_params=_attn_params(),
        name="diff_attn",
    )(qct, kd, vct, lqk, sn)


def _nsa_kernel(qt_ref, x16_ref, ks_ref, vst_ref, kw_ref, vwt_ref, gt_ref,
                petop_ref, pebot_ref, w1top_ref, w1bot_ref, w2k_ref, w2v_ref, ovt_ref,
                o_ref, kc_s, vct_s, m_s, acc_s, m_w, acc_w):
    i = pl.program_id(2)
    r3 = B_HPG * TQ

    @pl.when(i == 0)
    def _():
        x = x16_ref[...]
        yt = _dot((x + petop_ref[...]).astype(BF16), w1top_ref[...])
        yb = _dot((x + pebot_ref[...]).astype(BF16), w1bot_ref[...])
        pre = yt + pltpu.roll(yb, N_CMP_PAD - 1, 0)
        act = (pre * jax.nn.sigmoid(pre)).astype(BF16)
        kc_s[...] = _dot(act, w2k_ref[...]).astype(BF16)
        vct_s[...] = _dot(act, w2v_ref[...]).T.astype(BF16)

    q3 = jnp.concatenate([qt_ref[h] for h in range(B_HPG)], axis=1)

    s = _dot(kc_s[...], q3)
    t_col = i * TQ + (lax.broadcasted_iota(jnp.int32, (N_CMP_PAD, r3), 1) & (TQ - 1))
    c_row = lax.broadcasted_iota(jnp.int32, (N_CMP_PAD, r3), 0)
    valid = (c_row * CMP_STRIDE + (CMP_LEN - 1)) <= t_col
    smax = jnp.max(jnp.where(valid, s, MASKED), axis=0, keepdims=True)
    e = jnp.where(valid, jnp.exp2(s - smax), 0.0)
    den = jnp.sum(e, axis=0, keepdims=True)
    p = e / jnp.where(den > 0.0, den, 1.0)
    o_cmp = _dot(vct_s[...], p.astype(BF16))

    psum = p[:, 0:TQ] + p[:, TQ:2 * TQ] + p[:, 2 * TQ:3 * TQ]
    p_hi = psum.astype(BF16)
    p_lo = (psum - p_hi.astype(F32)).astype(BF16)
    ovt = ovt_ref[...]
    imp = (_dot(ovt, p_hi) + _dot(ovt, p_lo))[64:128]
    m_idx = lax.broadcasted_iota(jnp.int32, (N_SLC, TQ), 0)
    cur = (i * TQ + lax.broadcasted_iota(jnp.int32, (N_SLC, TQ), 1)) // SLC_LEN
    forced = (m_idx == 0) | (m_idx == cur) | (m_idx == cur - 1)
    score = jnp.where(forced, 1e9, jnp.where(m_idx <= cur, imp, -1.0))
    rank = jnp.zeros((N_SLC, TQ), F32)
    for mp in range(N_SLC):
        srow = score[mp:mp + 1, :]
        ge = jnp.where(srow >= score, 1.0, 0.0)
        gt = jnp.where(srow > score, 1.0, 0.0)
        rank = rank + jnp.where(m_idx > mp, ge, gt)
    bias = jnp.where(rank < float(SLC_TOPN), 0.0, -SEL_BIG)
    bias = jnp.concatenate([jnp.zeros((64, TQ), F32), bias], axis=0).astype(BF16)
    q_sel = q3 + jnp.concatenate([bias] * B_HPG, axis=1)

    o_slc = _normalise(_causal_attention(q_sel, ks_ref, vst_ref, i, m_s, acc_s))

    kk = lax.broadcasted_iota(jnp.int32, (TK, r3), 0)
    tt = lax.broadcasted_iota(jnp.int32, (TK, r3), 1) & (TQ - 1)
    diag = pl.multiple_of(i * TK, TK)
    _first_chunk(q3, kw_ref[pl.ds(diag, TK), :], vwt_ref[i], kk <= tt, m_w, acc_w)

    @pl.when(i >= 1)
    def _():
        off = pl.multiple_of((i - 1) * TK, TK)
        _next_chunk(q3, kw_ref[pl.ds(off, TK), :], vwt_ref[i - 1], tt < kk, m_w, acc_w)

    o_win = _normalise(acc_w[...])

    gate = gt_ref[...]
    for h in range(B_HPG):
        sl = slice(h * TQ, (h + 1) * TQ)
        o = (gate[3 * h:3 * h + 1] * o_cmp[:, sl]
             + gate[3 * h + 1:3 * h + 2] * o_slc[:, sl]
             + gate[3 * h + 2:3 * h + 3] * o_win[:, sl])
        o_ref[:, h * LANE:(h + 1) * LANE] = o.T.astype(BF16)


def _nsa_call(qbt, x16, ks, vst, kw, vwt, gt, petop, pebot, w1top, w1bot, w2k, w2v, ovt):
    r3 = B_HPG * TQ
    return pl.pallas_call(
        _nsa_kernel,
        grid=(BATCH, B_KV_GROUPS, NQ),
        in_specs=[_qt_spec(B_HPG),
                  pl.BlockSpec((None, None, N_CMP_PAD, 16 * LANE), lambda b, g, i: (g, b, 0, 0)),
                  _K_SPEC, _VT_SPEC, _K_SPEC, _VT_SPEC,
                  pl.BlockSpec((None, None, GATE_ROWS, TQ), lambda b, g, i: (g, b * NQ + i, 0, 0)),
                  _const_spec((1, 16 * LANE)), _const_spec((1, 16 * LANE)),
                  _const_spec((16 * LANE, LANE)), _const_spec((16 * LANE, LANE)),
                  _const_spec((LANE, LANE)), _const_spec((LANE, LANE)),
                  _const_spec((LANE, N_CMP_PAD))],
        out_specs=pl.BlockSpec((None, TQ, B_HPG * LANE), lambda b, g, i: (b, i, g)),
        out_shape=jax.ShapeDtypeStruct((BATCH, SEQ, B_HEADS * LANE), BF16),
        scratch_shapes=[pltpu.VMEM((N_CMP_PAD, LANE), BF16), pltpu.VMEM((LANE, N_CMP_PAD), BF16)]
        + _stats_scratch(r3) + _stats_scratch(r3),
        compiler_params=_attn_params(),
        name="nsa_attn",
    )(qbt, x16, ks, vst, kw, vwt, gt, petop, pebot, w1top, w1bot, w2k, w2v, ovt)


def _expand_w_in(w):
    a = w[:, :A_COLS]
    b = w[:, A_COLS:A_COLS + B_COLS]
    c = w[:, A_COLS + B_COLS:]
    z = lambda n: jnp.zeros((D_MODEL, n), w.dtype)
    parts = [a[:, 0:192], z(64), a[:, 192:320], z(64), a[:, 320:352], z(32)]
    for g in range(B_KV_GROUPS):
        for hg in range(B_HPG):
            o = g * 192 + hg * 64
            parts += [b[:, o:o + 64], z(64)]
    for g in range(B_KV_GROUPS):
        parts += [b[:, 384 + g * 64:448 + g * 64], b[:, 512 + g * 64:576 + g * 64]]
    for base in (640, 768, 896, 1024):
        for g in range(B_KV_GROUPS):
            parts += [b[:, base + g * 64:base + g * 64 + 64], z(64)]
    for g in range(B_KV_GROUPS):
        parts += [b[:, 1152 + g * 9:1161 + g * 9], z(119)]
    for h in range(C_HEADS):
        parts += [c[:, h * 64:(h + 1) * 64], z(64)]
    for h in range(C_HEADS):
        parts += [c[:, 256 + h * 64:320 + h * 64], c[:, 512 + h * 64:576 + h * 64]]
    return jnp.concatenate(parts, axis=1).astype(BF16)


def _expand_w_out(w):
    z = jnp.zeros((64, D_MODEL), w.dtype)
    parts = []
    for h in range(A_HEADS + B_HEADS + C_HEADS):
        parts += [w[h * 64:(h + 1) * 64], z]
    return jnp.concatenate(parts, axis=0).astype(BF16)


def _rope_tables(positions):
    pos = positions.astype(F32).reshape(N_TOK, 1)

    def cs(rot):
        inv = 1.0 / (ROPE_THETA ** (jnp.arange(0, rot, 2, dtype=F32) / rot))
        ang = pos * inv
        return jnp.cos(ang), jnp.sin(ang)

    one = lambda n: jnp.ones((N_TOK, n), F32)
    zero = lambda n: jnp.zeros((N_TOK, n), F32)
    ca, sa = cs(A_ROPE)
    cb, sb = cs(B_ROT)
    cc, sc = cs(C_ROT)
    tabs = [
        jnp.concatenate([one(64), ca, ca, one(32)], axis=1),
        jnp.concatenate([zero(64), -sa, sa, zero(32)], axis=1),
        jnp.concatenate([cb, cb, one(112)], axis=1),
        jnp.concatenate([-sb, sb, zero(112)], axis=1),
        jnp.concatenate([cc, cc, one(24), cc, cc, one(24), one(64)], axis=1),
        jnp.concatenate([-sc, sc, zero(24), -sc, sc, zero(24), zero(64)], axis=1),
    ]
    return jnp.concatenate(tabs, axis=1)


def _overlap_t():
    c = jnp.arange(N_CMP_PAD)[None, :]
    m = jnp.arange(N_SLC)[:, None]
    ov = ((c * CMP_STRIDE < m * SLC_LEN + SLC_LEN)
          & (c * CMP_STRIDE + CMP_LEN - 1 >= m * SLC_LEN) & (c < N_CMP))
    return jnp.concatenate([jnp.zeros((64, N_CMP_PAD), BF16), ov.astype(BF16)], axis=0)


def _pad_lanes(v, width):
    return jnp.pad(v.reshape(1, -1), ((0, 0), (0, width - v.shape[-1])))


def kernel(x, positions, ffn1_norm, ffn1_wg, ffn1_wu, ffn1_wd, mix_norm, w_in, mla_q_norm, mla_kv_norm, mla_w_uq, mla_w_ukv, nsa_pe_k, nsa_pe_v, nsa_phi_k1, nsa_phi_k2, nsa_phi_v1, nsa_phi_v2, diff_lq1, diff_lk1, diff_lq2, diff_lk2, diff_sub_norm, w_out, ffn2_norm, ffn2_wg, ffn2_wu, ffn2_wd, final_norm):
    xf = x.reshape(N_TOK, D_MODEL)
    rope_tab = _rope_tables(positions)
    ovt = _overlap_t()

    def ffn_weights(wg, wu, wd):
        wgu = jnp.stack([wg.reshape(D_MODEL, NFC, TF), wu.reshape(D_MODEL, NFC, TF)], axis=2)
        return wgu.reshape(D_MODEL, 2 * D_FF).astype(BF16), wd.astype(BF16)

    for l in range(DEPTH):
        wgu1, wd1 = ffn_weights(ffn1_wg[l], ffn1_wu[l], ffn1_wd[l])
        wgu2, wd2 = ffn_weights(ffn2_wg[l], ffn2_wu[l], ffn2_wd[l])
        w_e = _expand_w_in(w_in[l])
        w_o = _expand_w_out(w_out[l])
        wuq = jnp.pad(mla_w_uq[l].reshape(A_Q_RANK, A_HEADS, A_NOPE + A_ROPE),
                      ((0, 256 - A_Q_RANK), (0, 0), (0, LANE - A_NOPE - A_ROPE)))
        wuq = wuq.reshape(256, A_HEADS * LANE).astype(BF16)
        wkv = mla_w_ukv[l].reshape(A_KV_RANK, A_HEADS, A_NOPE + A_V)
        padv = ((0, 0), (0, 0), (0, LANE - 64))
        wukv = jnp.concatenate(
            [jnp.pad(wkv[:, :, :A_NOPE], padv).reshape(A_KV_RANK, A_HEADS * LANE),
             jnp.pad(wkv[:, :, A_NOPE:], padv).reshape(A_KV_RANK, A_HEADS * LANE)],
            axis=1).astype(BF16)
        qn = _pad_lanes(mla_q_norm[l], 256)
        kvn = _pad_lanes(mla_kv_norm[l], LANE)

        k1 = nsa_phi_k1[l].reshape(CMP_LEN, B_DH, B_DH)
        v1 = nsa_phi_v1[l].reshape(CMP_LEN, B_DH, B_DH)
        z3 = jnp.zeros_like(k1)
        w1 = jnp.concatenate([jnp.concatenate([k1, z3], axis=2),
                              jnp.concatenate([z3, v1], axis=2)], axis=1)
        w1top = w1[:16].reshape(16 * LANE, LANE).astype(BF16)
        w1bot = w1[16:].reshape(16 * LANE, LANE).astype(BF16)
        pe = jnp.concatenate([nsa_pe_k[l], nsa_pe_v[l]], axis=1)
        petop = pe[:16].reshape(1, 16 * LANE)
        pebot = pe[16:].reshape(1, 16 * LANE)
        w2k = jnp.zeros((LANE, LANE), F32).at[0:64, 0:64].set(nsa_phi_k2[l]).astype(BF16)
        w2v = jnp.zeros((LANE, LANE), F32).at[64:128, 0:64].set(nsa_phi_v2[l]).astype(BF16)

        lqk = jnp.pad(jnp.stack([diff_lq1[l], diff_lk1[l], diff_lq2[l], diff_lk2[l]], axis=0),
                      ((0, 4), (0, LANE - C_DH)))
        sn = _pad_lanes(diff_sub_norm[l], LANE)
        lam_init = 0.8 - 0.6 * math.exp(-0.3 * l)

        xf = _ffn_call(xf, ffn1_norm[l].reshape(1, D_MODEL), wgu1, wd1)
        (qat, ka, vat, qbt, kvc, ks, vst, kw, vwt, gt, qct, kd, vct) = _proj_call(
            xf, mix_norm[l].reshape(1, D_MODEL), w_e, rope_tab, qn, kvn, wuq, wukv)
        b3 = lambda t: t.reshape(BATCH, SEQ, t.shape[-1])
        ch = lambda t: t.reshape(t.shape[0], BATCH, NQ, t.shape[2], TK)
        oa = _mla_call(qat, b3(ka), ch(vat))
        x16 = kvc.reshape(B_KV_GROUPS, BATCH, N_CMP_PAD, 16 * LANE)
        ob = _nsa_call(qbt, x16, b3(ks), ch(vst), b3(kw), ch(vwt), gt,
                       petop, pebot, w1top, w1bot, w2k, w2v, ovt)
        oc = _diff_call(qct, b3(kd), ch(vct), lqk, sn, lam_init)
        flat = lambda t: t.reshape(N_TOK, t.shape[-1])
        fg = final_norm.reshape(1, D_MODEL) if l == DEPTH - 1 else None
        xf = _ffn_call(xf, ffn2_norm[l].reshape(1, D_MODEL), wgu2, wd2,
                       mix=(flat(oa), flat(ob), flat(oc), w_o), final_gain=fg)
    return xf.reshape(BATCH, SEQ, D_MODEL)
```

```python
import functools
import math

import jax
import jax.numpy as jnp
from jax import lax
from jax.experimental import pallas as pl
from jax.experimental.pallas import tpu as pltpu

F32 = jnp.float32
BF16 = jnp.bfloat16

D_MODEL = 1024
BATCH = 4
SEQ = 4096
DEPTH = 2
N_TOK = BATCH * SEQ
EPS = 1e-6
ROPE_THETA = 500000.0
D_FF = 2816

A_HEADS = 6
A_Q_RANK = 192
A_KV_RANK = 128
A_NOPE = 64
A_ROPE = 32
A_V = 64

B_HEADS = 6
B_KV_GROUPS = 2
B_HPG = 3
B_DH = 64
B_ROT = 16
CMP_LEN = 32
CMP_STRIDE = 16
N_CMP = (SEQ - CMP_LEN) // CMP_STRIDE + 1
N_CMP_PAD = SEQ // CMP_STRIDE
SLC_LEN = 64
N_SLC = SEQ // SLC_LEN
SLC_TOPN = 16
WINDOW = 512

C_HEADS = 4
C_DH = 32
C_ROT = 8

A_COLS = A_Q_RANK + A_KV_RANK + A_ROPE
B_COLS = B_HEADS * B_DH + 6 * B_KV_GROUPS * B_DH + 3 * B_HEADS
C_COLS = 3 * C_HEADS * 2 * C_DH

LANE = 128
SUBLANES = 8
VROW = 64
MASKED = -1e30
SEL_BIG = 1e9
LOG2E = 1.0 / math.log(2.0)

TM = 512
TF = 256
NFC = D_FF // TF
TQ = 512
TK = 512
NQ = SEQ // TQ
NCH = N_TOK // TK
GATE_ROWS = 16

CE_A = 0
CE_BQ = 512
CE_KVC = 1280
CE_KS = 1536
CE_VS = 1792
CE_KW = 2048
CE_VW = 2304
CE_GATE = 2560
CE_CQ = 2816
CE_CKV = 3328
CE = 3840

VMEM_LIMIT = 56 * 1024 * 1024


def _const_spec(shape):
    nd = len(shape)
    return pl.BlockSpec(shape, lambda *_: (0,) * nd, pipeline_mode=pl.Buffered(1))


def _rms(x, g, n):
    ms = jnp.sum(x * x, axis=-1, keepdims=True) * (1.0 / n)
    return x * lax.rsqrt(ms + EPS) * g


def _dot(a, b):
    return jnp.dot(a, b, preferred_element_type=F32)


def _ffn_kernel(*refs, mix, final):
    it = iter(refs)
    x_ref = next(it)
    if mix:
        oa_ref, ob_ref, oc_ref, wo_ref = next(it), next(it), next(it), next(it)
    g_ref, wgu_ref, wd_ref = next(it), next(it), next(it)
    if final:
        fg_ref = next(it)
    o_ref = next(it)
    acc_ref = next(it)

    x = x_ref[...]
    if mix:
        x = x + _dot(oa_ref[...], wo_ref[0:768, :])
        x = x + _dot(ob_ref[...], wo_ref[768:1536, :])
        x = x + _dot(oc_ref[...], wo_ref[1536:2048, :])
    xn = _rms(x, g_ref[...], D_MODEL).astype(BF16)
    for c in range(NFC):
        gu = _dot(xn, wgu_ref[:, c * 2 * TF:(c + 1) * 2 * TF])
        g = gu[:, :TF]
        u = gu[:, TF:]
        h = (g * jax.nn.sigmoid(g) * u).astype(BF16)
        d = _dot(h, wd_ref[c * TF:(c + 1) * TF, :])
        if c == 0:
            acc_ref[...] = d
        else:
            acc_ref[...] += d
    y = x + 0.5 * acc_ref[...]
    if final:
        y = _rms(y, fg_ref[...], D_MODEL)
    o_ref[...] = y


def _ffn_call(x, gain, wgu, wd, mix=None, final_gain=None):
    tok = lambda w: pl.BlockSpec((TM, w), lambda i: (i, 0))
    args = [x]
    specs = [tok(D_MODEL)]
    if mix is not None:
        oa, ob, oc, wo = mix
        args += [oa, ob, oc, wo]
        specs += [tok(768), tok(768), tok(512), _const_spec((2048, D_MODEL))]
    args += [gain, wgu, wd]
    specs += [_const_spec((1, D_MODEL)), _const_spec((D_MODEL, 2 * D_FF)),
              _const_spec((D_FF, D_MODEL))]
    if final_gain is not None:
        args.append(final_gain)
        specs.append(_const_spec((1, D_MODEL)))
    kern = functools.partial(_ffn_kernel, mix=mix is not None, final=final_gain is not None)
    return pl.pallas_call(
        kern,
        grid=(N_TOK // TM,),
        in_specs=specs,
        out_specs=tok(D_MODEL),
        out_shape=jax.ShapeDtypeStruct((N_TOK, D_MODEL), F32),
        scratch_shapes=[pltpu.VMEM((TM, D_MODEL), F32)],
        compiler_params=pltpu.CompilerParams(
            dimension_semantics=("arbitrary",), vmem_limit_bytes=VMEM_LIMIT),
        name="ffn",
    )(*args)


ROPE_HALF = (16, 8, 4)


def _proj_kernel(x_ref, g_ref, w_ref, rope_ref, qn_ref, kvn_ref, wuq_ref, wukv_ref,
                 qat_ref, ka_ref, vat_ref, qbt_ref, kvc_ref, ks_ref, vst_ref, kw_ref, vwt_ref,
                 gt_ref, qct_ref, kd_ref, vct_ref):
    xn = _rms(x_ref[...], g_ref[...], D_MODEL).astype(BF16)
    lane = lax.broadcasted_iota(jnp.int32, (TM, LANE), 1)
    second = (
        (lane >= 80) & (lane < 96),
        (lane >= 8) & (lane < 16),
        ((lane >= 4) & (lane < 8)) | ((lane >= 36) & (lane < 40)),
    )
    ones_row = (lane == VROW).astype(F32)

    def rope(v, typ):
        cos = rope_ref[:, 2 * typ * LANE:(2 * typ + 1) * LANE]
        sin = rope_ref[:, (2 * typ + 1) * LANE:(2 * typ + 2) * LANE]
        r = ROPE_HALF[typ]
        partner = jnp.where(second[typ], pltpu.roll(v, r, 1), pltpu.roll(v, LANE - r, 1))
        return v * cos + partner * sin

    def blk(v, b):
        return v[:, b * LANE:(b + 1) * LANE]

    def tr(v):
        return v.T.astype(BF16)

    ha = _dot(xn, w_ref[:, CE_A:CE_A + 512])
    cqn = _rms(ha[:, 0:256], qn_ref[...], A_Q_RANK).astype(BF16)
    ckvn = _rms(ha[:, 256:384], kvn_ref[...], A_KV_RANK).astype(BF16)
    kpe = rope(ha[:, 384:512], 0)
    q = _dot(cqn, wuq_ref[...])
    kv = _dot(ckvn, wukv_ref[...])
    scale_a = (A_NOPE + A_ROPE) ** -0.5 * LOG2E
    for h in range(A_HEADS):
        qat_ref[h, 0] = tr(rope(blk(q, h), 0) * scale_a)
        ka_ref[:, h * LANE:(h + 1) * LANE] = (blk(kv, h) + kpe).astype(BF16)
        vat_ref[h, 0] = tr(blk(kv, A_HEADS + h) + ones_row)

    hq = _dot(xn, w_ref[:, CE_BQ:CE_BQ + 768])
    scale_b = B_DH ** -0.5 * LOG2E
    for b in range(B_HEADS):
        qbt_ref[b, 0] = tr(rope(blk(hq, b), 1) * scale_b)
    hk = _dot(xn, w_ref[:, CE_KVC:CE_KVC + 256])
    for g in range(B_KV_GROUPS):
        kvc_ref[g] = rope(blk(hk, g), 1)
    tpos = (pl.program_id(0) % (SEQ // TM)) * TM + lax.broadcasted_iota(jnp.int32, (TM, LANE), 0)
    onehot = ((lane - 64) == (tpos // SLC_LEN)).astype(F32)
    hk = _dot(xn, w_ref[:, CE_KS:CE_KS + 256])
    for g in range(B_KV_GROUPS):
        ks_ref[:, g * LANE:(g + 1) * LANE] = (rope(blk(hk, g), 1) + onehot).astype(BF16)
    hk = _dot(xn, w_ref[:, CE_VS:CE_VS + 256])
    for g in range(B_KV_GROUPS):
        vst_ref[g, 0] = tr(blk(hk, g) + ones_row)
    hk = _dot(xn, w_ref[:, CE_KW:CE_KW + 256])
    for g in range(B_KV_GROUPS):
        kw_ref[:, g * LANE:(g + 1) * LANE] = rope(blk(hk, g), 1).astype(BF16)
    hk = _dot(xn, w_ref[:, CE_VW:CE_VW + 256])
    for g in range(B_KV_GROUPS):
        vwt_ref[g, 0] = tr(blk(hk, g) + ones_row)
    hk = jax.nn.sigmoid(_dot(xn, w_ref[:, CE_GATE:CE_GATE + 256]))
    for g in range(B_KV_GROUPS):
        gt_ref[g, 0] = blk(hk, g).T[0:GATE_ROWS, :]

    hc = _dot(xn, w_ref[:, CE_CQ:CE_CQ + 512])
    scale_c = C_DH ** -0.5 * LOG2E
    for h in range(C_HEADS):
        qct_ref[h, 0] = tr(rope(blk(hc, h), 2) * scale_c)
    hc = _dot(xn, w_ref[:, CE_CKV:CE_CKV + 512])
    for h in range(C_HEADS):
        kv_blk = rope(blk(hc, h), 2)
        kd_ref[:, h * LANE:(h + 1) * LANE] = kv_blk.astype(BF16)
        v_lo = jnp.where(lane < 64, pltpu.roll(kv_blk, 64, 1), ones_row)
        vct_ref[h, 0] = tr(v_lo)


def _proj_call(x, gain, w_e, rope_tab, qn, kvn, wuq, wukv):
    tok = lambda w: pl.BlockSpec((TM, w), lambda i: (i, 0))
    bf = lambda w: jax.ShapeDtypeStruct((N_TOK, w), BF16)
    tspec = lambda nb, rows: pl.BlockSpec((nb, 1, rows, TM), lambda i: (0, i, 0, 0))
    tshape = lambda nb, rows, dt: jax.ShapeDtypeStruct((nb, NCH, rows, TM), dt)
    return pl.pallas_call(
        _proj_kernel,
        grid=(N_TOK // TM,),
        in_specs=[tok(D_MODEL), _const_spec((1, D_MODEL)), _const_spec((D_MODEL, CE)),
                  tok(6 * LANE), _const_spec((1, 256)), _const_spec((1, LANE)),
                  _const_spec((256, 768)), _const_spec((LANE, 1536))],
        out_specs=[tspec(6, LANE), tok(768), tspec(6, LANE), tspec(6, LANE),
                   pl.BlockSpec((B_KV_GROUPS, TM, LANE), lambda i: (0, i, 0)),
                   tok(256), tspec(2, LANE), tok(256), tspec(2, LANE),
                   tspec(2, GATE_ROWS), tspec(4, LANE), tok(512), tspec(4, LANE)],
        out_shape=[tshape(6, LANE, BF16), bf(768), tshape(6, LANE, BF16), tshape(6, LANE, BF16),
                   jax.ShapeDtypeStruct((B_KV_GROUPS, N_TOK, LANE), F32),
                   bf(256), tshape(2, LANE, BF16), bf(256), tshape(2, LANE, BF16),
                   tshape(2, GATE_ROWS, F32), tshape(4, LANE, BF16), bf(512),
                   tshape(4, LANE, BF16)],
        compiler_params=pltpu.CompilerParams(
            dimension_semantics=("arbitrary",), vmem_limit_bytes=VMEM_LIMIT),
        name="proj",
    )(x, gain, w_e, rope_tab, qn, kvn, wuq, wukv)


def _tile_pos(r):
    kk = lax.broadcasted_iota(jnp.int32, (TK, 1), 0)
    tt = lax.broadcasted_iota(jnp.int32, (1, r), 1) & (TQ - 1)
    return kk, tt


def _score_stage(qt, k, mask, m_prev, s_ref, m_ref):
    s = _dot(k, qt)
    if mask is not None:
        s = jnp.where(mask, s, MASKED)
    cm = jnp.max(s, axis=0, keepdims=True)
    m_ref[...] = cm if m_prev is None else jnp.maximum(m_prev[...], cm)
    s_ref[...] = s


def _value_stage(vt, s_ref, m_ref, mb_ref, acc_ref):
    m = m_ref[...]
    alpha = jnp.exp2(mb_ref[...] - m)
    p = jnp.exp2(s_ref[...] - m).astype(BF16)
    mb_ref[...] = m
    acc_ref[...] = alpha * acc_ref[...] + _dot(vt, p)


def _causal_attention(streams, i):
    for qt, k_of, vt_of, (s0, s1, m0, m1, mb, acc) in streams:
        kk, tt = _tile_pos(qt.shape[1])
        _score_stage(qt, k_of(i), kk <= tt, None, s0, m0)
        mb[...] = m0[...]
        acc[...] = jnp.zeros_like(acc)

    def step(score_j, pend_j, a, b):
        for qt, k_of, vt_of, sc in streams:
            bufs, maxs, mb, acc = (sc[0], sc[1]), (sc[2], sc[3]), sc[4], sc[5]
            _score_stage(qt, k_of(score_j), None, maxs[a], bufs[b], maxs[b])
            _value_stage(vt_of(pend_j), bufs[a], maxs[a], mb, acc)

    def fold(pend_j, a):
        for qt, k_of, vt_of, sc in streams:
            _value_stage(vt_of(pend_j), sc[a], sc[2 + a], sc[4], sc[5])

    def body(jj, carry):
        step(2 * jj, jnp.where(jj == 0, i, 2 * jj - 1), 0, 1)
        step(2 * jj + 1, 2 * jj, 1, 0)
        return carry

    npairs = i // 2
    lax.fori_loop(0, npairs, body, 0)
    pend = jnp.where(npairs == 0, i, 2 * npairs - 1)

    @pl.when(i % 2 == 1)
    def _():
        step(i - 1, pend, 0, 1)
        fold(i - 1, 1)

    @pl.when(i % 2 == 0)
    def _():
        fold(pend, 0)

    return [st[3][5][...] for st in streams]


def _normalise(acc):
    row = lax.broadcasted_iota(jnp.int32, acc.shape, 0)
    return jnp.where(row < VROW, acc / acc[VROW:VROW + 1, :], 0.0)


def _pipe_scratch(r):
    return ([pltpu.VMEM((TK, r), F32)] * 2 + [pltpu.VMEM((1, r), F32)] * 3
            + [pltpu.VMEM((LANE, r), F32)])


N_PIPE = 6


def _k_chunk(k_ref, j, sl=slice(None)):
    return k_ref[pl.ds(pl.multiple_of(j * TK, TK), TK), sl]


def _qt_spec(nb):
    return pl.BlockSpec((nb, None, LANE, TQ), lambda b, h, i: (h, b * NQ + i, 0, 0))


_K_SPEC = pl.BlockSpec((None, SEQ, LANE), lambda b, h, i: (b, 0, h))
_VT_SPEC = pl.BlockSpec((None, None, NQ, LANE, TK), lambda b, h, i: (h, b, 0, 0, 0))


def _attn_params():
    return pltpu.CompilerParams(dimension_semantics=("arbitrary",) * 3,
                                vmem_limit_bytes=VMEM_LIMIT)


A_HPS = 2


def _mla_kernel(qt_ref, k_ref, vt_ref, o_ref, *scratch):
    i = pl.program_id(2)
    streams = []
    for h in range(A_HPS):
        sl = slice(h * LANE, (h + 1) * LANE)
        streams.append((qt_ref[h],
                        lambda j, sl=sl: _k_chunk(k_ref, j, sl),
                        lambda j, h=h: vt_ref[h, j],
                        scratch[N_PIPE * h:N_PIPE * (h + 1)]))
    accs = _causal_attention(streams, i)
    for h in range(A_HPS):
        o_ref[:, h * LANE:(h + 1) * LANE] = _normalise(accs[h]).T.astype(BF16)


def _mla_call(qat, ka, vat):
    w = A_HPS * LANE
    return pl.pallas_call(
        _mla_kernel,
        grid=(BATCH, A_HEADS // A_HPS, NQ),
        in_specs=[_qt_spec(A_HPS),
                  pl.BlockSpec((None, SEQ, w), lambda b, h, i: (b, 0, h)),
                  pl.BlockSpec((A_HPS, None, NQ, LANE, TK), lambda b, h, i: (h, b, 0, 0, 0))],
        out_specs=pl.BlockSpec((None, TQ, w), lambda b, h, i: (b, i, h)),
        out_shape=jax.ShapeDtypeStruct((BATCH, SEQ, A_HEADS * LANE), BF16),
        scratch_shapes=_pipe_scratch(TQ) * A_HPS,
        compiler_params=_attn_params(),
        name="mla_attn",
    )(qat, ka, vat)


def _diff_kernel(qt_ref, k_ref, vt_ref, lqk_ref, sn_ref, o_ref, *scratch, lam_init):
    i = pl.program_id(2)
    qt = qt_ref[0]
    row = lax.broadcasted_iota(jnp.int32, (LANE, TQ), 0)
    zero = jnp.zeros_like(qt)
    q2 = jnp.concatenate([jnp.where(row < C_DH, qt, zero),
                          jnp.where((row >= C_DH) & (row < 2 * C_DH), qt, zero)], axis=1)
    stream = (q2, lambda j: _k_chunk(k_ref, j), lambda j: vt_ref[j], scratch)
    o = _normalise(_causal_attention([stream], i)[0])
    lqk = lqk_ref[...]
    lam = (jnp.exp(jnp.sum(lqk[0:1] * lqk[1:2], axis=-1, keepdims=True))
           - jnp.exp(jnp.sum(lqk[2:3] * lqk[3:4], axis=-1, keepdims=True)) + lam_init)
    d = (o[:, 0:TQ] - lam * o[:, TQ:2 * TQ]).T
    ms = jnp.sum(d * d, axis=-1, keepdims=True) * (1.0 / (2 * C_DH))
    y = d * lax.rsqrt(ms + EPS) * sn_ref[...] * (1.0 - lam_init)
    o_ref[...] = y.astype(BF16)


def _diff_call(qct, kd, vct, lqk, sn, lam_init):
    return pl.pallas_call(
        functools.partial(_diff_kernel, lam_init=lam_init),
        grid=(BATCH, C_HEADS, NQ),
        in_specs=[_qt_spec(1), _K_SPEC, _VT_SPEC, _const_spec((8, LANE)), _const_spec((1, LANE))],
        out_specs=pl.BlockSpec((None, TQ, LANE), lambda b, h, i: (b, i, h)),
        out_shape=jax.ShapeDtypeStruct((BATCH, SEQ, C_HEADS * LANE), BF16),
        scratch_shapes=_pipe_scratch(2 * TQ),
        compiler_params=_attn_params(),
        name="diff_attn",
    )(qct, kd, vct, lqk, sn)


def _nsa_kernel(qt_ref, x16_ref, ks_ref, vst_ref, kw_ref, vwt_ref, gt_ref,
                petop_ref, pebot_ref, w1top_ref, w1bot_ref, w2k_ref, w2v_ref, ovt_ref,
                o_ref, kc_s, vct_s, *slc_scratch):
    i = pl.program_id(2)
    r3 = B_HPG * TQ

    @pl.when(i == 0)
    def _():
        x = x16_ref[...]
        yt = _dot((x + petop_ref[...]).astype(BF16), w1top_ref[...])
        yb = _dot((x + pebot_ref[...]).astype(BF16), w1bot_ref[...])
        pre = yt + pltpu.roll(yb, N_CMP_PAD - 1, 0)
        act = (pre * jax.nn.sigmoid(pre)).astype(BF16)
        kc_s[...] = _dot(act, w2k_ref[...]).astype(BF16)
        vct_s[...] = _dot(act, w2v_ref[...]).T.astype(BF16)

    q3 = jnp.concatenate([qt_ref[h] for h in range(B_HPG)], axis=1)

    kk, tt = _tile_pos(r3)
    c0 = jnp.maximum(i - 1, 0)
    s_diag = jnp.where(kk <= tt, _dot(_k_chunk(kw_ref, i), q3), MASKED)
    s_prev = jnp.where(tt + jnp.where(i >= 1, 0, TQ) < kk, _dot(_k_chunk(kw_ref, c0), q3), MASKED)
    m_win = jnp.maximum(jnp.max(s_diag, axis=0, keepdims=True),
                        jnp.max(s_prev, axis=0, keepdims=True))
    o_win = _normalise(_dot(vwt_ref[i], jnp.exp2(s_diag - m_win).astype(BF16))
                       + _dot(vwt_ref[c0], jnp.exp2(s_prev - m_win).astype(BF16)))

    s = _dot(kc_s[...], q3)
    c_end = lax.broadcasted_iota(jnp.int32, (N_CMP_PAD, 1), 0) * CMP_STRIDE + (CMP_LEN - 1)
    valid = c_end <= i * TQ + tt
    smax = jnp.max(jnp.where(valid, s, MASKED), axis=0, keepdims=True)
    e = jnp.where(valid, jnp.exp2(s - smax), 0.0)
    den = jnp.sum(e, axis=0, keepdims=True)
    p = e / jnp.where(den > 0.0, den, 1.0)
    o_cmp = _dot(vct_s[...], p.astype(BF16))

    psum = p[:, 0:TQ] + p[:, TQ:2 * TQ] + p[:, 2 * TQ:3 * TQ]
    p_hi = psum.astype(BF16)
    p_lo = (psum - p_hi.astype(F32)).astype(BF16)
    ovt = ovt_ref[...]
    imp = (_dot(ovt, p_hi) + _dot(ovt, p_lo))[64:128]
    m_idx = lax.broadcasted_iota(jnp.int32, (N_SLC, TQ), 0)
    cur = (i * TQ + lax.broadcasted_iota(jnp.int32, (N_SLC, TQ), 1)) // SLC_LEN
    forced = (m_idx == 0) | (m_idx == cur) | (m_idx == cur - 1)
    score = jnp.where(forced, 1e9, jnp.where(m_idx <= cur, imp, -1.0))
    sub = lax.broadcasted_iota(jnp.int32, (SUBLANES, TQ), 0)
    bias = [jnp.zeros((64, TQ), F32)]
    for v in range(N_SLC // SUBLANES):
        tile = score[v * SUBLANES:(v + 1) * SUBLANES, :]
        rank = jnp.zeros((SUBLANES, TQ), jnp.int32)
        for mp in range(N_SLC):
            srow = score[mp:mp + 1, :]
            if mp < v * SUBLANES:
                ahead = srow >= tile
            elif mp >= (v + 1) * SUBLANES:
                ahead = srow > tile
            else:
                ahead = (srow > tile) | ((srow == tile) & (sub > mp - v * SUBLANES))
            rank = rank + jnp.where(ahead, 1, 0)
        bias.append(jnp.where(rank < SLC_TOPN, 0.0, -SEL_BIG))
    bias = jnp.concatenate(bias, axis=0).astype(BF16)
    q_sel = q3 + jnp.concatenate([bias] * B_HPG, axis=1)

    stream = (q_sel, lambda j: _k_chunk(ks_ref, j), lambda j: vst_ref[j], slc_scratch)
    o_slc = _normalise(_causal_attention([stream], i)[0])

    gate = gt_ref[...]
    for h in range(B_HPG):
        sl = slice(h * TQ, (h + 1) * TQ)
        o = (gate[3 * h:3 * h + 1] * o_cmp[:, sl]
             + gate[3 * h + 1:3 * h + 2] * o_slc[:, sl]
             + gate[3 * h + 2:3 * h + 3] * o_win[:, sl])
        o_ref[:, h * LANE:(h + 1) * LANE] = o.T.astype(BF16)


def _nsa_call(qbt, x16, ks, vst, kw, vwt, gt, petop, pebot, w1top, w1bot, w2k, w2v, ovt):
    r3 = B_HPG * TQ
    return pl.pallas_call(
        _nsa_kernel,
        grid=(BATCH, B_KV_GROUPS, NQ),
        in_specs=[_qt_spec(B_HPG),
                  pl.BlockSpec((None, None, N_CMP_PAD, 16 * LANE), lambda b, g, i: (g, b, 0, 0)),
                  _K_SPEC, _VT_SPEC, _K_SPEC, _VT_SPEC,
                  pl.BlockSpec((None, None, GATE_ROWS, TQ), lambda b, g, i: (g, b * NQ + i, 0, 0)),
                  _const_spec((1, 16 * LANE)), _const_spec((1, 16 * LANE)),
                  _const_spec((16 * LANE, LANE)), _const_spec((16 * LANE, LANE)),
                  _const_spec((LANE, LANE)), _const_spec((LANE, LANE)),
                  _const_spec((LANE, N_CMP_PAD))],
        out_specs=pl.BlockSpec((None, TQ, B_HPG * LANE), lambda b, g, i: (b, i, g)),
        out_shape=jax.ShapeDtypeStruct((BATCH, SEQ, B_HEADS * LANE), BF16),
        scratch_shapes=[pltpu.VMEM((N_CMP_PAD, LANE), BF16), pltpu.VMEM((LANE, N_CMP_PAD), BF16)]
        + _pipe_scratch(r3),
        compiler_params=_attn_params(),
        name="nsa_attn",
    )(qbt, x16, ks, vst, kw, vwt, gt, petop, pebot, w1top, w1bot, w2k, w2v, ovt)


def _expand_w_in(w):
    a = w[:, :A_COLS]
    b = w[:, A_COLS:A_COLS + B_COLS]
    c = w[:, A_COLS + B_COLS:]
    z = lambda n: jnp.zeros((D_MODEL, n), w.dtype)
    parts = [a[:, 0:192], z(64), a[:, 192:320], z(64), a[:, 320:352], z(32)]
    for g in range(B_KV_GROUPS):
        for hg in range(B_HPG):
            o = g * 192 + hg * 64
            parts += [b[:, o:o + 64], z(64)]
    for g in range(B_KV_GROUPS):
        parts += [b[:, 384 + g * 64:448 + g * 64], b[:, 512 + g * 64:576 + g * 64]]
    for base in (640, 768, 896, 1024):
        for g in range(B_KV_GROUPS):
            parts += [b[:, base + g * 64:base + g * 64 + 64], z(64)]
    for g in range(B_KV_GROUPS):
        parts += [b[:, 1152 + g * 9:1161 + g * 9], z(119)]
    for h in range(C_HEADS):
        parts += [c[:, h * 64:(h + 1) * 64], z(64)]
    for h in range(C_HEADS):
        parts += [c[:, 256 + h * 64:320 + h * 64], c[:, 512 + h * 64:576 + h * 64]]
    return jnp.concatenate(parts, axis=1).astype(BF16)


def _expand_w_out(w):
    z = jnp.zeros((64, D_MODEL), w.dtype)
    parts = []
    for h in range(A_HEADS + B_HEADS + C_HEADS):
        parts += [w[h * 64:(h + 1) * 64], z]
    return jnp.concatenate(parts, axis=0).astype(BF16)


def _rope_tables(positions):
    pos = positions.astype(F32).reshape(N_TOK, 1)

    def cs(rot):
        inv = 1.0 / (ROPE_THETA ** (jnp.arange(0, rot, 2, dtype=F32) / rot))
        ang = pos * inv
        return jnp.cos(ang), jnp.sin(ang)

    one = lambda n: jnp.ones((N_TOK, n), F32)
    zero = lambda n: jnp.zeros((N_TOK, n), F32)
    ca, sa = cs(A_ROPE)
    cb, sb = cs(B_ROT)
    cc, sc = cs(C_ROT)
    tabs = [
        jnp.concatenate([one(64), ca, ca, one(32)], axis=1),
        jnp.concatenate([zero(64), -sa, sa, zero(32)], axis=1),
        jnp.concatenate([cb, cb, one(112)], axis=1),
        jnp.concatenate([-sb, sb, zero(112)], axis=1),
        jnp.concatenate([cc, cc, one(24), cc, cc, one(24), one(64)], axis=1),
        jnp.concatenate([-sc, sc, zero(24), -sc, sc, zero(24), zero(64)], axis=1),
    ]
    return jnp.concatenate(tabs, axis=1)


def _overlap_t():
    c = jnp.arange(N_CMP_PAD)[None, :]
    m = jnp.arange(N_SLC)[:, None]
    ov = ((c * CMP_STRIDE < m * SLC_LEN + SLC_LEN)
          & (c * CMP_STRIDE + CMP_LEN - 1 >= m * SLC_LEN) & (c < N_CMP))
    return jnp.concatenate([jnp.zeros((64, N_CMP_PAD), BF16), ov.astype(BF16)], axis=0)


def _pad_lanes(v, width):
    return jnp.pad(v.reshape(1, -1), ((0, 0), (0, width - v.shape[-1])))


def kernel(x, positions, ffn1_norm, ffn1_wg, ffn1_wu, ffn1_wd, mix_norm, w_in, mla_q_norm, mla_kv_norm, mla_w_uq, mla_w_ukv, nsa_pe_k, nsa_pe_v, nsa_phi_k1, nsa_phi_k2, nsa_phi_v1, nsa_phi_v2, diff_lq1, diff_lk1, diff_lq2, diff_lk2, diff_sub_norm, w_out, ffn2_norm, ffn2_wg, ffn2_wu, ffn2_wd, final_norm):
    xf = x.reshape(N_TOK, D_MODEL)
    rope_tab = _rope_tables(positions)
    ovt = _overlap_t()

    def ffn_weights(wg, wu, wd):
        wgu = jnp.stack([wg.reshape(D_MODEL, NFC, TF), wu.reshape(D_MODEL, NFC, TF)], axis=2)
        return wgu.reshape(D_MODEL, 2 * D_FF).astype(BF16), wd.astype(BF16)

    for l in range(DEPTH):
        wgu1, wd1 = ffn_weights(ffn1_wg[l], ffn1_wu[l], ffn1_wd[l])
        wgu2, wd2 = ffn_weights(ffn2_wg[l], ffn2_wu[l], ffn2_wd[l])
        w_e = _expand_w_in(w_in[l])
        w_o = _expand_w_out(w_out[l])
        wuq = jnp.pad(mla_w_uq[l].reshape(A_Q_RANK, A_HEADS, A_NOPE + A_ROPE),
                      ((0, 256 - A_Q_RANK), (0, 0), (0, LANE - A_NOPE - A_ROPE)))
        wuq = wuq.reshape(256, A_HEADS * LANE).astype(BF16)
        wkv = mla_w_ukv[l].reshape(A_KV_RANK, A_HEADS, A_NOPE + A_V)
        padv = ((0, 0), (0, 0), (0, LANE - 64))
        wukv = jnp.concatenate(
            [jnp.pad(wkv[:, :, :A_NOPE], padv).reshape(A_KV_RANK, A_HEADS * LANE),
             jnp.pad(wkv[:, :, A_NOPE:], padv).reshape(A_KV_RANK, A_HEADS * LANE)],
            axis=1).astype(BF16)
        qn = _pad_lanes(mla_q_norm[l], 256)
        kvn = _pad_lanes(mla_kv_norm[l], LANE)

        k1 = nsa_phi_k1[l].reshape(CMP_LEN, B_DH, B_DH)
        v1 = nsa_phi_v1[l].reshape(CMP_LEN, B_DH, B_DH)
        z3 = jnp.zeros_like(k1)
        w1 = jnp.concatenate([jnp.concatenate([k1, z3], axis=2),
                              jnp.concatenate([z3, v1], axis=2)], axis=1)
        w1top = w1[:16].reshape(16 * LANE, LANE).astype(BF16)
        w1bot = w1[16:].reshape(16 * LANE, LANE).astype(BF16)
        pe = jnp.concatenate([nsa_pe_k[l], nsa_pe_v[l]], axis=1)
        petop = pe[:16].reshape(1, 16 * LANE)
        pebot = pe[16:].reshape(1, 16 * LANE)
        w2k = jnp.zeros((LANE, LANE), F32).at[0:64, 0:64].set(nsa_phi_k2[l]).astype(BF16)
        w2v = jnp.zeros((LANE, LANE), F32).at[64:128, 0:64].set(nsa_phi_v2[l]).astype(BF16)

        lqk = jnp.pad(jnp.stack([diff_lq1[l], diff_lk1[l], diff_lq2[l], diff_lk2[l]], axis=0),
                      ((0, 4), (0, LANE - C_DH)))
        sn = _pad_lanes(diff_sub_norm[l], LANE)
        lam_init = 0.8 - 0.6 * math.exp(-0.3 * l)

        xf = _ffn_call(xf, ffn1_norm[l].reshape(1, D_MODEL), wgu1, wd1)
        (qat, ka, vat, qbt, kvc, ks, vst, kw, vwt, gt, qct, kd, vct) = _proj_call(
            xf, mix_norm[l].reshape(1, D_MODEL), w_e, rope_tab, qn, kvn, wuq, wukv)
        b3 = lambda t: t.reshape(BATCH, SEQ, t.shape[-1])
        ch = lambda t: t.reshape(t.shape[0], BATCH, NQ, t.shape[2], TK)
        oa = _mla_call(qat, b3(ka), ch(vat))
        x16 = kvc.reshape(B_KV_GROUPS, BATCH, N_CMP_PAD, 16 * LANE)
        ob = _nsa_call(qbt, x16, b3(ks), ch(vst), b3(kw), ch(vwt), gt,
                       petop, pebot, w1top, w1bot, w2k, w2v, ovt)
        oc = _diff_call(qct, b3(kd), ch(vct), lqk, sn, lam_init)
        flat = lambda t: t.reshape(N_TOK, t.shape[-1])
        fg = final_norm.reshape(1, D_MODEL) if l == DEPTH - 1 else None
        xf = _ffn_call(xf, ffn2_norm[l].reshape(1, D_MODEL), wgu2, wd2,
                       mix=(flat(oa), flat(ob), flat(oc), w_o), final_gain=fg)
    return xf.reshape(BATCH, SEQ, D_MODEL)
```

```python
import functools
import math

import jax
import jax.numpy as jnp
from jax import lax
from jax.experimental import pallas as pl
from jax.experimental.pallas import tpu as pltpu

F32 = jnp.float32
BF16 = jnp.bfloat16

D_MODEL = 1024
BATCH = 4
SEQ = 4096
DEPTH = 2
N_TOK = BATCH * SEQ
EPS = 1e-6
ROPE_THETA = 500000.0
D_FF = 2816

A_HEADS = 6
A_Q_RANK = 192
A_KV_RANK = 128
A_NOPE = 64
A_ROPE = 32
A_V = 64

B_HEADS = 6
B_KV_GROUPS = 2
B_HPG = 3
B_DH = 64
B_ROT = 16
CMP_LEN = 32
CMP_STRIDE = 16
N_CMP = (SEQ - CMP_LEN) // CMP_STRIDE + 1
N_CMP_PAD = SEQ // CMP_STRIDE
SLC_LEN = 64
N_SLC = SEQ // SLC_LEN
SLC_TOPN = 16
WINDOW = 512

C_HEADS = 4
C_DH = 32
C_ROT = 8

A_COLS = A_Q_RANK + A_KV_RANK + A_ROPE
B_COLS = B_HEADS * B_DH + 6 * B_KV_GROUPS * B_DH + 3 * B_HEADS
C_COLS = 3 * C_HEADS * 2 * C_DH

LANE = 128
SUBLANES = 8
VROW = 64
MASKED = -1e30
SEL_BIG = 1e9
LOG2E = 1.0 / math.log(2.0)

TM = 512
TF = 256
NFC = D_FF // TF
TQ = 512
TK = 512
NQ = SEQ // TQ
NCH = N_TOK // TK
GATE_ROWS = 16

CE_A = 0
CE_BQ = 384
CE_KVC = 1152
CE_KSVS = 1408
CE_KWVW = 1664
CE_CQ = 1920
CE_CKV = 2432
CE = 2944

VMEM_LIMIT = 56 * 1024 * 1024


def _const_spec(shape):
    nd = len(shape)
    return pl.BlockSpec(shape, lambda *_: (0,) * nd, pipeline_mode=pl.Buffered(1))


def _layer_spec(shape, l):
    nd = len(shape)
    return pl.BlockSpec((None,) + tuple(shape), lambda *_: (l,) + (0,) * nd,
                        pipeline_mode=pl.Buffered(1))


def _rms(x, g, n):
    ms = jnp.sum(x * x, axis=-1, keepdims=True) * (1.0 / n)
    return x * lax.rsqrt(ms + EPS) * g


def _dot(a, b):
    return jnp.dot(a, b, preferred_element_type=F32)


def _ffn_kernel(*refs, mix, final):
    it = iter(refs)
    x_ref = next(it)
    if mix:
        oa_ref, ob_ref, oc_ref, wo_ref = next(it), next(it), next(it), next(it)
    g_ref, wg_ref, wu_ref, wd_ref = next(it), next(it), next(it), next(it)
    if final:
        fg_ref = next(it)
    o_ref = next(it)
    acc_ref = next(it)

    x = x_ref[...]
    if mix:
        x = x + _dot(oa_ref[...], wo_ref[0:768, :])
        x = x + _dot(ob_ref[...], wo_ref[768:1536, :])
        x = x + _dot(oc_ref[...], wo_ref[1536:2048, :])
    xn = _rms(x, g_ref[...], D_MODEL).astype(BF16)
    for c in range(NFC):
        g = _dot(xn, wg_ref[:, c * TF:(c + 1) * TF])
        u = _dot(xn, wu_ref[:, c * TF:(c + 1) * TF])
        h = (g * jax.nn.sigmoid(g) * u).astype(BF16)
        d = _dot(h, wd_ref[c * TF:(c + 1) * TF, :])
        if c == 0:
            acc_ref[...] = d
        else:
            acc_ref[...] += d
    y = x + 0.5 * acc_ref[...]
    if final:
        y = _rms(y, fg_ref[...], D_MODEL)
    o_ref[...] = y


def _ffn_call(l, x, gain, wg, wu, wd, mix=None, final_gain=None):
    tok = lambda w: pl.BlockSpec((TM, w), lambda i: (i, 0))
    args = [x]
    specs = [tok(D_MODEL)]
    if mix is not None:
        oa, ob, oc, wo = mix
        args += [oa, ob, oc, wo]
        specs += [tok(768), tok(768), tok(512), _layer_spec((2048, D_MODEL), l)]
    args += [gain, wg, wu, wd]
    specs += [_layer_spec((1, D_MODEL), l), _layer_spec((D_MODEL, D_FF), l),
              _layer_spec((D_MODEL, D_FF), l), _layer_spec((D_FF, D_MODEL), l)]
    if final_gain is not None:
        args.append(final_gain)
        specs.append(_const_spec((1, D_MODEL)))
    kern = functools.partial(_ffn_kernel, mix=mix is not None, final=final_gain is not None)
    return pl.pallas_call(
        kern,
        grid=(N_TOK // TM,),
        in_specs=specs,
        out_specs=tok(D_MODEL),
        out_shape=jax.ShapeDtypeStruct((N_TOK, D_MODEL), F32),
        scratch_shapes=[pltpu.VMEM((TM, D_MODEL), F32)],
        compiler_params=pltpu.CompilerParams(
            dimension_semantics=("arbitrary",), vmem_limit_bytes=VMEM_LIMIT),
        name="ffn",
    )(*args)


ROPE_HALF = (16, 8, 4)


def _proj_kernel(x_ref, g_ref, w_ref, rope_ref, qn_ref, kvn_ref, wuq_ref, wukv_ref,
                 qa_ref, ka_ref, vat_ref, qb_ref, kvc_ref, ks_ref, vst_ref, kw_ref, vwt_ref,
                 gt_ref, qc_ref, kd_ref, vct_ref):
    xn = _rms(x_ref[...], g_ref[...], D_MODEL).astype(BF16)
    lane = lax.broadcasted_iota(jnp.int32, (TM, LANE), 1)
    lo = lane < 64
    between = lambda a, b: (lane >= a) & (lane < b)
    rot = (between(64, 96), lane < 16, between(16, 24) | between(48, 56))
    second = (between(80, 96), between(8, 16), between(20, 24) | between(52, 56))
    cos_all = rope_ref[:, 0:LANE]
    sin_all = rope_ref[:, LANE:2 * LANE]
    cos = [jnp.where(m, cos_all, 1.0) for m in rot]
    sin = [jnp.where(m, sin_all, 0.0) for m in rot]
    ones_row = (lane == VROW).astype(F32)

    def rope(v, typ):
        r = ROPE_HALF[typ]
        partner = jnp.where(second[typ], pltpu.roll(v, r, 1), pltpu.roll(v, LANE - r, 1))
        return v * cos[typ] + partner * sin[typ]

    def blk(v, b):
        return v[:, b * LANE:(b + 1) * LANE]

    def tr(v):
        return v.T.astype(BF16)

    def upper_as_vt(v):
        return tr(jnp.where(lo, pltpu.roll(v, 64, 1), ones_row))

    ha = _dot(xn, w_ref[:, CE_A:CE_A + 384])
    a1 = blk(ha, 1)
    cq = jnp.concatenate([blk(ha, 0), jnp.where(lo, a1, 0.0)], axis=1)
    cqn = _rms(cq, qn_ref[...], A_Q_RANK).astype(BF16)
    ckvn = _rms(blk(ha, 2), kvn_ref[...], A_KV_RANK).astype(BF16)
    kpe = jnp.where(rot[0], rope(a1, 0), 0.0)
    gt_ref[0] = jax.nn.sigmoid(a1).T[96:128, :]
    q = _dot(cqn, wuq_ref[...])
    kv = _dot(ckvn, wukv_ref[...])
    scale_a = (A_NOPE + A_ROPE) ** -0.5 * LOG2E
    for h in range(A_HEADS):
        qa_ref[:, h * LANE:(h + 1) * LANE] = (rope(blk(q, h), 0) * scale_a).astype(BF16)
        ka_ref[:, h * LANE:(h + 1) * LANE] = (blk(kv, h) + kpe).astype(BF16)
        vat_ref[h, 0] = tr(blk(kv, A_HEADS + h) + ones_row)

    hq = _dot(xn, w_ref[:, CE_BQ:CE_BQ + 768])
    scale_b = B_DH ** -0.5 * LOG2E
    for b in range(B_HEADS):
        qb_ref[:, b * LANE:(b + 1) * LANE] = (rope(blk(hq, b), 1) * scale_b).astype(BF16)
    hk = _dot(xn, w_ref[:, CE_KVC:CE_KVC + 256])
    for g in range(B_KV_GROUPS):
        kvc_ref[g] = rope(blk(hk, g), 1)
    tpos = (pl.program_id(0) % (SEQ // TM)) * TM + lax.broadcasted_iota(jnp.int32, (TM, LANE), 0)
    onehot = ((lane - 64) == (tpos // SLC_LEN)).astype(F32)
    hk = _dot(xn, w_ref[:, CE_KSVS:CE_KSVS + 256])
    for g in range(B_KV_GROUPS):
        ks_ref[:, g * LANE:(g + 1) * LANE] = jnp.where(lo, rope(blk(hk, g), 1), onehot).astype(BF16)
        vst_ref[g, 0] = upper_as_vt(blk(hk, g))
    hk = _dot(xn, w_ref[:, CE_KWVW:CE_KWVW + 256])
    for g in range(B_KV_GROUPS):
        kw_ref[:, g * LANE:(g + 1) * LANE] = rope(blk(hk, g), 1).astype(BF16)
        vwt_ref[g, 0] = upper_as_vt(blk(hk, g))

    hc = _dot(xn, w_ref[:, CE_CQ:CE_CQ + 512])
    scale_c = C_DH ** -0.5 * LOG2E
    for h in range(C_HEADS):
        qc_ref[:, h * LANE:(h + 1) * LANE] = (rope(blk(hc, h), 2) * scale_c).astype(BF16)
    hc = _dot(xn, w_ref[:, CE_CKV:CE_CKV + 512])
    for h in range(C_HEADS):
        kv_blk = rope(blk(hc, h), 2)
        kd_ref[:, h * LANE:(h + 1) * LANE] = kv_blk.astype(BF16)
        vct_ref[h, 0] = upper_as_vt(kv_blk)


def _proj_call(l, x, gain, w_e, rope_tab, qn, kvn, wuq, wukv):
    tok = lambda w: pl.BlockSpec((TM, w), lambda i: (i, 0))
    bf = lambda w: jax.ShapeDtypeStruct((N_TOK, w), BF16)
    tspec = lambda nb, rows: pl.BlockSpec((nb, 1, rows, TM), lambda i: (0, i, 0, 0))
    tshape = lambda nb, rows, dt: jax.ShapeDtypeStruct((nb, NCH, rows, TM), dt)
    return pl.pallas_call(
        _proj_kernel,
        grid=(N_TOK // TM,),
        in_specs=[tok(D_MODEL), _layer_spec((1, D_MODEL), l), _layer_spec((D_MODEL, CE), l),
                  tok(2 * LANE), _layer_spec((1, 256), l), _layer_spec((1, LANE), l),
                  _layer_spec((256, 768), l), _layer_spec((LANE, 1536), l)],
        out_specs=[tok(768), tok(768), tspec(6, LANE), tok(768),
                   pl.BlockSpec((B_KV_GROUPS, TM, LANE), lambda i: (0, i, 0)),
                   tok(256), tspec(2, LANE), tok(256), tspec(2, LANE),
                   pl.BlockSpec((1, B_KV_GROUPS * GATE_ROWS, TM), lambda i: (i, 0, 0)),
                   tok(512), tok(512), tspec(4, LANE)],
        out_shape=[bf(768), bf(768), tshape(6, LANE, BF16), bf(768),
                   jax.ShapeDtypeStruct((B_KV_GROUPS, N_TOK, LANE), F32),
                   bf(256), tshape(2, LANE, BF16), bf(256), tshape(2, LANE, BF16),
                   jax.ShapeDtypeStruct((NCH, B_KV_GROUPS * GATE_ROWS, TM), F32),
                   bf(512), bf(512), tshape(4, LANE, BF16)],
        compiler_params=pltpu.CompilerParams(
            dimension_semantics=("arbitrary",), vmem_limit_bytes=VMEM_LIMIT),
        name="proj",
    )(x, gain, w_e, rope_tab, qn, kvn, wuq, wukv)


def _tile_pos(r):
    kk = lax.broadcasted_iota(jnp.int32, (TK, 1), 0)
    tt = lax.broadcasted_iota(jnp.int32, (1, r), 1) & (TQ - 1)
    return kk, tt


def _dot_nt(a, b):
    return lax.dot_general(a, b, (((1,), (1,)), ((), ())), preferred_element_type=F32)


def _score_stage(q, k, mask, m_prev, s_ref, m_ref):
    s = _dot_nt(k, q)
    if mask is not None:
        s = jnp.where(mask, s, MASKED)
    cm = jnp.max(s, axis=0, keepdims=True)
    m_ref[...] = cm if m_prev is None else jnp.maximum(m_prev[...], cm)
    s_ref[...] = s


def _value_stage(vt, s_ref, m_ref, mb_ref, acc_ref):
    m = m_ref[...]
    alpha = jnp.exp2(mb_ref[...] - m)
    p = jnp.exp2(s_ref[...] - m).astype(BF16)
    mb_ref[...] = m
    acc_ref[...] = alpha * acc_ref[...] + _dot(vt, p)


def _causal_attention(streams, i):
    for qt, k_of, vt_of, (s0, s1, m0, m1, mb, acc) in streams:
        kk, tt = _tile_pos(qt.shape[0])
        _score_stage(qt, k_of(i), kk <= tt, None, s0, m0)
        mb[...] = m0[...]
        acc[...] = jnp.zeros_like(acc)

    def step(score_j, pend_j, a, b):
        for qt, k_of, vt_of, sc in streams:
            bufs, maxs, mb, acc = (sc[0], sc[1]), (sc[2], sc[3]), sc[4], sc[5]
            _score_stage(qt, k_of(score_j), None, maxs[a], bufs[b], maxs[b])
            _value_stage(vt_of(pend_j), bufs[a], maxs[a], mb, acc)

    def fold(pend_j, a):
        for qt, k_of, vt_of, sc in streams:
            _value_stage(vt_of(pend_j), sc[a], sc[2 + a], sc[4], sc[5])

    def body(jj, carry):
        step(2 * jj, jnp.where(jj == 0, i, 2 * jj - 1), 0, 1)
        step(2 * jj + 1, 2 * jj, 1, 0)
        return carry

    npairs = i // 2
    lax.fori_loop(0, npairs, body, 0)
    pend = jnp.where(npairs == 0, i, 2 * npairs - 1)

    @pl.when(i % 2 == 1)
    def _():
        step(i - 1, pend, 0, 1)
        fold(i - 1, 1)

    @pl.when(i % 2 == 0)
    def _():
        fold(pend, 0)

    return [st[3][5][...] for st in streams]


def _normalise(acc):
    row = lax.broadcasted_iota(jnp.int32, acc.shape, 0)
    return jnp.where(row < VROW, acc / acc[VROW:VROW + 1, :], 0.0)


def _pipe_scratch(r):
    return ([pltpu.VMEM((TK, r), F32)] * 2 + [pltpu.VMEM((1, r), F32)] * 3
            + [pltpu.VMEM((LANE, r), F32)])


N_PIPE = 6


def _k_chunk(k_ref, j, sl=slice(None)):
    return k_ref[pl.ds(pl.multiple_of(j * TK, TK), TK), sl]


def _q_spec(nb):
    return pl.BlockSpec((None, TQ, nb * LANE), lambda b, h, i: (b, i, h))


_K_SPEC = pl.BlockSpec((None, SEQ, LANE), lambda b, h, i: (b, 0, h))
_VT_SPEC = pl.BlockSpec((None, None, NQ, LANE, TK), lambda b, h, i: (h, b, 0, 0, 0))


def _attn_params():
    return pltpu.CompilerParams(dimension_semantics=("arbitrary",) * 3,
                                vmem_limit_bytes=VMEM_LIMIT)


A_HPS = 2


def _mla_kernel(q_ref, k_ref, vt_ref, o_ref, *scratch):
    i = pl.program_id(2)
    streams = []
    for h in range(A_HPS):
        sl = slice(h * LANE, (h + 1) * LANE)
        streams.append((q_ref[:, sl],
                        lambda j, sl=sl: _k_chunk(k_ref, j, sl),
                        lambda j, h=h: vt_ref[h, j],
                        scratch[N_PIPE * h:N_PIPE * (h + 1)]))
    accs = _causal_attention(streams, i)
    for h in range(A_HPS):
        o_ref[:, h * LANE:(h + 1) * LANE] = _normalise(accs[h]).T.astype(BF16)


def _mla_call(qa, ka, vat):
    w = A_HPS * LANE
    return pl.pallas_call(
        _mla_kernel,
        grid=(BATCH, A_HEADS // A_HPS, NQ),
        in_specs=[_q_spec(A_HPS),
                  pl.BlockSpec((None, SEQ, w), lambda b, h, i: (b, 0, h)),
                  pl.BlockSpec((A_HPS, None, NQ, LANE, TK), lambda b, h, i: (h, b, 0, 0, 0))],
        out_specs=pl.BlockSpec((None, TQ, w), lambda b, h, i: (b, i, h)),
        out_shape=jax.ShapeDtypeStruct((BATCH, SEQ, A_HEADS * LANE), BF16),
        scratch_shapes=_pipe_scratch(TQ) * A_HPS,
        compiler_params=_attn_params(),
        name="mla_attn",
    )(qa, ka, vat)


def _diff_kernel(q_ref, k_ref, vt_ref, lqk_ref, sn_ref, o_ref, *scratch, lam_init):
    i = pl.program_id(2)
    q = q_ref[...]
    lane = lax.broadcasted_iota(jnp.int32, (TQ, LANE), 1)
    zero = jnp.zeros_like(q)
    q2 = jnp.concatenate([jnp.where(lane < C_DH, q, zero),
                          jnp.where((lane >= C_DH) & (lane < 2 * C_DH), q, zero)], axis=0)
    stream = (q2, lambda j: _k_chunk(k_ref, j), lambda j: vt_ref[j], scratch)
    o = _normalise(_causal_attention([stream], i)[0])
    lqk = lqk_ref[...]
    lam = (jnp.exp(jnp.sum(lqk[0:1] * lqk[1:2], axis=-1, keepdims=True))
           - jnp.exp(jnp.sum(lqk[2:3] * lqk[3:4], axis=-1, keepdims=True)) + lam_init)
    d = (o[:, 0:TQ] - lam * o[:, TQ:2 * TQ]).T
    ms = jnp.sum(d * d, axis=-1, keepdims=True) * (1.0 / (2 * C_DH))
    y = d * lax.rsqrt(ms + EPS) * sn_ref[...] * (1.0 - lam_init)
    o_ref[...] = y.astype(BF16)


def _diff_call(l, qc, kd, vct, lqk, sn):
    lam_init = 0.8 - 0.6 * math.exp(-0.3 * l)
    return pl.pallas_call(
        functools.partial(_diff_kernel, lam_init=lam_init),
        grid=(BATCH, C_HEADS, NQ),
        in_specs=[_q_spec(1), _K_SPEC, _VT_SPEC, _layer_spec((8, LANE), l),
                  _layer_spec((1, LANE), l)],
        out_specs=pl.BlockSpec((None, TQ, LANE), lambda b, h, i: (b, i, h)),
        out_shape=jax.ShapeDtypeStruct((BATCH, SEQ, C_HEADS * LANE), BF16),
        scratch_shapes=_pipe_scratch(2 * TQ),
        compiler_params=_attn_params(),
        name="diff_attn",
    )(qc, kd, vct, lqk, sn)


def _nsa_kernel(q_ref, x16_ref, ks_ref, vst_ref, kw_ref, vwt_ref, gt_ref,
                petop_ref, pebot_ref, w1top_ref, w1bot_ref, w2k_ref, w2v_ref, ovt_ref,
                o_ref, kc_s, vct_s, *slc_scratch):
    i = pl.program_id(2)
    r3 = B_HPG * TQ

    @pl.when(i == 0)
    def _():
        x = x16_ref[...]
        yt = _dot((x + petop_ref[...]).astype(BF16), w1top_ref[...])
        yb = _dot((x + pebot_ref[...]).astype(BF16), w1bot_ref[...])
        pre = yt + pltpu.roll(yb, N_CMP_PAD - 1, 0)
        act = (pre * jax.nn.sigmoid(pre)).astype(BF16)
        kc_s[...] = _dot(act, w2k_ref[...]).astype(BF16)
        vct_s[...] = _dot(act, w2v_ref[...]).T.astype(BF16)

    q3 = jnp.concatenate([q_ref[:, h * LANE:(h + 1) * LANE] for h in range(B_HPG)], axis=0)

    kk, tt = _tile_pos(r3)
    c0 = jnp.maximum(i - 1, 0)
    s_diag = jnp.where(kk <= tt, _dot_nt(_k_chunk(kw_ref, i), q3), MASKED)
    s_prev = jnp.where(tt + jnp.where(i >= 1, 0, TQ) < kk,
                       _dot_nt(_k_chunk(kw_ref, c0), q3), MASKED)
    m_win = jnp.maximum(jnp.max(s_diag, axis=0, keepdims=True),
                        jnp.max(s_prev, axis=0, keepdims=True))
    o_win = _normalise(_dot(vwt_ref[i], jnp.exp2(s_diag - m_win).astype(BF16))
                       + _dot(vwt_ref[c0], jnp.exp2(s_prev - m_win).astype(BF16)))

    s = _dot_nt(kc_s[...], q3)
    c_end = lax.broadcasted_iota(jnp.int32, (N_CMP_PAD, 1), 0) * CMP_STRIDE + (CMP_LEN - 1)
    valid = c_end <= i * TQ + tt
    smax = jnp.max(jnp.where(valid, s, MASKED), axis=0, keepdims=True)
    e = jnp.where(valid, jnp.exp2(s - smax), 0.0)
    den = jnp.sum(e, axis=0, keepdims=True)
    p = e / jnp.where(den > 0.0, den, 1.0)
    o_cmp = _dot(vct_s[...], p.astype(BF16))

    psum = p[:, 0:TQ] + p[:, TQ:2 * TQ] + p[:, 2 * TQ:3 * TQ]
    p_hi = psum.astype(BF16)
    p_lo = (psum - p_hi.astype(F32)).astype(BF16)
    ovt = ovt_ref[...]
    imp = (_dot(ovt, p_hi) + _dot(ovt, p_lo))[64:128]
    m_idx = lax.broadcasted_iota(jnp.int32, (N_SLC, TQ), 0)
    cur = (i * TQ + lax.broadcasted_iota(jnp.int32, (N_SLC, TQ), 1)) // SLC_LEN
    forced = (m_idx == 0) | (m_idx == cur) | (m_idx == cur - 1)
    score = jnp.where(forced, 1e9, jnp.where(m_idx <= cur, imp, -1.0))
    sub = lax.broadcasted_iota(jnp.int32, (SUBLANES, TQ), 0)
    bias = [jnp.zeros((64, TQ), F32)]
    for v in range(N_SLC // SUBLANES):
        tile = score[v * SUBLANES:(v + 1) * SUBLANES, :]
        rank = jnp.zeros((SUBLANES, TQ), jnp.int32)
        for mp in range(N_SLC):
            srow = score[mp:mp + 1, :]
            if mp < v * SUBLANES:
                ahead = srow >= tile
            elif mp >= (v + 1) * SUBLANES:
                ahead = srow > tile
            else:
                ahead = (srow > tile) | ((srow == tile) & (sub > mp - v * SUBLANES))
            rank = rank + jnp.where(ahead, 1, 0)
        bias.append(jnp.where(rank < SLC_TOPN, 0.0, -SEL_BIG))
    bias = jnp.concatenate(bias, axis=0).T.astype(BF16)
    q_sel = q3 + jnp.concatenate([bias] * B_HPG, axis=0)

    stream = (q_sel, lambda j: _k_chunk(ks_ref, j), lambda j: vst_ref[j], slc_scratch)
    o_slc = _normalise(_causal_attention([stream], i)[0])

    gate = gt_ref[...]
    for h in range(B_HPG):
        sl = slice(h * TQ, (h + 1) * TQ)
        o = (gate[3 * h:3 * h + 1] * o_cmp[:, sl]
             + gate[3 * h + 1:3 * h + 2] * o_slc[:, sl]
             + gate[3 * h + 2:3 * h + 3] * o_win[:, sl])
        o_ref[:, h * LANE:(h + 1) * LANE] = o.T.astype(BF16)


def _nsa_call(l, qb, x16, ks, vst, kw, vwt, gt, petop, pebot, w1top, w1bot, w2k, w2v, ovt):
    r3 = B_HPG * TQ
    return pl.pallas_call(
        _nsa_kernel,
        grid=(BATCH, B_KV_GROUPS, NQ),
        in_specs=[_q_spec(B_HPG),
                  pl.BlockSpec((None, None, N_CMP_PAD, 16 * LANE), lambda b, g, i: (g, b, 0, 0)),
                  _K_SPEC, _VT_SPEC, _K_SPEC, _VT_SPEC,
                  pl.BlockSpec((None, GATE_ROWS, TQ), lambda b, g, i: (b * NQ + i, g, 0)),
                  _layer_spec((1, 16 * LANE), l), _layer_spec((1, 16 * LANE), l),
                  _layer_spec((16 * LANE, LANE), l), _layer_spec((16 * LANE, LANE), l),
                  _layer_spec((LANE, LANE), l), _layer_spec((LANE, LANE), l),
                  _const_spec((LANE, N_CMP_PAD))],
        out_specs=pl.BlockSpec((None, TQ, B_HPG * LANE), lambda b, g, i: (b, i, g)),
        out_shape=jax.ShapeDtypeStruct((BATCH, SEQ, B_HEADS * LANE), BF16),
        scratch_shapes=[pltpu.VMEM((N_CMP_PAD, LANE), BF16), pltpu.VMEM((LANE, N_CMP_PAD), BF16)]
        + _pipe_scratch(r3),
        compiler_params=_attn_params(),
        name="nsa_attn",
    )(qb, x16, ks, vst, kw, vwt, gt, petop, pebot, w1top, w1bot, w2k, w2v, ovt)


def _expand_w_in(w):
    a = w[..., :A_COLS]
    b = w[..., A_COLS:A_COLS + B_COLS]
    c = w[..., A_COLS + B_COLS:]
    z = lambda n: jnp.zeros(w.shape[:-1] + (n,), w.dtype)
    gate = lambda g: [b[..., 1152 + g * 9:1161 + g * 9], z(GATE_ROWS - 9)]
    parts = [a[..., 0:192], a[..., 320:352]] + gate(0) + gate(1) + [a[..., 192:320]]
    for h in range(B_HEADS):
        parts += [b[..., h * 64:(h + 1) * 64], z(64)]
    for base in (384, 640, 896):
        for g in range(B_KV_GROUPS):
            parts += [b[..., base + g * 64:base + g * 64 + 64],
                      b[..., base + 128 + g * 64:base + 192 + g * 64]]

    def rot_mid(t, o):
        return [t[..., o + 8:o + 24], t[..., o:o + 8], t[..., o + 24:o + 32]]

    for h in range(C_HEADS):
        parts += rot_mid(c, h * 64) + rot_mid(c, h * 64 + 32) + [z(64)]
    for h in range(C_HEADS):
        parts += (rot_mid(c, 256 + h * 64) + rot_mid(c, 288 + h * 64)
                  + [c[..., 512 + h * 64:576 + h * 64]])
    return jnp.concatenate(parts, axis=-1).astype(BF16)


def _expand_w_out(w):
    nh = A_HEADS + B_HEADS + C_HEADS
    w = w.reshape(DEPTH, nh, 64, D_MODEL)
    return jnp.pad(w, ((0, 0), (0, 0), (0, 64), (0, 0))).reshape(DEPTH, nh * LANE, D_MODEL).astype(BF16)


def _rope_table(positions):
    def inv(rot):
        return 1.0 / (ROPE_THETA ** (jnp.arange(0, rot, 2, dtype=F32) / rot))

    fa, fb, fc = inv(A_ROPE), inv(B_ROT), inv(C_ROT)
    z = lambda n: jnp.zeros((n,), F32)
    freq = jnp.concatenate([fb, fb, fc, fc, z(24), fc, fc, z(8), fa, fa, z(32)])
    sign = jnp.concatenate([-jnp.ones(8), jnp.ones(8), -jnp.ones(4), jnp.ones(4), z(24),
                            -jnp.ones(4), jnp.ones(4), z(8), -jnp.ones(16), jnp.ones(16),
                            z(32)]).astype(F32)
    ang = positions.astype(F32).reshape(N_TOK, 1) * freq[None, :]
    return jnp.concatenate([jnp.cos(ang), sign[None, :] * jnp.sin(ang)], axis=1)


def _overlap_t():
    c = jnp.arange(N_CMP_PAD)[None, :]
    m = jnp.arange(N_SLC)[:, None]
    ov = ((c * CMP_STRIDE < m * SLC_LEN + SLC_LEN)
          & (c * CMP_STRIDE + CMP_LEN - 1 >= m * SLC_LEN) & (c < N_CMP))
    return jnp.concatenate([jnp.zeros((64, N_CMP_PAD), BF16), ov.astype(BF16)], axis=0)


def kernel(x, positions, ffn1_norm, ffn1_wg, ffn1_wu, ffn1_wd, mix_norm, w_in, mla_q_norm, mla_kv_norm, mla_w_uq, mla_w_ukv, nsa_pe_k, nsa_pe_v, nsa_phi_k1, nsa_phi_k2, nsa_phi_v1, nsa_phi_v2, diff_lq1, diff_lk1, diff_lq2, diff_lk2, diff_sub_norm, w_out, ffn2_norm, ffn2_wg, ffn2_wu, ffn2_wd, final_norm):
    xf = x.reshape(N_TOK, D_MODEL)
    rope_tab = _rope_table(positions)
    ovt = _overlap_t()
    bf = lambda t: t.astype(BF16)
    row = lambda t: t.reshape(DEPTH, 1, t.shape[-1])
    pad_to = lambda t, axis, n: jnp.pad(
        t, [(0, n - t.shape[a]) if a == axis else (0, 0) for a in range(t.ndim)])

    ffn1 = (row(ffn1_norm), bf(ffn1_wg), bf(ffn1_wu), bf(ffn1_wd))
    ffn2 = (row(ffn2_norm), bf(ffn2_wg), bf(ffn2_wu), bf(ffn2_wd))
    w_e = _expand_w_in(w_in)
    w_o = _expand_w_out(w_out)
    wuq = pad_to(pad_to(mla_w_uq.reshape(DEPTH, A_Q_RANK, A_HEADS, A_NOPE + A_ROPE), 3, LANE), 1, 256)
    wuq = bf(wuq.reshape(DEPTH, 256, A_HEADS * LANE))
    wkv = mla_w_ukv.reshape(DEPTH, A_KV_RANK, A_HEADS, A_NOPE + A_V)
    wukv = bf(jnp.concatenate(
        [pad_to(wkv[..., :A_NOPE], 3, LANE).reshape(DEPTH, A_KV_RANK, A_HEADS * LANE),
         pad_to(wkv[..., A_NOPE:], 3, LANE).reshape(DEPTH, A_KV_RANK, A_HEADS * LANE)], axis=2))
    qn = pad_to(row(mla_q_norm), 2, 256)
    kvn = row(mla_kv_norm)

    k1 = nsa_phi_k1.reshape(DEPTH, CMP_LEN, B_DH, B_DH)
    v1 = nsa_phi_v1.reshape(DEPTH, CMP_LEN, B_DH, B_DH)
    w1 = bf(jnp.concatenate([pad_to(k1, 3, LANE), jnp.pad(v1, ((0, 0), (0, 0), (0, 0), (64, 0)))],
                            axis=2))
    w1top = w1[:, :16].reshape(DEPTH, 16 * LANE, LANE)
    w1bot = w1[:, 16:].reshape(DEPTH, 16 * LANE, LANE)
    pe = jnp.concatenate([nsa_pe_k, nsa_pe_v], axis=2)
    petop = pe[:, :16].reshape(DEPTH, 1, 16 * LANE)
    pebot = pe[:, 16:].reshape(DEPTH, 1, 16 * LANE)
    w2k = bf(pad_to(pad_to(nsa_phi_k2, 1, LANE), 2, LANE))
    w2v = bf(jnp.pad(nsa_phi_v2, ((0, 0), (64, 0), (0, 64))))
    lqk = pad_to(pad_to(jnp.stack([diff_lq1, diff_lk1, diff_lq2, diff_lk2], axis=1), 1, 8), 2, LANE)
    sn = pad_to(row(diff_sub_norm), 2, LANE)

    b3 = lambda t: t.reshape(BATCH, SEQ, t.shape[-1])
    ch = lambda t: t.reshape(t.shape[0], BATCH, NQ, t.shape[2], TK)
    flat = lambda t: t.reshape(N_TOK, t.shape[-1])
    for l in range(DEPTH):
        xf = _ffn_call(l, xf, *ffn1)
        (qa, ka, vat, qb, kvc, ks, vst, kw, vwt, gt, qc, kd, vct) = _proj_call(
            l, xf, row(mix_norm), w_e, rope_tab, qn, kvn, wuq, wukv)
        oa = _mla_call(b3(qa), b3(ka), ch(vat))
        x16 = kvc.reshape(B_KV_GROUPS, BATCH, N_CMP_PAD, 16 * LANE)
        ob = _nsa_call(l, b3(qb), x16, b3(ks), ch(vst), b3(kw), ch(vwt), gt,
                       petop, pebot, w1top, w1bot, w2k, w2v, ovt)
        oc = _diff_call(l, b3(qc), b3(kd), ch(vct), lqk, sn)
        fg = final_norm.reshape(1, D_MODEL) if l == DEPTH - 1 else None
        xf = _ffn_call(l, xf, *ffn2, mix=(flat(oa), flat(ob), flat(oc), w_o), final_gain=fg)
    return xf.reshape(BATCH, SEQ, D_MODEL)
```

```python
import functools
import math

import jax
import jax.numpy as jnp
from jax import lax
from jax.experimental import pallas as pl
from jax.experimental.pallas import tpu as pltpu

F32 = jnp.float32
BF16 = jnp.bfloat16

D_MODEL = 1024
BATCH = 4
SEQ = 4096
DEPTH = 2
N_TOK = BATCH * SEQ
EPS = 1e-6
ROPE_THETA = 500000.0
D_FF = 2816

A_HEADS = 6
A_Q_RANK = 192
A_KV_RANK = 128
A_NOPE = 64
A_ROPE = 32
A_V = 64

B_HEADS = 6
B_KV_GROUPS = 2
B_HPG = 3
B_DH = 64
B_ROT = 16
CMP_LEN = 32
CMP_STRIDE = 16
N_CMP = (SEQ - CMP_LEN) // CMP_STRIDE + 1
N_CMP_PAD = SEQ // CMP_STRIDE
SLC_LEN = 64
N_SLC = SEQ // SLC_LEN
SLC_TOPN = 16
WINDOW = 512

C_HEADS = 4
C_DH = 32
C_ROT = 8

A_COLS = A_Q_RANK + A_KV_RANK + A_ROPE
B_COLS = B_HEADS * B_DH + 6 * B_KV_GROUPS * B_DH + 3 * B_HEADS
C_COLS = 3 * C_HEADS * 2 * C_DH

LANE = 128
SUBLANES = 8
VROW = 64
VT_ROWS = 72
MASKED = -1e30
SEL_BIG = 1e9
LOG2E = 1.0 / math.log(2.0)

TM = 512
TF = 256
NFC = D_FF // TF
TQ = 512
TK = 512
NQ = SEQ // TQ
NCH = N_TOK // TK
GATE_ROWS = 16

CE_A = 0
CE_BQ = 384
CE_KVC = 1152
CE_KSVS = 1408
CE_KWVW = 1664
CE_CQ = 1920
CE_CKV = 2432
CE = 2944

WO_B = A_HEADS * A_V
WO_C = WO_B + B_KV_GROUPS * 2 * LANE
WO_ROWS = WO_C + C_HEADS * 2 * C_DH

VMEM_LIMIT = 56 * 1024 * 1024


def _const_spec(shape):
    nd = len(shape)
    return pl.BlockSpec(shape, lambda *_: (0,) * nd, pipeline_mode=pl.Buffered(1))


def _layer_spec(shape, l):
    nd = len(shape)
    return pl.BlockSpec((None,) + tuple(shape), lambda *_: (l,) + (0,) * nd,
                        pipeline_mode=pl.Buffered(1))


def _rms(x, g, n):
    ms = jnp.sum(x * x, axis=-1, keepdims=True) * (1.0 / n)
    return x * lax.rsqrt(ms + EPS) * g


def _dot(a, b):
    return jnp.dot(a, b, preferred_element_type=F32)


def _ffn_kernel(*refs, mix, final):
    it = iter(refs)
    x_ref = next(it)
    if mix:
        oa_ref, ob_ref, oc_ref, wo_ref = next(it), next(it), next(it), next(it)
    g_ref, wg_ref, wu_ref, wd_ref = next(it), next(it), next(it), next(it)
    if final:
        fg_ref = next(it)
    o_ref = next(it)
    acc_ref = next(it)

    x = x_ref[...]
    if mix:
        x = x + _dot(oa_ref[...], wo_ref[0:WO_B, :])
        x = x + _dot(ob_ref[...], wo_ref[WO_B:WO_C, :])
        x = x + _dot(oc_ref[...], wo_ref[WO_C:WO_ROWS, :])
    xn = _rms(x, g_ref[...], D_MODEL).astype(BF16)
    for c in range(NFC):
        g = _dot(xn, wg_ref[:, c * TF:(c + 1) * TF])
        u = _dot(xn, wu_ref[:, c * TF:(c + 1) * TF])
        h = (g * jax.nn.sigmoid(g) * u).astype(BF16)
        d = _dot(h, wd_ref[c * TF:(c + 1) * TF, :])
        if c == 0:
            acc_ref[...] = d
        else:
            acc_ref[...] += d
    y = x + 0.5 * acc_ref[...]
    if final:
        y = _rms(y, fg_ref[...], D_MODEL)
    o_ref[...] = y


def _ffn_call(l, x, gain, wg, wu, wd, mix=None, final_gain=None):
    tok = lambda w: pl.BlockSpec((TM, w), lambda i: (i, 0))
    args = [x]
    specs = [tok(D_MODEL)]
    if mix is not None:
        oa, ob, oc, wo = mix
        args += [oa, ob, oc, wo]
        specs += [tok(WO_B), tok(WO_C - WO_B), tok(WO_ROWS - WO_C),
                  _layer_spec((WO_ROWS, D_MODEL), l)]
    args += [gain, wg, wu, wd]
    specs += [_layer_spec((1, D_MODEL), l), _layer_spec((D_MODEL, D_FF), l),
              _layer_spec((D_MODEL, D_FF), l), _layer_spec((D_FF, D_MODEL), l)]
    if final_gain is not None:
        args.append(final_gain)
        specs.append(_const_spec((1, D_MODEL)))
    kern = functools.partial(_ffn_kernel, mix=mix is not None, final=final_gain is not None)
    return pl.pallas_call(
        kern,
        grid=(N_TOK // TM,),
        in_specs=specs,
        out_specs=tok(D_MODEL),
        out_shape=jax.ShapeDtypeStruct((N_TOK, D_MODEL), F32),
        scratch_shapes=[pltpu.VMEM((TM, D_MODEL), F32)],
        compiler_params=pltpu.CompilerParams(
            dimension_semantics=("arbitrary",), vmem_limit_bytes=VMEM_LIMIT),
        name="ffn",
    )(*args)


ROPE_HALF = (16, 8, 4)


def _proj_kernel(x_ref, g_ref, w_ref, rope_ref, qn_ref, kvn_ref, wuq_ref, wukv_ref,
                 qa_ref, ka_ref, vat_ref, qb_ref, kvc_ref, ks_ref, vst_ref, kw_ref, vwt_ref,
                 gt_ref, qc_ref, kd_ref, vct_ref):
    xn = _rms(x_ref[...], g_ref[...], D_MODEL).astype(BF16)
    lane = lax.broadcasted_iota(jnp.int32, (TM, LANE), 1)
    lo = lane < 64
    between = lambda a, b: (lane >= a) & (lane < b)
    rot = (between(64, 96), lane < 16, between(16, 24) | between(48, 56))
    second = (between(80, 96), between(8, 16), between(20, 24) | between(52, 56))
    cos_all = rope_ref[:, 0:LANE]
    sin_all = rope_ref[:, LANE:2 * LANE]
    cos = [jnp.where(m, cos_all, 1.0) for m in rot]
    sin = [jnp.where(m, sin_all, 0.0) for m in rot]
    ones_row = (lane == VROW).astype(F32)

    def rope(v, typ):
        r = ROPE_HALF[typ]
        partner = jnp.where(second[typ], pltpu.roll(v, r, 1), pltpu.roll(v, LANE - r, 1))
        return v * cos[typ] + partner * sin[typ]

    def blk(v, b):
        return v[:, b * LANE:(b + 1) * LANE]

    def tr(v):
        return v.T.astype(BF16)

    def upper_as_vt(v):
        return tr(jnp.where(lo, pltpu.roll(v, 64, 1), ones_row))

    ha = _dot(xn, w_ref[:, CE_A:CE_A + 384])
    a1 = blk(ha, 1)
    cq = jnp.concatenate([blk(ha, 0), jnp.where(lo, a1, 0.0)], axis=1)
    cqn = _rms(cq, qn_ref[...], A_Q_RANK).astype(BF16)
    ckvn = _rms(blk(ha, 2), kvn_ref[...], A_KV_RANK).astype(BF16)
    kpe = jnp.where(rot[0], rope(a1, 0), 0.0)
    gt_ref[0] = jax.nn.sigmoid(a1).T[96:128, :]
    q = _dot(cqn, wuq_ref[...])
    kv = _dot(ckvn, wukv_ref[...])
    scale_a = (A_NOPE + A_ROPE) ** -0.5 * LOG2E
    for h in range(A_HEADS):
        qa_ref[:, h * LANE:(h + 1) * LANE] = (rope(blk(q, h), 0) * scale_a).astype(BF16)
        ka_ref[:, h * LANE:(h + 1) * LANE] = (blk(kv, h) + kpe).astype(BF16)
        vat_ref[h, 0] = tr(blk(kv, A_HEADS + h) + ones_row)

    hq = _dot(xn, w_ref[:, CE_BQ:CE_BQ + 768])
    scale_b = B_DH ** -0.5 * LOG2E
    for b in range(B_HEADS):
        qb_ref[:, b * LANE:(b + 1) * LANE] = (rope(blk(hq, b), 1) * scale_b).astype(BF16)
    hk = _dot(xn, w_ref[:, CE_KVC:CE_KVC + 256])
    for g in range(B_KV_GROUPS):
        kvc_ref[g] = rope(blk(hk, g), 1)
    tpos = (pl.program_id(0) % (SEQ // TM)) * TM + lax.broadcasted_iota(jnp.int32, (TM, LANE), 0)
    onehot = ((lane - 64) == (tpos // SLC_LEN)).astype(F32)
    hk = _dot(xn, w_ref[:, CE_KSVS:CE_KSVS + 256])
    for g in range(B_KV_GROUPS):
        ks_ref[:, g * LANE:(g + 1) * LANE] = jnp.where(lo, rope(blk(hk, g), 1), onehot).astype(BF16)
        vst_ref[g, 0] = upper_as_vt(blk(hk, g))
    hk = _dot(xn, w_ref[:, CE_KWVW:CE_KWVW + 256])
    for g in range(B_KV_GROUPS):
        kw_ref[:, g * LANE:(g + 1) * LANE] = rope(blk(hk, g), 1).astype(BF16)
        vwt_ref[g, 0] = upper_as_vt(blk(hk, g))

    hc = _dot(xn, w_ref[:, CE_CQ:CE_CQ + 512])
    scale_c = C_DH ** -0.5 * LOG2E
    for h in range(C_HEADS):
        qc_ref[:, h * LANE:(h + 1) * LANE] = (rope(blk(hc, h), 2) * scale_c).astype(BF16)
    hc = _dot(xn, w_ref[:, CE_CKV:CE_CKV + 512])
    for h in range(C_HEADS):
        kv_blk = rope(blk(hc, h), 2)
        kd_ref[:, h * LANE:(h + 1) * LANE] = kv_blk.astype(BF16)
        vct_ref[h, 0] = upper_as_vt(kv_blk)


def _proj_call(l, x, gain, w_e, rope_tab, qn, kvn, wuq, wukv):
    tok = lambda w: pl.BlockSpec((TM, w), lambda i: (i, 0))
    bf = lambda w: jax.ShapeDtypeStruct((N_TOK, w), BF16)
    tspec = lambda nb, rows: pl.BlockSpec((nb, 1, rows, TM), lambda i: (0, i, 0, 0))
    tshape = lambda nb, rows, dt: jax.ShapeDtypeStruct((nb, NCH, rows, TM), dt)
    return pl.pallas_call(
        _proj_kernel,
        grid=(N_TOK // TM,),
        in_specs=[tok(D_MODEL), _layer_spec((1, D_MODEL), l), _layer_spec((D_MODEL, CE), l),
                  tok(2 * LANE), _layer_spec((1, 256), l), _layer_spec((1, LANE), l),
                  _layer_spec((256, 768), l), _layer_spec((LANE, 1536), l)],
        out_specs=[tok(768), tok(768), tspec(6, LANE), tok(768),
                   pl.BlockSpec((B_KV_GROUPS, TM, LANE), lambda i: (0, i, 0)),
                   tok(256), tspec(2, LANE), tok(256), tspec(2, LANE),
                   pl.BlockSpec((1, B_KV_GROUPS * GATE_ROWS, TM), lambda i: (i, 0, 0)),
                   tok(512), tok(512), tspec(4, LANE)],
        out_shape=[bf(768), bf(768), tshape(6, LANE, BF16), bf(768),
                   jax.ShapeDtypeStruct((B_KV_GROUPS, N_TOK, LANE), F32),
                   bf(256), tshape(2, LANE, BF16), bf(256), tshape(2, LANE, BF16),
                   jax.ShapeDtypeStruct((NCH, B_KV_GROUPS * GATE_ROWS, TM), F32),
                   bf(512), bf(512), tshape(4, LANE, BF16)],
        compiler_params=pltpu.CompilerParams(
            dimension_semantics=("arbitrary",), vmem_limit_bytes=VMEM_LIMIT),
        name="proj",
    )(x, gain, w_e, rope_tab, qn, kvn, wuq, wukv)


def _tile_pos(r):
    kk = lax.broadcasted_iota(jnp.int32, (TK, 1), 0)
    tt = lax.broadcasted_iota(jnp.int32, (1, r), 1) & (TQ - 1)
    return kk, tt


def _dot_nt(a, b):
    return lax.dot_general(a, b, (((1,), (1,)), ((), ())), preferred_element_type=F32)


def _score_stage(q, k, mask, m_prev, s_ref, m_ref):
    s = _dot_nt(k, q)
    if mask is not None:
        s = jnp.where(mask, s, MASKED)
    cm = jnp.max(s, axis=0, keepdims=True)
    m_ref[...] = cm if m_prev is None else jnp.maximum(m_prev[...], cm)
    s_ref[...] = s


def _value_stage(vt, s_ref, m_ref, mb_ref, acc_ref):
    m = m_ref[...]
    alpha = jnp.exp2(mb_ref[...] - m)
    p = jnp.exp2(s_ref[...] - m).astype(BF16)
    mb_ref[...] = m
    acc_ref[...] = alpha * acc_ref[...] + _dot(vt, p)


def _causal_attention(streams, i):
    for qt, k_of, vt_of, (s0, s1, m0, m1, mb, acc) in streams:
        kk, tt = _tile_pos(qt.shape[0])
        _score_stage(qt, k_of(i), kk <= tt, None, s0, m0)
        mb[...] = m0[...]
        acc[...] = jnp.zeros_like(acc)

    def step(score_j, pend_j, a, b):
        for qt, k_of, vt_of, sc in streams:
            bufs, maxs, mb, acc = (sc[0], sc[1]), (sc[2], sc[3]), sc[4], sc[5]
            _score_stage(qt, k_of(score_j), None, maxs[a], bufs[b], maxs[b])
            _value_stage(vt_of(pend_j), bufs[a], maxs[a], mb, acc)

    def fold(pend_j, a):
        for qt, k_of, vt_of, sc in streams:
            _value_stage(vt_of(pend_j), sc[a], sc[2 + a], sc[4], sc[5])

    def body(jj, carry):
        step(2 * jj, jnp.where(jj == 0, i, 2 * jj - 1), 0, 1)
        step(2 * jj + 1, 2 * jj, 1, 0)
        return carry

    npairs = i // 2
    lax.fori_loop(0, npairs, body, 0)
    pend = jnp.where(npairs == 0, i, 2 * npairs - 1)

    @pl.when(i % 2 == 1)
    def _():
        step(i - 1, pend, 0, 1)
        fold(i - 1, 1)

    @pl.when(i % 2 == 0)
    def _():
        fold(pend, 0)

    return [st[3][5][...] for st in streams]


def _normalise(acc):
    return acc[0:VROW] / acc[VROW:VROW + 1, :]


def _pipe_scratch(r):
    return ([pltpu.VMEM((TK, r), F32)] * 2 + [pltpu.VMEM((1, r), F32)] * 3
            + [pltpu.VMEM((VT_ROWS, r), F32)])


N_PIPE = 6


def _k_chunk(k_ref, j, sl=slice(None)):
    return k_ref[pl.ds(pl.multiple_of(j * TK, TK), TK), sl]


def _q_spec(nb):
    return pl.BlockSpec((None, TQ, nb * LANE), lambda b, h, i: (b, i, h))


_K_SPEC = pl.BlockSpec((None, SEQ, LANE), lambda b, h, i: (b, 0, h))
_VT_SPEC = pl.BlockSpec((None, None, NQ, LANE, TK), lambda b, h, i: (h, b, 0, 0, 0))


def _attn_params():
    return pltpu.CompilerParams(dimension_semantics=("arbitrary",) * 3,
                                vmem_limit_bytes=VMEM_LIMIT)


A_HPS = 2


def _mla_kernel(q_ref, k_ref, vt_ref, o_ref, *scratch):
    i = pl.program_id(2)
    streams = []
    for h in range(A_HPS):
        sl = slice(h * LANE, (h + 1) * LANE)
        streams.append((q_ref[:, sl],
                        lambda j, sl=sl: _k_chunk(k_ref, j, sl),
                        lambda j, h=h: vt_ref[h, j, 0:VT_ROWS, :],
                        scratch[N_PIPE * h:N_PIPE * (h + 1)]))
    accs = _causal_attention(streams, i)
    o_ref[...] = jnp.concatenate([_normalise(a) for a in accs], axis=0).T.astype(BF16)


def _mla_call(qa, ka, vat):
    w = A_HPS * LANE
    return pl.pallas_call(
        _mla_kernel,
        grid=(BATCH, A_HEADS // A_HPS, NQ),
        in_specs=[_q_spec(A_HPS),
                  pl.BlockSpec((None, SEQ, w), lambda b, h, i: (b, 0, h)),
                  pl.BlockSpec((A_HPS, None, NQ, LANE, TK), lambda b, h, i: (h, b, 0, 0, 0))],
        out_specs=pl.BlockSpec((None, TQ, LANE), lambda b, h, i: (b, i, h)),
        out_shape=jax.ShapeDtypeStruct((BATCH, SEQ, A_HEADS * A_V), BF16),
        scratch_shapes=_pipe_scratch(TQ) * A_HPS,
        compiler_params=_attn_params(),
        name="mla_attn",
    )(qa, ka, vat)


C_HPS = 2


def _diff_kernel(q_ref, k_ref, vt_ref, lqk_ref, sn_ref, o_ref, *scratch, lam_init):
    i = pl.program_id(2)
    lane = lax.broadcasted_iota(jnp.int32, (TQ, LANE), 1)
    streams = []
    for h in range(C_HPS):
        sl = slice(h * LANE, (h + 1) * LANE)
        q = q_ref[:, sl]
        zero = jnp.zeros_like(q)
        q2 = jnp.concatenate([jnp.where(lane < C_DH, q, zero),
                              jnp.where((lane >= C_DH) & (lane < 2 * C_DH), q, zero)], axis=0)
        streams.append((q2,
                        lambda j, sl=sl: _k_chunk(k_ref, j, sl),
                        lambda j, h=h: vt_ref[h, j, 0:VT_ROWS, :],
                        scratch[N_PIPE * h:N_PIPE * (h + 1)]))
    accs = _causal_attention(streams, i)
    lqk = lqk_ref[...]
    lam = (jnp.exp(jnp.sum(lqk[0:1] * lqk[1:2], axis=-1, keepdims=True))
           - jnp.exp(jnp.sum(lqk[2:3] * lqk[3:4], axis=-1, keepdims=True)) + lam_init)
    diffs = []
    for a in accs:
        o = _normalise(a)
        diffs.append(o[:, 0:TQ] - lam * o[:, TQ:2 * TQ])
    d = jnp.concatenate(diffs, axis=0).T
    d2 = d * d
    first = lane < 2 * C_DH
    ms0 = jnp.sum(jnp.where(first, d2, 0.0), axis=-1, keepdims=True)
    ms1 = jnp.sum(jnp.where(first, 0.0, d2), axis=-1, keepdims=True)
    ms = jnp.where(first, ms0, ms1) * (1.0 / (2 * C_DH))
    y = d * lax.rsqrt(ms + EPS) * sn_ref[...] * (1.0 - lam_init)
    o_ref[...] = y.astype(BF16)


def _diff_call(l, qc, kd, vct, lqk, sn):
    lam_init = 0.8 - 0.6 * math.exp(-0.3 * l)
    w = C_HPS * LANE
    return pl.pallas_call(
        functools.partial(_diff_kernel, lam_init=lam_init),
        grid=(BATCH, C_HEADS // C_HPS, NQ),
        in_specs=[_q_spec(C_HPS),
                  pl.BlockSpec((None, SEQ, w), lambda b, h, i: (b, 0, h)),
                  pl.BlockSpec((C_HPS, None, NQ, LANE, TK), lambda b, h, i: (h, b, 0, 0, 0)),
                  _layer_spec((8, LANE), l), _layer_spec((1, LANE), l)],
        out_specs=pl.BlockSpec((None, TQ, LANE), lambda b, h, i: (b, i, h)),
        out_shape=jax.ShapeDtypeStruct((BATCH, SEQ, C_HEADS * 2 * C_DH), BF16),
        scratch_shapes=_pipe_scratch(2 * TQ) * C_HPS,
        compiler_params=_attn_params(),
        name="diff_attn",
    )(qc, kd, vct, lqk, sn)


def _nsa_kernel(q_ref, x16_ref, ks_ref, vst_ref, kw_ref, vwt_ref, gt_ref,
                petop_ref, pebot_ref, w1top_ref, w1bot_ref, w2k_ref, w2v_ref, ovt_ref,
                o_ref, kc_s, vct_s, *slc_scratch):
    i = pl.program_id(2)
    r3 = B_HPG * TQ

    @pl.when(i == 0)
    def _():
        x = x16_ref[...]
        yt = _dot((x + petop_ref[...]).astype(BF16), w1top_ref[...])
        yb = _dot((x + pebot_ref[...]).astype(BF16), w1bot_ref[...])
        pre = yt + pltpu.roll(yb, N_CMP_PAD - 1, 0)
        act = (pre * jax.nn.sigmoid(pre)).astype(BF16)
        kc_s[...] = _dot(act, w2k_ref[...]).astype(BF16)
        vct_s[...] = _dot(act, w2v_ref[...]).T[0:VROW].astype(BF16)

    q3 = jnp.concatenate([q_ref[:, h * LANE:(h + 1) * LANE] for h in range(B_HPG)], axis=0)

    kk, tt = _tile_pos(r3)
    c0 = jnp.maximum(i - 1, 0)
    s_diag = jnp.where(kk <= tt, _dot_nt(_k_chunk(kw_ref, i), q3), MASKED)
    s_prev = jnp.where(tt + jnp.where(i >= 1, 0, TQ) < kk,
                       _dot_nt(_k_chunk(kw_ref, c0), q3), MASKED)
    m_win = jnp.maximum(jnp.max(s_diag, axis=0, keepdims=True),
                        jnp.max(s_prev, axis=0, keepdims=True))
    o_win = _normalise(_dot(vwt_ref[i, 0:VT_ROWS, :], jnp.exp2(s_diag - m_win).astype(BF16))
                       + _dot(vwt_ref[c0, 0:VT_ROWS, :], jnp.exp2(s_prev - m_win).astype(BF16)))

    s = _dot_nt(kc_s[...], q3)
    c_end = lax.broadcasted_iota(jnp.int32, (N_CMP_PAD, 1), 0) * CMP_STRIDE + (CMP_LEN - 1)
    valid = c_end <= i * TQ + tt
    smax = jnp.max(jnp.where(valid, s, MASKED), axis=0, keepdims=True)
    e = jnp.where(valid, jnp.exp2(s - smax), 0.0)
    den = jnp.sum(e, axis=0, keepdims=True)
    p = e / jnp.where(den > 0.0, den, 1.0)
    o_cmp = _dot(vct_s[...], p.astype(BF16))

    psum = p[:, 0:TQ] + p[:, TQ:2 * TQ] + p[:, 2 * TQ:3 * TQ]
    p_hi = psum.astype(BF16)
    p_lo = (psum - p_hi.astype(F32)).astype(BF16)
    ovt = ovt_ref[...]
    imp = (_dot(ovt, p_hi) + _dot(ovt, p_lo))[64:128]
    m_idx = lax.broadcasted_iota(jnp.int32, (N_SLC, TQ), 0)
    cur = (i * TQ + lax.broadcasted_iota(jnp.int32, (N_SLC, TQ), 1)) // SLC_LEN
    forced = (m_idx == 0) | (m_idx == cur) | (m_idx == cur - 1)
    score = jnp.where(forced, 1e9, jnp.where(m_idx <= cur, imp, -1.0))
    sub = lax.broadcasted_iota(jnp.int32, (SUBLANES, TQ), 0)
    bias = [jnp.zeros((64, TQ), F32)]
    for v in range(N_SLC // SUBLANES):
        tile = score[v * SUBLANES:(v + 1) * SUBLANES, :]
        rank = jnp.zeros((SUBLANES, TQ), jnp.int32)
        for mp in range(N_SLC):
            srow = score[mp:mp + 1, :]
            if mp < v * SUBLANES:
                ahead = srow >= tile
            elif mp >= (v + 1) * SUBLANES:
                ahead = srow > tile
            else:
                ahead = (srow > tile) | ((srow == tile) & (sub > mp - v * SUBLANES))
            rank = rank + jnp.where(ahead, 1, 0)
        bias.append(jnp.where(rank < SLC_TOPN, 0.0, -SEL_BIG))
    bias = jnp.concatenate(bias, axis=0).T.astype(BF16)
    q_sel = q3 + jnp.concatenate([bias] * B_HPG, axis=0)

    stream = (q_sel, lambda j: _k_chunk(ks_ref, j), lambda j: vst_ref[j, 0:VT_ROWS, :],
              slc_scratch)
    o_slc = _normalise(_causal_attention([stream], i)[0])

    gate = gt_ref[...]
    outs = []
    for h in range(B_HPG):
        sl = slice(h * TQ, (h + 1) * TQ)
        outs.append(gate[3 * h:3 * h + 1] * o_cmp[:, sl]
                    + gate[3 * h + 1:3 * h + 2] * o_slc[:, sl]
                    + gate[3 * h + 2:3 * h + 3] * o_win[:, sl])
    o_ref[:, 0:LANE] = jnp.concatenate(outs[0:2], axis=0).T.astype(BF16)
    o_ref[:, LANE:2 * LANE] = jnp.concatenate(
        [outs[2], jnp.zeros((VROW, TQ), F32)], axis=0).T.astype(BF16)


def _nsa_call(l, qb, x16, ks, vst, kw, vwt, gt, petop, pebot, w1top, w1bot, w2k, w2v, ovt):
    r3 = B_HPG * TQ
    return pl.pallas_call(
        _nsa_kernel,
        grid=(BATCH, B_KV_GROUPS, NQ),
        in_specs=[_q_spec(B_HPG),
                  pl.BlockSpec((None, None, N_CMP_PAD, 16 * LANE), lambda b, g, i: (g, b, 0, 0)),
                  _K_SPEC, _VT_SPEC, _K_SPEC, _VT_SPEC,
                  pl.BlockSpec((None, GATE_ROWS, TQ), lambda b, g, i: (b * NQ + i, g, 0)),
                  _layer_spec((1, 16 * LANE), l), _layer_spec((1, 16 * LANE), l),
                  _layer_spec((16 * LANE, LANE), l), _layer_spec((16 * LANE, LANE), l),
                  _layer_spec((LANE, LANE), l), _layer_spec((LANE, LANE), l),
                  _const_spec((LANE, N_CMP_PAD))],
        out_specs=pl.BlockSpec((None, TQ, 2 * LANE), lambda b, g, i: (b, i, g)),
        out_shape=jax.ShapeDtypeStruct((BATCH, SEQ, B_KV_GROUPS * 2 * LANE), BF16),
        scratch_shapes=[pltpu.VMEM((N_CMP_PAD, LANE), BF16), pltpu.VMEM((VROW, N_CMP_PAD), BF16)]
        + _pipe_scratch(r3),
        compiler_params=_attn_params(),
        name="nsa_attn",
    )(qb, x16, ks, vst, kw, vwt, gt, petop, pebot, w1top, w1bot, w2k, w2v, ovt)


def _expand_w_in(w):
    a = w[..., :A_COLS]
    b = w[..., A_COLS:A_COLS + B_COLS]
    c = w[..., A_COLS + B_COLS:]
    z = lambda n: jnp.zeros(w.shape[:-1] + (n,), w.dtype)
    gate = lambda g: [b[..., 1152 + g * 9:1161 + g * 9], z(GATE_ROWS - 9)]
    parts = [a[..., 0:192], a[..., 320:352]] + gate(0) + gate(1) + [a[..., 192:320]]
    for h in range(B_HEADS):
        parts += [b[..., h * 64:(h + 1) * 64], z(64)]
    for base in (384, 640, 896):
        for g in range(B_KV_GROUPS):
            parts += [b[..., base + g * 64:base + g * 64 + 64],
                      b[..., base + 128 + g * 64:base + 192 + g * 64]]

    def rot_mid(t, o):
        return [t[..., o + 8:o + 24], t[..., o:o + 8], t[..., o + 24:o + 32]]

    for h in range(C_HEADS):
        parts += rot_mid(c, h * 64) + rot_mid(c, h * 64 + 32) + [z(64)]
    for h in range(C_HEADS):
        parts += (rot_mid(c, 256 + h * 64) + rot_mid(c, 288 + h * 64)
                  + [c[..., 512 + h * 64:576 + h * 64]])
    return jnp.concatenate(parts, axis=-1).astype(BF16)


def _expand_w_out(w):
    z = jnp.zeros((DEPTH, 64, D_MODEL), w.dtype)
    gw = B_HPG * B_DH
    return jnp.concatenate([w[:, 0:WO_B], w[:, WO_B:WO_B + gw], z, w[:, WO_B + gw:WO_B + 2 * gw], z,
                            w[:, WO_B + 2 * gw:]], axis=1).astype(BF16)


def _rope_table(positions):
    def inv(rot):
        return 1.0 / (ROPE_THETA ** (jnp.arange(0, rot, 2, dtype=F32) / rot))

    fa, fb, fc = inv(A_ROPE), inv(B_ROT), inv(C_ROT)
    z = lambda n: jnp.zeros((n,), F32)
    freq = jnp.concatenate([fb, fb, fc, fc, z(24), fc, fc, z(8), fa, fa, z(32)])
    sign = jnp.concatenate([-jnp.ones(8), jnp.ones(8), -jnp.ones(4), jnp.ones(4), z(24),
                            -jnp.ones(4), jnp.ones(4), z(8), -jnp.ones(16), jnp.ones(16),
                            z(32)]).astype(F32)
    ang = positions.astype(F32).reshape(N_TOK, 1) * freq[None, :]
    return jnp.concatenate([jnp.cos(ang), sign[None, :] * jnp.sin(ang)], axis=1)


def _overlap_t():
    c = jnp.arange(N_CMP_PAD)[None, :]
    m = jnp.arange(N_SLC)[:, None]
    ov = ((c * CMP_STRIDE < m * SLC_LEN + SLC_LEN)
          & (c * CMP_STRIDE + CMP_LEN - 1 >= m * SLC_LEN) & (c < N_CMP))
    return jnp.concatenate([jnp.zeros((64, N_CMP_PAD), BF16), ov.astype(BF16)], axis=0)


def kernel(x, positions, ffn1_norm, ffn1_wg, ffn1_wu, ffn1_wd, mix_norm, w_in, mla_q_norm, mla_kv_norm, mla_w_uq, mla_w_ukv, nsa_pe_k, nsa_pe_v, nsa_phi_k1, nsa_phi_k2, nsa_phi_v1, nsa_phi_v2, diff_lq1, diff_lk1, diff_lq2, diff_lk2, diff_sub_norm, w_out, ffn2_norm, ffn2_wg, ffn2_wu, ffn2_wd, final_norm):
    xf = x.reshape(N_TOK, D_MODEL)
    rope_tab = _rope_table(positions)
    ovt = _overlap_t()
    bf = lambda t: t.astype(BF16)
    row = lambda t: t.reshape(DEPTH, 1, t.shape[-1])
    pad_to = lambda t, axis, n: jnp.pad(
        t, [(0, n - t.shape[a]) if a == axis else (0, 0) for a in range(t.ndim)])

    ffn1 = (row(ffn1_norm), bf(ffn1_wg), bf(ffn1_wu), bf(ffn1_wd))
    ffn2 = (row(ffn2_norm), bf(ffn2_wg), bf(ffn2_wu), bf(ffn2_wd))
    w_e = _expand_w_in(w_in)
    w_o = _expand_w_out(w_out)
    wuq = pad_to(pad_to(mla_w_uq.reshape(DEPTH, A_Q_RANK, A_HEADS, A_NOPE + A_ROPE), 3, LANE), 1, 256)
    wuq = bf(wuq.reshape(DEPTH, 256, A_HEADS * LANE))
    wkv = mla_w_ukv.reshape(DEPTH, A_KV_RANK, A_HEADS, A_NOPE + A_V)
    wukv = bf(jnp.concatenate(
        [pad_to(wkv[..., :A_NOPE], 3, LANE).reshape(DEPTH, A_KV_RANK, A_HEADS * LANE),
         pad_to(wkv[..., A_NOPE:], 3, LANE).reshape(DEPTH, A_KV_RANK, A_HEADS * LANE)], axis=2))
    qn = pad_to(row(mla_q_norm), 2, 256)
    kvn = row(mla_kv_norm)

    k1 = nsa_phi_k1.reshape(DEPTH, CMP_LEN, B_DH, B_DH)
    v1 = nsa_phi_v1.reshape(DEPTH, CMP_LEN, B_DH, B_DH)
    w1 = bf(jnp.concatenate([pad_to(k1, 3, LANE), jnp.pad(v1, ((0, 0), (0, 0), (0, 0), (64, 0)))],
                            axis=2))
    w1top = w1[:, :16].reshape(DEPTH, 16 * LANE, LANE)
    w1bot = w1[:, 16:].reshape(DEPTH, 16 * LANE, LANE)
    pe = jnp.concatenate([nsa_pe_k, nsa_pe_v], axis=2)
    petop = pe[:, :16].reshape(DEPTH, 1, 16 * LANE)
    pebot = pe[:, 16:].reshape(DEPTH, 1, 16 * LANE)
    w2k = bf(pad_to(pad_to(nsa_phi_k2, 1, LANE), 2, LANE))
    w2v = bf(jnp.pad(nsa_phi_v2, ((0, 0), (64, 0), (0, 64))))
    lqk = pad_to(pad_to(jnp.stack([diff_lq1, diff_lk1, diff_lq2, diff_lk2], axis=1), 1, 8), 2, LANE)
    sn = jnp.tile(row(diff_sub_norm), (1, 1, C_HPS))

    b3 = lambda t: t.reshape(BATCH, SEQ, t.shape[-1])
    ch = lambda t: t.reshape(t.shape[0], BATCH, NQ, t.shape[2], TK)
    flat = lambda t: t.reshape(N_TOK, t.shape[-1])
    for l in range(DEPTH):
        xf = _ffn_call(l, xf, *ffn1)
        (qa, ka, vat, qb, kvc, ks, vst, kw, vwt, gt, qc, kd, vct) = _proj_call(
            l, xf, row(mix_norm), w_e, rope_tab, qn, kvn, wuq, wukv)
        oa = _mla_call(b3(qa), b3(ka), ch(vat))
        x16 = kvc.reshape(B_KV_GROUPS, BATCH, N_CMP_PAD, 16 * LANE)
        ob = _nsa_call(l, b3(qb), x16, b3(ks), ch(vst), b3(kw), ch(vwt), gt,
                       petop, pebot, w1top, w1bot, w2k, w2v, ovt)
        oc = _diff_call(l, b3(qc), b3(kd), ch(vct), lqk, sn)
        fg = final_norm.reshape(1, D_MODEL) if l == DEPTH - 1 else None
        xf = _ffn_call(l, xf, *ffn2, mix=(flat(oa), flat(ob), flat(oc), w_o), final_gain=fg)
    return xf.reshape(BATCH, SEQ, D_MODEL)
```

```python
import functools
import math

import jax
import jax.numpy as jnp
from jax import lax
from jax.experimental import pallas as pl
from jax.experimental.pallas import tpu as pltpu

F32 = jnp.float32
BF16 = jnp.bfloat16

D_MODEL = 1024
BATCH = 4
SEQ = 4096
DEPTH = 2
N_TOK = BATCH * SEQ
EPS = 1e-6
ROPE_THETA = 500000.0
D_FF = 2816

A_HEADS = 6
A_Q_RANK = 192
A_KV_RANK = 128
A_NOPE = 64
A_ROPE = 32
A_V = 64

B_HEADS = 6
B_KV_GROUPS = 2
B_HPG = 3
B_DH = 64
B_ROT = 16
CMP_LEN = 32
CMP_STRIDE = 16
N_CMP = (SEQ - CMP_LEN) // CMP_STRIDE + 1
N_CMP_PAD = SEQ // CMP_STRIDE
SLC_LEN = 64
N_SLC = SEQ // SLC_LEN
SLC_TOPN = 16
WINDOW = 512

C_HEADS = 4
C_DH = 32
C_ROT = 8

A_COLS = A_Q_RANK + A_KV_RANK + A_ROPE
B_COLS = B_HEADS * B_DH + 6 * B_KV_GROUPS * B_DH + 3 * B_HEADS
C_COLS = 3 * C_HEADS * 2 * C_DH

LANE = 128
SUBLANES = 8
VROW = 64
VT_ROWS = 72
MASKED = -1e30
SEL_BIG = 1e9
LOG2E = 1.0 / math.log(2.0)

TM = 512
TF = 256
NFC = D_FF // TF
TQ = 512
TK = 512
NQ = SEQ // TQ
NCH = N_TOK // TK
GATE_ROWS = 16

CE_A = 0
CE_BQ = 384
CE_KVC = 1152
CE_KSVS = 1408
CE_KWVW = 1664
CE_CQ = 1920
CE_CKV = 2432
CE = 2944

WO_B = A_HEADS * A_V
WO_C = WO_B + B_KV_GROUPS * 2 * LANE
WO_ROWS = WO_C + C_HEADS * 2 * C_DH

VMEM_LIMIT = 56 * 1024 * 1024


def _const_spec(shape):
    nd = len(shape)
    return pl.BlockSpec(shape, lambda *_: (0,) * nd, pipeline_mode=pl.Buffered(1))


def _layer_spec(shape, l):
    nd = len(shape)
    return pl.BlockSpec((None,) + tuple(shape), lambda *_: (l,) + (0,) * nd,
                        pipeline_mode=pl.Buffered(1))


def _rms(x, g, n):
    ms = jnp.sum(x * x, axis=-1, keepdims=True) * (1.0 / n)
    return x * lax.rsqrt(ms + EPS) * g


def _dot(a, b):
    return jnp.dot(a, b, preferred_element_type=F32)


def _ffn_kernel(*refs, mix, final):
    it = iter(refs)
    x_ref = next(it)
    if mix:
        oa_ref, ob_ref, oc_ref, wo_ref = next(it), next(it), next(it), next(it)
    g_ref, wg_ref, wu_ref, wd_ref = next(it), next(it), next(it), next(it)
    if final:
        fg_ref = next(it)
    o_ref = next(it)
    acc_ref = next(it)

    x = x_ref[...]
    if mix:
        x = x + _dot(oa_ref[...], wo_ref[0:WO_B, :])
        x = x + _dot(ob_ref[...], wo_ref[WO_B:WO_C, :])
        x = x + _dot(oc_ref[...], wo_ref[WO_C:WO_ROWS, :])
    xn = _rms(x, g_ref[...], D_MODEL).astype(BF16)
    for c in range(NFC):
        g = _dot(xn, wg_ref[:, c * TF:(c + 1) * TF])
        u = _dot(xn, wu_ref[:, c * TF:(c + 1) * TF])
        h = (g * jax.nn.sigmoid(g) * u).astype(BF16)
        d = _dot(h, wd_ref[c * TF:(c + 1) * TF, :])
        if c == 0:
            acc_ref[...] = d
        else:
            acc_ref[...] += d
    y = x + 0.5 * acc_ref[...]
    if final:
        y = _rms(y, fg_ref[...], D_MODEL)
    o_ref[...] = y


def _ffn_call(l, x, gain, wg, wu, wd, mix=None, final_gain=None):
    tok = lambda w: pl.BlockSpec((TM, w), lambda i: (i, 0))
    args = [x]
    specs = [tok(D_MODEL)]
    if mix is not None:
        oa, ob, oc, wo = mix
        args += [oa, ob, oc, wo]
        specs += [tok(WO_B), tok(WO_C - WO_B), tok(WO_ROWS - WO_C),
                  _layer_spec((WO_ROWS, D_MODEL), l)]
    args += [gain, wg, wu, wd]
    specs += [_layer_spec((1, D_MODEL), l), _layer_spec((D_MODEL, D_FF), l),
              _layer_spec((D_MODEL, D_FF), l), _layer_spec((D_FF, D_MODEL), l)]
    if final_gain is not None:
        args.append(final_gain)
        specs.append(_const_spec((1, D_MODEL)))
    kern = functools.partial(_ffn_kernel, mix=mix is not None, final=final_gain is not None)
    return pl.pallas_call(
        kern,
        grid=(N_TOK // TM,),
        in_specs=specs,
        out_specs=tok(D_MODEL),
        out_shape=jax.ShapeDtypeStruct((N_TOK, D_MODEL), F32),
        scratch_shapes=[pltpu.VMEM((TM, D_MODEL), F32)],
        compiler_params=pltpu.CompilerParams(
            dimension_semantics=("arbitrary",), vmem_limit_bytes=VMEM_LIMIT),
        name="ffn",
    )(*args)


ROPE_HALF = (16, 8, 4)


def _rope_fn(table):
    lane = lax.broadcasted_iota(jnp.int32, (table.shape[0], LANE), 1)
    between = lambda a, b: (lane >= a) & (lane < b)
    rot = (between(64, 96), lane < 16, between(16, 24) | between(48, 56))
    second = (between(80, 96), between(8, 16), between(20, 24) | between(52, 56))
    cache = {}

    def rope(v, typ):
        if typ not in cache:
            cache[typ] = (jnp.where(rot[typ], table[:, 0:LANE], 1.0),
                          jnp.where(rot[typ], table[:, LANE:2 * LANE], 0.0))
        cos, sin = cache[typ]
        r = ROPE_HALF[typ]
        partner = jnp.where(second[typ], pltpu.roll(v, r, 1), pltpu.roll(v, LANE - r, 1))
        return v * cos + partner * sin

    return rope


def _rope_q(q_ref, rope_ref, typ):
    rope = _rope_fn(rope_ref[...])
    nb = q_ref.shape[1] // LANE
    return [rope(q_ref[:, b * LANE:(b + 1) * LANE].astype(F32), typ).astype(BF16)
            for b in range(nb)]


def _proj_kernel(x_ref, g_ref, w_ref, rope_ref, qn_ref, kvn_ref, wuq_ref, wukv_ref,
                 qa_ref, ka_ref, va_ref, qb_ref, kvc_ref, ks_ref, vs_ref, kw_ref,
                 gt_ref, qc_ref, kd_ref):
    xn = _rms(x_ref[...], g_ref[...], D_MODEL).astype(BF16)
    lane = lax.broadcasted_iota(jnp.int32, (TM, LANE), 1)
    lo = lane < 64
    rope = _rope_fn(rope_ref[...])
    ones_row = (lane == VROW).astype(F32)

    def blk(v, b):
        return v[:, b * LANE:(b + 1) * LANE]

    ha = _dot(xn, w_ref[:, CE_A:CE_A + 384])
    a1 = blk(ha, 1)
    cq = jnp.concatenate([blk(ha, 0), jnp.where(lo, a1, 0.0)], axis=1)
    cqn = _rms(cq, qn_ref[...], A_Q_RANK).astype(BF16)
    ckvn = _rms(blk(ha, 2), kvn_ref[...], A_KV_RANK).astype(BF16)
    kpe = jnp.where((lane >= 64) & (lane < 96), rope(a1, 0), 0.0)
    gt_ref[0] = jax.nn.sigmoid(a1).T[96:128, :]
    q = _dot(cqn, wuq_ref[...])
    kv = _dot(ckvn, wukv_ref[...])
    qa_ref[...] = (q * ((A_NOPE + A_ROPE) ** -0.5 * LOG2E)).astype(BF16)
    for h in range(A_HEADS):
        ka_ref[:, h * LANE:(h + 1) * LANE] = (blk(kv, h) + kpe).astype(BF16)
        va_ref[:, h * LANE:(h + 1) * LANE] = (blk(kv, A_HEADS + h) + ones_row).astype(BF16)

    qb_ref[...] = (_dot(xn, w_ref[:, CE_BQ:CE_BQ + 768]) * (B_DH ** -0.5 * LOG2E)).astype(BF16)
    hk = _dot(xn, w_ref[:, CE_KVC:CE_KVC + 256])
    for g in range(B_KV_GROUPS):
        kvc_ref[g] = rope(blk(hk, g), 1)
    tpos = (pl.program_id(0) % (SEQ // TM)) * TM + lax.broadcasted_iota(jnp.int32, (TM, LANE), 0)
    onehot = ((lane - 64) == (tpos // SLC_LEN)).astype(F32)
    hk = _dot(xn, w_ref[:, CE_KSVS:CE_KSVS + 256])
    vs_ref[...] = hk.astype(BF16)
    for g in range(B_KV_GROUPS):
        ks_ref[:, g * LANE:(g + 1) * LANE] = jnp.where(lo, rope(blk(hk, g), 1), onehot).astype(BF16)
    hk = _dot(xn, w_ref[:, CE_KWVW:CE_KWVW + 256])
    for g in range(B_KV_GROUPS):
        kw_ref[:, g * LANE:(g + 1) * LANE] = rope(blk(hk, g), 1).astype(BF16)

    qc_ref[...] = (_dot(xn, w_ref[:, CE_CQ:CE_CQ + 512]) * (C_DH ** -0.5 * LOG2E)).astype(BF16)
    hc = _dot(xn, w_ref[:, CE_CKV:CE_CKV + 512])
    for h in range(C_HEADS):
        kd_ref[:, h * LANE:(h + 1) * LANE] = rope(blk(hc, h), 2).astype(BF16)


def _proj_call(l, x, gain, w_e, rope_tab, qn, kvn, wuq, wukv):
    tok = lambda w: pl.BlockSpec((TM, w), lambda i: (i, 0))
    bf = lambda w: jax.ShapeDtypeStruct((N_TOK, w), BF16)
    return pl.pallas_call(
        _proj_kernel,
        grid=(N_TOK // TM,),
        in_specs=[tok(D_MODEL), _layer_spec((1, D_MODEL), l), _layer_spec((D_MODEL, CE), l),
                  tok(2 * LANE), _layer_spec((1, 256), l), _layer_spec((1, LANE), l),
                  _layer_spec((256, 768), l), _layer_spec((LANE, 1536), l)],
        out_specs=[tok(768), tok(768), tok(768), tok(768),
                   pl.BlockSpec((B_KV_GROUPS, TM, LANE), lambda i: (0, i, 0)),
                   tok(256), tok(256), tok(256),
                   pl.BlockSpec((1, B_KV_GROUPS * GATE_ROWS, TM), lambda i: (i, 0, 0)),
                   tok(512), tok(512)],
        out_shape=[bf(768), bf(768), bf(768), bf(768),
                   jax.ShapeDtypeStruct((B_KV_GROUPS, N_TOK, LANE), F32),
                   bf(256), bf(256), bf(256),
                   jax.ShapeDtypeStruct((NCH, B_KV_GROUPS * GATE_ROWS, TM), F32),
                   bf(512), bf(512)],
        compiler_params=pltpu.CompilerParams(
            dimension_semantics=("arbitrary",), vmem_limit_bytes=VMEM_LIMIT),
        name="proj",
    )(x, gain, w_e, rope_tab, qn, kvn, wuq, wukv)


def _tile_pos(r):
    kk = lax.broadcasted_iota(jnp.int32, (TK, 1), 0)
    tt = lax.broadcasted_iota(jnp.int32, (1, r), 1) & (TQ - 1)
    return kk, tt


def _dot_nt(a, b):
    return lax.dot_general(a, b, (((1,), (1,)), ((), ())), preferred_element_type=F32)


def _score_stage(q, k, mask, m_prev, s_ref, m_ref):
    s = _dot_nt(k, q)
    if mask is not None:
        s = jnp.where(mask, s, MASKED)
    cm = jnp.max(s, axis=0, keepdims=True)
    m_ref[...] = cm if m_prev is None else jnp.maximum(m_prev[...], cm)
    s_ref[...] = s


def _value_stage(vt, s_ref, m_ref, mb_ref, acc_ref):
    m = m_ref[...]
    alpha = jnp.exp2(mb_ref[...] - m)
    p = jnp.exp2(s_ref[...] - m).astype(BF16)
    mb_ref[...] = m
    acc_ref[...] = alpha * acc_ref[...] + _dot(vt, p)


def _causal_attention(streams, i):
    for qt, k_of, vt_of, (s0, s1, m0, m1, mb, acc) in streams:
        kk, tt = _tile_pos(qt.shape[0])
        _score_stage(qt, k_of(i), kk <= tt, None, s0, m0)
        mb[...] = m0[...]
        acc[...] = jnp.zeros_like(acc)

    def step(score_j, pend_j, a, b):
        for qt, k_of, vt_of, sc in streams:
            bufs, maxs, mb, acc = (sc[0], sc[1]), (sc[2], sc[3]), sc[4], sc[5]
            _score_stage(qt, k_of(score_j), None, maxs[a], bufs[b], maxs[b])
            _value_stage(vt_of(pend_j), bufs[a], maxs[a], mb, acc)

    def fold(pend_j, a):
        for qt, k_of, vt_of, sc in streams:
            _value_stage(vt_of(pend_j), sc[a], sc[2 + a], sc[4], sc[5])

    def body(jj, carry):
        step(2 * jj, jnp.where(jj == 0, i, 2 * jj - 1), 0, 1)
        step(2 * jj + 1, 2 * jj, 1, 0)
        return carry

    npairs = i // 2
    lax.fori_loop(0, npairs, body, 0)
    pend = jnp.where(npairs == 0, i, 2 * npairs - 1)

    @pl.when(i % 2 == 1)
    def _():
        step(i - 1, pend, 0, 1)
        fold(i - 1, 1)

    @pl.when(i % 2 == 0)
    def _():
        fold(pend, 0)

    return [st[3][5][...] for st in streams]


def _normalise(acc):
    return acc[0:VROW] / acc[VROW:VROW + 1, :]


def _pipe_scratch(r):
    return ([pltpu.VMEM((TK, r), F32)] * 2 + [pltpu.VMEM((1, r), F32)] * 3
            + [pltpu.VMEM((VT_ROWS, r), F32)])


N_PIPE = 6


def _k_chunk(k_ref, j, sl=slice(None)):
    return k_ref[pl.ds(pl.multiple_of(j * TK, TK), TK), sl]


def _q_spec(nb):
    return pl.BlockSpec((None, TQ, nb * LANE), lambda b, h, i: (b, i, h))


def _kv_spec(nb):
    return pl.BlockSpec((None, SEQ, nb * LANE), lambda b, h, i: (b, 0, h))


_ROPE_SPEC = pl.BlockSpec((None, TQ, 2 * LANE), lambda b, h, i: (b, i, 0))


def _vt_lower(chunk):
    return chunk.T[0:VT_ROWS]


def _vt_upper(chunk):
    ones = (lax.broadcasted_iota(jnp.int32, (VT_ROWS - VROW, TK), 0) == 0).astype(BF16)
    return jnp.concatenate([chunk.T[VROW:2 * VROW], ones], axis=0)


def _attn_params():
    return pltpu.CompilerParams(dimension_semantics=("arbitrary",) * 3,
                                vmem_limit_bytes=VMEM_LIMIT)


A_HPS = 2


def _mla_kernel(q_ref, rope_ref, k_ref, v_ref, o_ref, *scratch):
    i = pl.program_id(2)
    qs = _rope_q(q_ref, rope_ref, 0)
    streams = []
    for h in range(A_HPS):
        sl = slice(h * LANE, (h + 1) * LANE)
        streams.append((qs[h],
                        lambda j, sl=sl: _k_chunk(k_ref, j, sl),
                        lambda j, sl=sl: _vt_lower(_k_chunk(v_ref, j, sl)),
                        scratch[N_PIPE * h:N_PIPE * (h + 1)]))
    accs = _causal_attention(streams, i)
    o_ref[...] = jnp.concatenate([_normalise(a) for a in accs], axis=0).T.astype(BF16)


def _mla_call(qa, rope_tab, ka, va):
    return pl.pallas_call(
        _mla_kernel,
        grid=(BATCH, A_HEADS // A_HPS, NQ),
        in_specs=[_q_spec(A_HPS), _ROPE_SPEC, _kv_spec(A_HPS), _kv_spec(A_HPS)],
        out_specs=pl.BlockSpec((None, TQ, LANE), lambda b, h, i: (b, i, h)),
        out_shape=jax.ShapeDtypeStruct((BATCH, SEQ, A_HEADS * A_V), BF16),
        scratch_shapes=_pipe_scratch(TQ) * A_HPS,
        compiler_params=_attn_params(),
        name="mla_attn",
    )(qa, rope_tab, ka, va)


C_HPS = 2


def _diff_kernel(q_ref, rope_ref, kv_ref, lqk_ref, sn_ref, o_ref, *scratch, lam_init):
    i = pl.program_id(2)
    lane = lax.broadcasted_iota(jnp.int32, (TQ, LANE), 1)
    qs = _rope_q(q_ref, rope_ref, 2)
    streams = []
    for h in range(C_HPS):
        sl = slice(h * LANE, (h + 1) * LANE)
        q = qs[h]
        zero = jnp.zeros_like(q)
        q2 = jnp.concatenate([jnp.where(lane < C_DH, q, zero),
                              jnp.where((lane >= C_DH) & (lane < 2 * C_DH), q, zero)], axis=0)
        streams.append((q2,
                        lambda j, sl=sl: _k_chunk(kv_ref, j, sl),
                        lambda j, sl=sl: _vt_upper(_k_chunk(kv_ref, j, sl)),
                        scratch[N_PIPE * h:N_PIPE * (h + 1)]))
    accs = _causal_attention(streams, i)
    lqk = lqk_ref[...]
    lam = (jnp.exp(jnp.sum(lqk[0:1] * lqk[1:2], axis=-1, keepdims=True))
           - jnp.exp(jnp.sum(lqk[2:3] * lqk[3:4], axis=-1, keepdims=True)) + lam_init)
    diffs = []
    for a in accs:
        o = _normalise(a)
        diffs.append(o[:, 0:TQ] - lam * o[:, TQ:2 * TQ])
    d = jnp.concatenate(diffs, axis=0).T
    d2 = d * d
    first = lane < 2 * C_DH
    ms0 = jnp.sum(jnp.where(first, d2, 0.0), axis=-1, keepdims=True)
    ms1 = jnp.sum(jnp.where(first, 0.0, d2), axis=-1, keepdims=True)
    ms = jnp.where(first, ms0, ms1) * (1.0 / (2 * C_DH))
    y = d * lax.rsqrt(ms + EPS) * sn_ref[...] * (1.0 - lam_init)
    o_ref[...] = y.astype(BF16)


def _diff_call(l, qc, rope_tab, kd, lqk, sn):
    lam_init = 0.8 - 0.6 * math.exp(-0.3 * l)
    return pl.pallas_call(
        functools.partial(_diff_kernel, lam_init=lam_init),
        grid=(BATCH, C_HEADS // C_HPS, NQ),
        in_specs=[_q_spec(C_HPS), _ROPE_SPEC, _kv_spec(C_HPS),
                  _layer_spec((8, LANE), l), _layer_spec((1, LANE), l)],
        out_specs=pl.BlockSpec((None, TQ, LANE), lambda b, h, i: (b, i, h)),
        out_shape=jax.ShapeDtypeStruct((BATCH, SEQ, C_HEADS * 2 * C_DH), BF16),
        scratch_shapes=_pipe_scratch(2 * TQ) * C_HPS,
        compiler_params=_attn_params(),
        name="diff_attn",
    )(qc, rope_tab, kd, lqk, sn)


def _nsa_kernel(q_ref, rope_ref, x16_ref, ks_ref, vs_ref, kw_ref, gt_ref,
                petop_ref, pebot_ref, w1top_ref, w1bot_ref, w2k_ref, w2v_ref, ovt_ref,
                o_ref, kc_s, vct_s, *slc_scratch):
    i = pl.program_id(2)
    r3 = B_HPG * TQ

    @pl.when(i == 0)
    def _():
        x = x16_ref[...]
        yt = _dot((x + petop_ref[...]).astype(BF16), w1top_ref[...])
        yb = _dot((x + pebot_ref[...]).astype(BF16), w1bot_ref[...])
        pre = yt + pltpu.roll(yb, N_CMP_PAD - 1, 0)
        act = (pre * jax.nn.sigmoid(pre)).astype(BF16)
        kc_s[...] = _dot(act, w2k_ref[...]).astype(BF16)
        vct_s[...] = _dot(act, w2v_ref[...]).T[0:VROW].astype(BF16)

    q3 = jnp.concatenate(_rope_q(q_ref, rope_ref, 1), axis=0)

    kk, tt = _tile_pos(r3)
    c0 = jnp.maximum(i - 1, 0)
    kv_diag = _k_chunk(kw_ref, i)
    kv_prev = _k_chunk(kw_ref, c0)
    s_diag = jnp.where(kk <= tt, _dot_nt(kv_diag, q3), MASKED)
    s_prev = jnp.where(tt + jnp.where(i >= 1, 0, TQ) < kk, _dot_nt(kv_prev, q3), MASKED)
    m_win = jnp.maximum(jnp.max(s_diag, axis=0, keepdims=True),
                        jnp.max(s_prev, axis=0, keepdims=True))
    o_win = _normalise(_dot(_vt_upper(kv_diag), jnp.exp2(s_diag - m_win).astype(BF16))
                       + _dot(_vt_upper(kv_prev), jnp.exp2(s_prev - m_win).astype(BF16)))

    s = _dot_nt(kc_s[...], q3)
    c_end = lax.broadcasted_iota(jnp.int32, (N_CMP_PAD, 1), 0) * CMP_STRIDE + (CMP_LEN - 1)
    valid = c_end <= i * TQ + tt
    smax = jnp.max(jnp.where(valid, s, MASKED), axis=0, keepdims=True)
    e = jnp.where(valid, jnp.exp2(s - smax), 0.0)
    den = jnp.sum(e, axis=0, keepdims=True)
    p = e / jnp.where(den > 0.0, den, 1.0)
    o_cmp = _dot(vct_s[...], p.astype(BF16))

    psum = p[:, 0:TQ] + p[:, TQ:2 * TQ] + p[:, 2 * TQ:3 * TQ]
    p_hi = psum.astype(BF16)
    p_lo = (psum - p_hi.astype(F32)).astype(BF16)
    ovt = ovt_ref[...]
    imp = (_dot(ovt, p_hi) + _dot(ovt, p_lo))[64:128]
    m_idx = lax.broadcasted_iota(jnp.int32, (N_SLC, TQ), 0)
    cur = (i * TQ + lax.broadcasted_iota(jnp.int32, (N_SLC, TQ), 1)) // SLC_LEN
    forced = (m_idx == 0) | (m_idx == cur) | (m_idx == cur - 1)
    score = jnp.where(forced, 1e9, jnp.where(m_idx <= cur, imp, -1.0))
    sub = lax.broadcasted_iota(jnp.int32, (SUBLANES, TQ), 0)
    bias = [jnp.zeros((64, TQ), F32)]
    for v in range(N_SLC // SUBLANES):
        tile = score[v * SUBLANES:(v + 1) * SUBLANES, :]
        rank = jnp.zeros((SUBLANES, TQ), jnp.int32)
        for mp in range(N_SLC):
            srow = score[mp:mp + 1, :]
            if mp < v * SUBLANES:
                ahead = srow >= tile
            elif mp >= (v + 1) * SUBLANES:
                ahead = srow > tile
            else:
                ahead = (srow > tile) | ((srow == tile) & (sub > mp - v * SUBLANES))
            rank = rank + jnp.where(ahead, 1, 0)
        bias.append(jnp.where(rank < SLC_TOPN, 0.0, -SEL_BIG))
    bias = jnp.concatenate(bias, axis=0).T.astype(BF16)
    q_sel = q3 + jnp.concatenate([bias] * B_HPG, axis=0)

    stream = (q_sel, lambda j: _k_chunk(ks_ref, j), lambda j: _vt_upper(_k_chunk(vs_ref, j)),
              slc_scratch)
    o_slc = _normalise(_causal_attention([stream], i)[0])

    gate = gt_ref[...]
    outs = []
    for h in range(B_HPG):
        sl = slice(h * TQ, (h + 1) * TQ)
        outs.append(gate[3 * h:3 * h + 1] * o_cmp[:, sl]
                    + gate[3 * h + 1:3 * h + 2] * o_slc[:, sl]
                    + gate[3 * h + 2:3 * h + 3] * o_win[:, sl])
    o_ref[:, 0:LANE] = jnp.concatenate(outs[0:2], axis=0).T.astype(BF16)
    o_ref[:, LANE:2 * LANE] = jnp.concatenate(
        [outs[2], jnp.zeros((VROW, TQ), F32)], axis=0).T.astype(BF16)


def _nsa_call(l, qb, rope_tab, x16, ks, vs, kw, gt, petop, pebot, w1top, w1bot, w2k, w2v, ovt):
    r3 = B_HPG * TQ
    return pl.pallas_call(
        _nsa_kernel,
        grid=(BATCH, B_KV_GROUPS, NQ),
        in_specs=[_q_spec(B_HPG), _ROPE_SPEC,
                  pl.BlockSpec((None, None, N_CMP_PAD, 16 * LANE), lambda b, g, i: (g, b, 0, 0)),
                  _kv_spec(1), _kv_spec(1), _kv_spec(1),
                  pl.BlockSpec((None, GATE_ROWS, TQ), lambda b, g, i: (b * NQ + i, g, 0)),
                  _layer_spec((1, 16 * LANE), l), _layer_spec((1, 16 * LANE), l),
                  _layer_spec((16 * LANE, LANE), l), _layer_spec((16 * LANE, LANE), l),
                  _layer_spec((LANE, LANE), l), _layer_spec((LANE, LANE), l),
                  _const_spec((LANE, N_CMP_PAD))],
        out_specs=pl.BlockSpec((None, TQ, 2 * LANE), lambda b, g, i: (b, i, g)),
        out_shape=jax.ShapeDtypeStruct((BATCH, SEQ, B_KV_GROUPS * 2 * LANE), BF16),
        scratch_shapes=[pltpu.VMEM((N_CMP_PAD, LANE), BF16), pltpu.VMEM((VROW, N_CMP_PAD), BF16)]
        + _pipe_scratch(r3),
        compiler_params=_attn_params(),
        name="nsa_attn",
    )(qb, rope_tab, x16, ks, vs, kw, gt, petop, pebot, w1top, w1bot, w2k, w2v, ovt)


def _expand_w_in(w):
    a = w[..., :A_COLS]
    b = w[..., A_COLS:A_COLS + B_COLS]
    c = w[..., A_COLS + B_COLS:]
    z = lambda n: jnp.zeros(w.shape[:-1] + (n,), w.dtype)
    gate = lambda g: [b[..., 1152 + g * 9:1161 + g * 9], z(GATE_ROWS - 9)]
    parts = [a[..., 0:192], a[..., 320:352]] + gate(0) + gate(1) + [a[..., 192:320]]
    for h in range(B_HEADS):
        parts += [b[..., h * 64:(h + 1) * 64], z(64)]
    for base in (384, 640, 896):
        for g in range(B_KV_GROUPS):
            parts += [b[..., base + g * 64:base + g * 64 + 64],
                      b[..., base + 128 + g * 64:base + 192 + g * 64]]

    def rot_mid(t, o):
        return [t[..., o + 8:o + 24], t[..., o:o + 8], t[..., o + 24:o + 32]]

    for h in range(C_HEADS):
        parts += rot_mid(c, h * 64) + rot_mid(c, h * 64 + 32) + [z(64)]
    for h in range(C_HEADS):
        parts += (rot_mid(c, 256 + h * 64) + rot_mid(c, 288 + h * 64)
                  + [c[..., 512 + h * 64:576 + h * 64]])
    return jnp.concatenate(parts, axis=-1).astype(BF16)


def _expand_w_out(w):
    z = jnp.zeros((DEPTH, 64, D_MODEL), w.dtype)
    gw = B_HPG * B_DH
    return jnp.concatenate([w[:, 0:WO_B], w[:, WO_B:WO_B + gw], z, w[:, WO_B + gw:WO_B + 2 * gw], z,
                            w[:, WO_B + 2 * gw:]], axis=1).astype(BF16)


def _rope_table(positions):
    def inv(rot):
        return 1.0 / (ROPE_THETA ** (jnp.arange(0, rot, 2, dtype=F32) / rot))

    fa, fb, fc = inv(A_ROPE), inv(B_ROT), inv(C_ROT)
    z = lambda n: jnp.zeros((n,), F32)
    freq = jnp.concatenate([fb, fb, fc, fc, z(24), fc, fc, z(8), fa, fa, z(32)])
    sign = jnp.concatenate([-jnp.ones(8), jnp.ones(8), -jnp.ones(4), jnp.ones(4), z(24),
                            -jnp.ones(4), jnp.ones(4), z(8), -jnp.ones(16), jnp.ones(16),
                            z(32)]).astype(F32)
    ang = positions.astype(F32).reshape(N_TOK, 1) * freq[None, :]
    return jnp.concatenate([jnp.cos(ang), sign[None, :] * jnp.sin(ang)], axis=1)


def _overlap_t():
    c = jnp.arange(N_CMP_PAD)[None, :]
    m = jnp.arange(N_SLC)[:, None]
    ov = ((c * CMP_STRIDE < m * SLC_LEN + SLC_LEN)
          & (c * CMP_STRIDE + CMP_LEN - 1 >= m * SLC_LEN) & (c < N_CMP))
    return jnp.concatenate([jnp.zeros((64, N_CMP_PAD), BF16), ov.astype(BF16)], axis=0)


def kernel(x, positions, ffn1_norm, ffn1_wg, ffn1_wu, ffn1_wd, mix_norm, w_in, mla_q_norm, mla_kv_norm, mla_w_uq, mla_w_ukv, nsa_pe_k, nsa_pe_v, nsa_phi_k1, nsa_phi_k2, nsa_phi_v1, nsa_phi_v2, diff_lq1, diff_lk1, diff_lq2, diff_lk2, diff_sub_norm, w_out, ffn2_norm, ffn2_wg, ffn2_wu, ffn2_wd, final_norm):
    xf = x.reshape(N_TOK, D_MODEL)
    rope_tab = _rope_table(positions)
    ovt = _overlap_t()
    bf = lambda t: t.astype(BF16)
    row = lambda t: t.reshape(DEPTH, 1, t.shape[-1])
    pad_to = lambda t, axis, n: jnp.pad(
        t, [(0, n - t.shape[a]) if a == axis else (0, 0) for a in range(t.ndim)])

    ffn1 = (row(ffn1_norm), bf(ffn1_wg), bf(ffn1_wu), bf(ffn1_wd))
    ffn2 = (row(ffn2_norm), bf(ffn2_wg), bf(ffn2_wu), bf(ffn2_wd))
    w_e = _expand_w_in(w_in)
    w_o = _expand_w_out(w_out)
    wuq = pad_to(pad_to(mla_w_uq.reshape(DEPTH, A_Q_RANK, A_HEADS, A_NOPE + A_ROPE), 3, LANE), 1, 256)
    wuq = bf(wuq.reshape(DEPTH, 256, A_HEADS * LANE))
    wkv = mla_w_ukv.reshape(DEPTH, A_KV_RANK, A_HEADS, A_NOPE + A_V)
    wukv = bf(jnp.concatenate(
        [pad_to(wkv[..., :A_NOPE], 3, LANE).reshape(DEPTH, A_KV_RANK, A_HEADS * LANE),
         pad_to(wkv[..., A_NOPE:], 3, LANE).reshape(DEPTH, A_KV_RANK, A_HEADS * LANE)], axis=2))
    qn = pad_to(row(mla_q_norm), 2, 256)
    kvn = row(mla_kv_norm)

    k1 = nsa_phi_k1.reshape(DEPTH, CMP_LEN, B_DH, B_DH)
    v1 = nsa_phi_v1.reshape(DEPTH, CMP_LEN, B_DH, B_DH)
    w1 = bf(jnp.concatenate([pad_to(k1, 3, LANE), jnp.pad(v1, ((0, 0), (0, 0), (0, 0), (64, 0)))],
                            axis=2))
    w1top = w1[:, :16].reshape(DEPTH, 16 * LANE, LANE)
    w1bot = w1[:, 16:].reshape(DEPTH, 16 * LANE, LANE)
    pe = jnp.concatenate([nsa_pe_k, nsa_pe_v], axis=2)
    petop = pe[:, :16].reshape(DEPTH, 1, 16 * LANE)
    pebot = pe[:, 16:].reshape(DEPTH, 1, 16 * LANE)
    w2k = bf(pad_to(pad_to(nsa_phi_k2, 1, LANE), 2, LANE))
    w2v = bf(jnp.pad(nsa_phi_v2, ((0, 0), (64, 0), (0, 64))))
    lqk = pad_to(pad_to(jnp.stack([diff_lq1, diff_lk1, diff_lq2, diff_lk2], axis=1), 1, 8), 2, LANE)
    sn = jnp.tile(row(diff_sub_norm), (1, 1, C_HPS))

    b3 = lambda t: t.reshape(BATCH, SEQ, t.shape[-1])
    flat = lambda t: t.reshape(N_TOK, t.shape[-1])
    rope3 = b3(rope_tab)
    for l in range(DEPTH):
        xf = _ffn_call(l, xf, *ffn1)
        (qa, ka, va, qb, kvc, ks, vs, kw, gt, qc, kd) = _proj_call(
            l, xf, row(mix_norm), w_e, rope_tab, qn, kvn, wuq, wukv)
        oa = _mla_call(b3(qa), rope3, b3(ka), b3(va))
        x16 = kvc.reshape(B_KV_GROUPS, BATCH, N_CMP_PAD, 16 * LANE)
        ob = _nsa_call(l, b3(qb), rope3, x16, b3(ks), b3(vs), b3(kw), gt,
                       petop, pebot, w1top, w1bot, w2k, w2v, ovt)
        oc = _diff_call(l, b3(qc), rope3, b3(kd), lqk, sn)
        fg = final_norm.reshape(1, D_MODEL) if l == DEPTH - 1 else None
        xf = _ffn_call(l, xf, *ffn2, mix=(flat(oa), flat(ob), flat(oc), w_o), final_gain=fg)
    return xf.reshape(BATCH, SEQ, D_MODEL)
```

```python
import functools
import math

import jax
import jax.numpy as jnp
from jax import lax
from jax.experimental import pallas as pl
from jax.experimental.pallas import tpu as pltpu

F32 = jnp.float32
BF16 = jnp.bfloat16

D_MODEL = 1024
BATCH = 4
SEQ = 4096
DEPTH = 2
N_TOK = BATCH * SEQ
EPS = 1e-6
ROPE_THETA = 500000.0
D_FF = 2816

A_HEADS = 6
A_Q_RANK = 192
A_KV_RANK = 128
A_NOPE = 64
A_ROPE = 32
A_V = 64

B_HEADS = 6
B_KV_GROUPS = 2
B_HPG = 3
B_DH = 64
B_ROT = 16
CMP_LEN = 32
CMP_STRIDE = 16
N_CMP = (SEQ - CMP_LEN) // CMP_STRIDE + 1
N_CMP_PAD = SEQ // CMP_STRIDE
SLC_LEN = 64
N_SLC = SEQ // SLC_LEN
SLC_TOPN = 16
WINDOW = 512

C_HEADS = 4
C_DH = 32
C_ROT = 8

A_COLS = A_Q_RANK + A_KV_RANK + A_ROPE
B_COLS = B_HEADS * B_DH + 6 * B_KV_GROUPS * B_DH + 3 * B_HEADS
C_COLS = 3 * C_HEADS * 2 * C_DH

LANE = 128
SUBLANES = 8
VROW = 64
VT_ROWS = 72
MASKED = -1e30
SEL_BIG = 1e9
LOG2E = 1.0 / math.log(2.0)

TM = 512
TF = 256
NFC = D_FF // TF
TQ = 512
TK = 512
NQ = SEQ // TQ
NCH = N_TOK // TK
GATE_ROWS = 16

CE_A = 0
CE_BQ = 384
CE_KVC = 1152
CE_KSVS = 1408
CE_KWVW = 1664
CE_CQ = 1920
CE_CKV = 2432
CE = 2944

WO_B = A_HEADS * A_V
WO_C = WO_B + B_KV_GROUPS * 2 * LANE
WO_ROWS = WO_C + C_HEADS * 2 * C_DH

VMEM_LIMIT = 56 * 1024 * 1024


def _const_spec(shape):
    nd = len(shape)
    return pl.BlockSpec(shape, lambda *_: (0,) * nd, pipeline_mode=pl.Buffered(1))


def _layer_spec(shape, l):
    nd = len(shape)
    return pl.BlockSpec((None,) + tuple(shape), lambda *_: (l,) + (0,) * nd,
                        pipeline_mode=pl.Buffered(1))


def _rms(x, g, n):
    ms = jnp.sum(x * x, axis=-1, keepdims=True) * (1.0 / n)
    return x * lax.rsqrt(ms + EPS) * g


def _dot(a, b):
    return jnp.dot(a, b, preferred_element_type=F32)


def _ffn_kernel(*refs, mix, final):
    it = iter(refs)
    x_ref = next(it)
    if mix:
        oa_ref, ob_ref, oc_ref, wo_ref = next(it), next(it), next(it), next(it)
    g_ref, wg_ref, wu_ref, wd_ref = next(it), next(it), next(it), next(it)
    if final:
        fg_ref = next(it)
    o_ref = next(it)
    acc_ref = next(it)

    x = x_ref[...]
    if mix:
        x = x + _dot(oa_ref[...], wo_ref[0:WO_B, :])
        x = x + _dot(ob_ref[...], wo_ref[WO_B:WO_C, :])
        x = x + _dot(oc_ref[...], wo_ref[WO_C:WO_ROWS, :])
    xn = _rms(x, g_ref[...], D_MODEL).astype(BF16)
    for c in range(NFC):
        g = _dot(xn, wg_ref[:, c * TF:(c + 1) * TF])
        u = _dot(xn, wu_ref[:, c * TF:(c + 1) * TF])
        h = (g * jax.nn.sigmoid(g) * u).astype(BF16)
        d = _dot(h, wd_ref[c * TF:(c + 1) * TF, :])
        if c == 0:
            acc_ref[...] = d
        else:
            acc_ref[...] += d
    y = x + 0.5 * acc_ref[...]
    if final:
        y = _rms(y, fg_ref[...], D_MODEL)
    o_ref[...] = y


def _ffn_call(l, x, gain, wg, wu, wd, mix=None, final_gain=None):
    tok = lambda w: pl.BlockSpec((TM, w), lambda i: (i, 0))
    args = [x]
    specs = [tok(D_MODEL)]
    if mix is not None:
        oa, ob, oc, wo = mix
        args += [oa, ob, oc, wo]
        specs += [tok(WO_B), tok(WO_C - WO_B), tok(WO_ROWS - WO_C),
                  _layer_spec((WO_ROWS, D_MODEL), l)]
    args += [gain, wg, wu, wd]
    specs += [_layer_spec((1, D_MODEL), l), _layer_spec((D_MODEL, D_FF), l),
              _layer_spec((D_MODEL, D_FF), l), _layer_spec((D_FF, D_MODEL), l)]
    if final_gain is not None:
        args.append(final_gain)
        specs.append(_const_spec((1, D_MODEL)))
    kern = functools.partial(_ffn_kernel, mix=mix is not None, final=final_gain is not None)
    return pl.pallas_call(
        kern,
        grid=(N_TOK // TM,),
        in_specs=specs,
        out_specs=tok(D_MODEL),
        out_shape=jax.ShapeDtypeStruct((N_TOK, D_MODEL), F32),
        scratch_shapes=[pltpu.VMEM((TM, D_MODEL), F32)],
        compiler_params=pltpu.CompilerParams(
            dimension_semantics=("arbitrary",), vmem_limit_bytes=VMEM_LIMIT),
        name="ffn",
    )(*args)


ROPE_HALF = (16, 8, 4)


def _rope_fn(table):
    lane = lax.broadcasted_iota(jnp.int32, (table.shape[0], LANE), 1)
    between = lambda a, b: (lane >= a) & (lane < b)
    rot = (between(64, 96), lane < 16, between(16, 24) | between(48, 56))
    second = (between(80, 96), between(8, 16), between(20, 24) | between(52, 56))
    cache = {}

    def rope(v, typ):
        if typ not in cache:
            cache[typ] = (jnp.where(rot[typ], table[:, 0:LANE], 1.0),
                          jnp.where(rot[typ], table[:, LANE:2 * LANE], 0.0))
        cos, sin = cache[typ]
        r = ROPE_HALF[typ]
        partner = jnp.where(second[typ], pltpu.roll(v, r, 1), pltpu.roll(v, LANE - r, 1))
        return v * cos + partner * sin

    return rope


def _rope_q(q_ref, rope_ref, typ):
    rope = _rope_fn(rope_ref[...])
    nb = q_ref.shape[1] // LANE
    return [rope(q_ref[:, b * LANE:(b + 1) * LANE].astype(F32), typ).astype(BF16)
            for b in range(nb)]


def _proj_kernel(x_ref, g_ref, w_ref, rope_ref, qn_ref, kvn_ref, wuq_ref, wukv_ref,
                 qa_ref, ka_ref, va_ref, qb_ref, kvc_ref, ks_ref, vs_ref, kw_ref,
                 gt_ref, qc_ref, kd_ref):
    xn = _rms(x_ref[...], g_ref[...], D_MODEL).astype(BF16)
    lane = lax.broadcasted_iota(jnp.int32, (TM, LANE), 1)
    lo = lane < 64
    rope = _rope_fn(rope_ref[...])
    ones_row = (lane == VROW).astype(F32)

    def blk(v, b):
        return v[:, b * LANE:(b + 1) * LANE]

    ha = _dot(xn, w_ref[:, CE_A:CE_A + 384])
    a1 = blk(ha, 1)
    cq = jnp.concatenate([blk(ha, 0), jnp.where(lo, a1, 0.0)], axis=1)
    cqn = _rms(cq, qn_ref[...], A_Q_RANK).astype(BF16)
    ckvn = _rms(blk(ha, 2), kvn_ref[...], A_KV_RANK).astype(BF16)
    kpe = jnp.where((lane >= 64) & (lane < 96), rope(a1, 0), 0.0)
    gt_ref[0] = jax.nn.sigmoid(a1).T[96:128, :]
    q = _dot(cqn, wuq_ref[...])
    kv = _dot(ckvn, wukv_ref[...])
    qa_ref[...] = (q * ((A_NOPE + A_ROPE) ** -0.5 * LOG2E)).astype(BF16)
    for h in range(A_HEADS):
        ka_ref[:, h * LANE:(h + 1) * LANE] = (blk(kv, h) + kpe).astype(BF16)
        va_ref[:, h * LANE:(h + 1) * LANE] = (blk(kv, A_HEADS + h) + ones_row).astype(BF16)

    qb_ref[...] = (_dot(xn, w_ref[:, CE_BQ:CE_BQ + 768]) * (B_DH ** -0.5 * LOG2E)).astype(BF16)
    hk = _dot(xn, w_ref[:, CE_KVC:CE_KVC + 256])
    for g in range(B_KV_GROUPS):
        kvc_ref[g] = rope(blk(hk, g), 1)
    tpos = (pl.program_id(0) % (SEQ // TM)) * TM + lax.broadcasted_iota(jnp.int32, (TM, LANE), 0)
    onehot = ((lane - 64) == (tpos // SLC_LEN)).astype(F32)
    hk = _dot(xn, w_ref[:, CE_KSVS:CE_KSVS + 256])
    vs_ref[...] = hk.astype(BF16)
    for g in range(B_KV_GROUPS):
        ks_ref[:, g * LANE:(g + 1) * LANE] = jnp.where(lo, rope(blk(hk, g), 1), onehot).astype(BF16)
    hk = _dot(xn, w_ref[:, CE_KWVW:CE_KWVW + 256])
    for g in range(B_KV_GROUPS):
        kw_ref[:, g * LANE:(g + 1) * LANE] = rope(blk(hk, g), 1).astype(BF16)

    qc_ref[...] = (_dot(xn, w_ref[:, CE_CQ:CE_CQ + 512]) * (C_DH ** -0.5 * LOG2E)).astype(BF16)
    hc = _dot(xn, w_ref[:, CE_CKV:CE_CKV + 512])
    for h in range(C_HEADS):
        kd_ref[:, h * LANE:(h + 1) * LANE] = rope(blk(hc, h), 2).astype(BF16)


def _proj_call(l, x, gain, w_e, rope_tab, qn, kvn, wuq, wukv):
    tok = lambda w: pl.BlockSpec((TM, w), lambda i: (i, 0))
    bf = lambda w: jax.ShapeDtypeStruct((N_TOK, w), BF16)
    return pl.pallas_call(
        _proj_kernel,
        grid=(N_TOK // TM,),
        in_specs=[tok(D_MODEL), _layer_spec((1, D_MODEL), l), _layer_spec((D_MODEL, CE), l),
                  pl.BlockSpec((None, TM, 2 * LANE), lambda i: (i // (SEQ // TM), i % (SEQ // TM), 0)),
                  _layer_spec((1, 256), l), _layer_spec((1, LANE), l),
                  _layer_spec((256, 768), l), _layer_spec((LANE, 1536), l)],
        out_specs=[tok(768), tok(768), tok(768), tok(768),
                   pl.BlockSpec((B_KV_GROUPS, TM, LANE), lambda i: (0, i, 0)),
                   tok(256), tok(256), tok(256),
                   pl.BlockSpec((1, B_KV_GROUPS * GATE_ROWS, TM), lambda i: (i, 0, 0)),
                   tok(512), tok(512)],
        out_shape=[bf(768), bf(768), bf(768), bf(768),
                   jax.ShapeDtypeStruct((B_KV_GROUPS, N_TOK, LANE), F32),
                   bf(256), bf(256), bf(256),
                   jax.ShapeDtypeStruct((NCH, B_KV_GROUPS * GATE_ROWS, TM), F32),
                   bf(512), bf(512)],
        compiler_params=pltpu.CompilerParams(
            dimension_semantics=("arbitrary",), vmem_limit_bytes=VMEM_LIMIT),
        name="proj",
    )(x, gain, w_e, rope_tab, qn, kvn, wuq, wukv)


def _tile_pos(r):
    kk = lax.broadcasted_iota(jnp.int32, (TK, 1), 0)
    tt = lax.broadcasted_iota(jnp.int32, (1, r), 1) & (TQ - 1)
    return kk, tt


def _dot_nt(a, b):
    return lax.dot_general(a, b, (((1,), (1,)), ((), ())), preferred_element_type=F32)


def _score_stage(q, k, mask, m_prev, s_ref, m_ref):
    s = _dot_nt(k, q)
    if mask is not None:
        s = jnp.where(mask, s, MASKED)
    cm = jnp.max(s, axis=0, keepdims=True)
    m_ref[...] = cm if m_prev is None else jnp.maximum(m_prev[...], cm)
    s_ref[...] = s


def _value_stage(vt, s_ref, m_ref, mb_ref, acc_ref):
    m = m_ref[...]
    alpha = jnp.exp2(mb_ref[...] - m)
    p = jnp.exp2(s_ref[...] - m).astype(BF16)
    mb_ref[...] = m
    acc_ref[...] = alpha * acc_ref[...] + _dot(vt, p)


def _causal_attention(streams, i):
    for qt, k_of, vt_of, (s0, s1, m0, m1, mb, acc) in streams:
        kk, tt = _tile_pos(qt.shape[0])
        _score_stage(qt, k_of(i), kk <= tt, None, s0, m0)
        mb[...] = m0[...]
        acc[...] = jnp.zeros_like(acc)

    def step(score_j, pend_j, a, b):
        for qt, k_of, vt_of, sc in streams:
            bufs, maxs, mb, acc = (sc[0], sc[1]), (sc[2], sc[3]), sc[4], sc[5]
            _score_stage(qt, k_of(score_j), None, maxs[a], bufs[b], maxs[b])
            _value_stage(vt_of(pend_j), bufs[a], maxs[a], mb, acc)

    def fold(pend_j, a):
        for qt, k_of, vt_of, sc in streams:
            _value_stage(vt_of(pend_j), sc[a], sc[2 + a], sc[4], sc[5])

    def body(jj, carry):
        step(2 * jj, jnp.where(jj == 0, i, 2 * jj - 1), 0, 1)
        step(2 * jj + 1, 2 * jj, 1, 0)
        return carry

    npairs = i // 2
    lax.fori_loop(0, npairs, body, 0)
    pend = jnp.where(npairs == 0, i, 2 * npairs - 1)

    @pl.when(i % 2 == 1)
    def _():
        step(i - 1, pend, 0, 1)
        fold(i - 1, 1)

    @pl.when(i % 2 == 0)
    def _():
        fold(pend, 0)

    return [st[3][5][...] for st in streams]


def _normalise(acc):
    return acc[0:VROW] / acc[VROW:VROW + 1, :]


def _pipe_scratch(r):
    return ([pltpu.VMEM((TK, r), F32)] * 2 + [pltpu.VMEM((1, r), F32)] * 3
            + [pltpu.VMEM((VT_ROWS, r), F32)])


N_PIPE = 6


def _k_chunk(k_ref, j, sl=slice(None)):
    return k_ref[pl.ds(pl.multiple_of(j * TK, TK), TK), sl]


def _q_spec(nb):
    return pl.BlockSpec((None, TQ, nb * LANE), lambda b, h, i: (b, i, h))


def _kv_spec(nb, buffers=None):
    mode = {} if buffers is None else {"pipeline_mode": pl.Buffered(buffers)}
    return pl.BlockSpec((None, SEQ, nb * LANE), lambda b, h, i: (b, 0, h), **mode)


_ROPE_SPEC = pl.BlockSpec((None, TQ, 2 * LANE), lambda b, h, i: (b, i, 0))


def _vt_lower(chunk):
    return chunk.T[0:VT_ROWS]


def _vt_upper(chunk):
    ones = (lax.broadcasted_iota(jnp.int32, (VT_ROWS - VROW, TK), 0) == 0).astype(BF16)
    return jnp.concatenate([chunk.T[VROW:2 * VROW], ones], axis=0)


def _attn_params():
    return pltpu.CompilerParams(dimension_semantics=("arbitrary",) * 3,
                                vmem_limit_bytes=VMEM_LIMIT)


A_HPS = 2


def _mla_kernel(q_ref, rope_ref, k_ref, v_ref, o_ref, *scratch):
    i = pl.program_id(2)
    qs = _rope_q(q_ref, rope_ref, 0)
    streams = []
    for h in range(A_HPS):
        sl = slice(h * LANE, (h + 1) * LANE)
        streams.append((qs[h],
                        lambda j, sl=sl: _k_chunk(k_ref, j, sl),
                        lambda j, sl=sl: _vt_lower(_k_chunk(v_ref, j, sl)),
                        scratch[N_PIPE * h:N_PIPE * (h + 1)]))
    accs = _causal_attention(streams, i)
    o_ref[...] = jnp.concatenate([_normalise(a) for a in accs], axis=0).T.astype(BF16)


def _mla_call(qa, rope_tab, ka, va):
    return pl.pallas_call(
        _mla_kernel,
        grid=(BATCH, A_HEADS // A_HPS, NQ),
        in_specs=[_q_spec(A_HPS), _ROPE_SPEC, _kv_spec(A_HPS), _kv_spec(A_HPS)],
        out_specs=pl.BlockSpec((None, TQ, LANE), lambda b, h, i: (b, i, h)),
        out_shape=jax.ShapeDtypeStruct((BATCH, SEQ, A_HEADS * A_V), BF16),
        scratch_shapes=_pipe_scratch(TQ) * A_HPS,
        compiler_params=_attn_params(),
        name="mla_attn",
    )(qa, rope_tab, ka, va)


C_HPS = 2


def _diff_kernel(q_ref, rope_ref, kv_ref, lqk_ref, sn_ref, o_ref, *scratch, lam_init):
    i = pl.program_id(2)
    lane = lax.broadcasted_iota(jnp.int32, (TQ, LANE), 1)
    qs = _rope_q(q_ref, rope_ref, 2)
    streams = []
    for h in range(C_HPS):
        sl = slice(h * LANE, (h + 1) * LANE)
        q = qs[h]
        zero = jnp.zeros_like(q)
        q2 = jnp.concatenate([jnp.where(lane < C_DH, q, zero),
                              jnp.where((lane >= C_DH) & (lane < 2 * C_DH), q, zero)], axis=0)
        streams.append((q2,
                        lambda j, sl=sl: _k_chunk(kv_ref, j, sl),
                        lambda j, sl=sl: _vt_upper(_k_chunk(kv_ref, j, sl)),
                        scratch[N_PIPE * h:N_PIPE * (h + 1)]))
    accs = _causal_attention(streams, i)
    lqk = lqk_ref[...]
    lam = (jnp.exp(jnp.sum(lqk[0:1] * lqk[1:2], axis=-1, keepdims=True))
           - jnp.exp(jnp.sum(lqk[2:3] * lqk[3:4], axis=-1, keepdims=True)) + lam_init)
    diffs = []
    for a in accs:
        o = _normalise(a)
        diffs.append(o[:, 0:TQ] - lam * o[:, TQ:2 * TQ])
    d = jnp.concatenate(diffs, axis=0).T
    d2 = d * d
    first = lane < 2 * C_DH
    ms0 = jnp.sum(jnp.where(first, d2, 0.0), axis=-1, keepdims=True)
    ms1 = jnp.sum(jnp.where(first, 0.0, d2), axis=-1, keepdims=True)
    ms = jnp.where(first, ms0, ms1) * (1.0 / (2 * C_DH))
    y = d * lax.rsqrt(ms + EPS) * sn_ref[...] * (1.0 - lam_init)
    o_ref[...] = y.astype(BF16)


def _diff_call(l, qc, rope_tab, kd, lqk, sn):
    lam_init = 0.8 - 0.6 * math.exp(-0.3 * l)
    return pl.pallas_call(
        functools.partial(_diff_kernel, lam_init=lam_init),
        grid=(BATCH, C_HEADS // C_HPS, NQ),
        in_specs=[_q_spec(C_HPS), _ROPE_SPEC, _kv_spec(C_HPS),
                  _layer_spec((8, LANE), l), _layer_spec((1, LANE), l)],
        out_specs=pl.BlockSpec((None, TQ, LANE), lambda b, h, i: (b, i, h)),
        out_shape=jax.ShapeDtypeStruct((BATCH, SEQ, C_HEADS * 2 * C_DH), BF16),
        scratch_shapes=_pipe_scratch(2 * TQ) * C_HPS,
        compiler_params=_attn_params(),
        name="diff_attn",
    )(qc, rope_tab, kd, lqk, sn)


B_GPS = 2


def _nsa_kernel(q_ref, rope_ref, x16_ref, ks_ref, vs_ref, kw_ref, gt_ref,
                petop_ref, pebot_ref, w1top_ref, w1bot_ref, w2k_ref, w2v_ref, ovt_ref,
                o_ref, kc_s, vct_s, *slc_scratch):
    i = pl.program_id(2)

    @pl.when(i == 0)
    def _():
        for g in range(B_GPS):
            x = x16_ref[g]
            yt = _dot((x + petop_ref[...]).astype(BF16), w1top_ref[...])
            yb = _dot((x + pebot_ref[...]).astype(BF16), w1bot_ref[...])
            pre = yt + pltpu.roll(yb, N_CMP_PAD - 1, 0)
            act = (pre * jax.nn.sigmoid(pre)).astype(BF16)
            kc_s[g] = _dot(act, w2k_ref[...]).astype(BF16)
            vct_s[g] = _dot(act, w2v_ref[...]).T[0:VROW].astype(BF16)

    qs = _rope_q(q_ref, rope_ref, 1)
    fronts = [_nsa_front(i, g, qs[g * B_HPG:(g + 1) * B_HPG], kw_ref, kc_s, vct_s, ovt_ref)
              for g in range(B_GPS)]

    streams = []
    for g in range(B_GPS):
        sl = slice(g * LANE, (g + 1) * LANE)
        streams.append((fronts[g][0],
                        lambda j, sl=sl: _k_chunk(ks_ref, j, sl),
                        lambda j, sl=sl: _vt_upper(_k_chunk(vs_ref, j, sl)),
                        slc_scratch[N_PIPE * g:N_PIPE * (g + 1)]))
    accs = _causal_attention(streams, i)

    for g in range(B_GPS):
        _, o_cmp, o_win = fronts[g]
        o_slc = _normalise(accs[g])
        gate = gt_ref[g * GATE_ROWS:(g + 1) * GATE_ROWS, :]
        outs = []
        for h in range(B_HPG):
            sl = slice(h * TQ, (h + 1) * TQ)
            outs.append(gate[3 * h:3 * h + 1] * o_cmp[:, sl]
                        + gate[3 * h + 1:3 * h + 2] * o_slc[:, sl]
                        + gate[3 * h + 2:3 * h + 3] * o_win[:, sl])
        base = g * 2 * LANE
        o_ref[:, base:base + LANE] = jnp.concatenate(outs[0:2], axis=0).T.astype(BF16)
        o_ref[:, base + LANE:base + 2 * LANE] = jnp.concatenate(
            [outs[2], jnp.zeros((VROW, TQ), F32)], axis=0).T.astype(BF16)


def _nsa_front(i, g, q_heads, kw_ref, kc_s, vct_s, ovt_ref):
    r3 = B_HPG * TQ
    gl = slice(g * LANE, (g + 1) * LANE)
    q3 = jnp.concatenate(q_heads, axis=0)

    kk, tt = _tile_pos(r3)
    c0 = jnp.maximum(i - 1, 0)
    kv_diag = _k_chunk(kw_ref, i, gl)
    kv_prev = _k_chunk(kw_ref, c0, gl)
    s_diag = jnp.where(kk <= tt, _dot_nt(kv_diag, q3), MASKED)
    s_prev = jnp.where(tt + jnp.where(i >= 1, 0, TQ) < kk, _dot_nt(kv_prev, q3), MASKED)
    m_win = jnp.maximum(jnp.max(s_diag, axis=0, keepdims=True),
                        jnp.max(s_prev, axis=0, keepdims=True))
    o_win = _normalise(_dot(_vt_upper(kv_diag), jnp.exp2(s_diag - m_win).astype(BF16))
                       + _dot(_vt_upper(kv_prev), jnp.exp2(s_prev - m_win).astype(BF16)))

    s = _dot_nt(kc_s[g], q3)
    c_end = lax.broadcasted_iota(jnp.int32, (N_CMP_PAD, 1), 0) * CMP_STRIDE + (CMP_LEN - 1)
    valid = c_end <= i * TQ + tt
    smax = jnp.max(jnp.where(valid, s, MASKED), axis=0, keepdims=True)
    e = jnp.where(valid, jnp.exp2(s - smax), 0.0)
    den = jnp.sum(e, axis=0, keepdims=True)
    p = e / jnp.where(den > 0.0, den, 1.0)
    o_cmp = _dot(vct_s[g], p.astype(BF16))

    psum = p[:, 0:TQ] + p[:, TQ:2 * TQ] + p[:, 2 * TQ:3 * TQ]
    p_hi = psum.astype(BF16)
    p_lo = (psum - p_hi.astype(F32)).astype(BF16)
    ovt = ovt_ref[...]
    imp = (_dot(ovt, p_hi) + _dot(ovt, p_lo))[64:128]
    m_idx = lax.broadcasted_iota(jnp.int32, (N_SLC, TQ), 0)
    cur = (i * TQ + lax.broadcasted_iota(jnp.int32, (N_SLC, TQ), 1)) // SLC_LEN
    forced = (m_idx == 0) | (m_idx == cur) | (m_idx == cur - 1)
    score = jnp.where(forced, 1e9, jnp.where(m_idx <= cur, imp, -1.0))
    sub = lax.broadcasted_iota(jnp.int32, (SUBLANES, TQ), 0)
    bias = [jnp.zeros((64, TQ), F32)]
    for v in range(N_SLC // SUBLANES):
        tile = score[v * SUBLANES:(v + 1) * SUBLANES, :]
        rank = jnp.zeros((SUBLANES, TQ), jnp.int32)
        for mp in range(N_SLC):
            srow = score[mp:mp + 1, :]
            if mp < v * SUBLANES:
                ahead = srow >= tile
            elif mp >= (v + 1) * SUBLANES:
                ahead = srow > tile
            else:
                ahead = (srow > tile) | ((srow == tile) & (sub > mp - v * SUBLANES))
            rank = rank + jnp.where(ahead, 1, 0)
        bias.append(jnp.where(rank < SLC_TOPN, 0.0, -SEL_BIG))
    bias = jnp.concatenate(bias, axis=0).T.astype(BF16)
    q_sel = q3 + jnp.concatenate([bias] * B_HPG, axis=0)
    return q_sel, o_cmp, o_win


def _nsa_call(l, qb, rope_tab, x16, ks, vs, kw, gt, petop, pebot, w1top, w1bot, w2k, w2v, ovt):
    r3 = B_HPG * TQ
    return pl.pallas_call(
        _nsa_kernel,
        grid=(BATCH, B_KV_GROUPS // B_GPS, NQ),
        in_specs=[_q_spec(B_GPS * B_HPG), _ROPE_SPEC,
                  pl.BlockSpec((B_GPS, None, N_CMP_PAD, 16 * LANE), lambda b, g, i: (g, b, 0, 0),
                               pipeline_mode=pl.Buffered(1)),
                  _kv_spec(B_GPS, 1), _kv_spec(B_GPS, 1), _kv_spec(B_GPS, 1),
                  pl.BlockSpec((None, B_GPS * GATE_ROWS, TQ), lambda b, g, i: (b * NQ + i, g, 0)),
                  _layer_spec((1, 16 * LANE), l), _layer_spec((1, 16 * LANE), l),
                  _layer_spec((16 * LANE, LANE), l), _layer_spec((16 * LANE, LANE), l),
                  _layer_spec((LANE, LANE), l), _layer_spec((LANE, LANE), l),
                  _const_spec((LANE, N_CMP_PAD))],
        out_specs=pl.BlockSpec((None, TQ, B_GPS * 2 * LANE), lambda b, g, i: (b, i, g)),
        out_shape=jax.ShapeDtypeStruct((BATCH, SEQ, B_KV_GROUPS * 2 * LANE), BF16),
        scratch_shapes=[pltpu.VMEM((B_GPS, N_CMP_PAD, LANE), BF16),
                        pltpu.VMEM((B_GPS, VROW, N_CMP_PAD), BF16)]
        + _pipe_scratch(r3) * B_GPS,
        compiler_params=_attn_params(),
        name="nsa_attn",
    )(qb, rope_tab, x16, ks, vs, kw, gt, petop, pebot, w1top, w1bot, w2k, w2v, ovt)


def _expand_w_in(w):
    a = w[..., :A_COLS]
    b = w[..., A_COLS:A_COLS + B_COLS]
    c = w[..., A_COLS + B_COLS:]
    z = lambda n: jnp.zeros(w.shape[:-1] + (n,), w.dtype)
    gate = lambda g: [b[..., 1152 + g * 9:1161 + g * 9], z(GATE_ROWS - 9)]
    parts = [a[..., 0:192], a[..., 320:352]] + gate(0) + gate(1) + [a[..., 192:320]]
    for h in range(B_HEADS):
        parts += [b[..., h * 64:(h + 1) * 64], z(64)]
    for base in (384, 640, 896):
        for g in range(B_KV_GROUPS):
            parts += [b[..., base + g * 64:base + g * 64 + 64],
                      b[..., base + 128 + g * 64:base + 192 + g * 64]]

    def rot_mid(t, o):
        return [t[..., o + 8:o + 24], t[..., o:o + 8], t[..., o + 24:o + 32]]

    for h in range(C_HEADS):
        parts += rot_mid(c, h * 64) + rot_mid(c, h * 64 + 32) + [z(64)]
    for h in range(C_HEADS):
        parts += (rot_mid(c, 256 + h * 64) + rot_mid(c, 288 + h * 64)
                  + [c[..., 512 + h * 64:576 + h * 64]])
    return jnp.concatenate(parts, axis=-1).astype(BF16)


def _expand_w_out(w):
    z = jnp.zeros((DEPTH, 64, D_MODEL), w.dtype)
    gw = B_HPG * B_DH
    return jnp.concatenate([w[:, 0:WO_B], w[:, WO_B:WO_B + gw], z, w[:, WO_B + gw:WO_B + 2 * gw], z,
                            w[:, WO_B + 2 * gw:]], axis=1).astype(BF16)


def _rope_table(positions):
    def inv(rot):
        return 1.0 / (ROPE_THETA ** (jnp.arange(0, rot, 2, dtype=F32) / rot))

    fa, fb, fc = inv(A_ROPE), inv(B_ROT), inv(C_ROT)
    z = lambda n: jnp.zeros((n,), F32)
    freq = jnp.concatenate([fb, fb, fc, fc, z(24), fc, fc, z(8), fa, fa, z(32)])
    sign = jnp.concatenate([-jnp.ones(8), jnp.ones(8), -jnp.ones(4), jnp.ones(4), z(24),
                            -jnp.ones(4), jnp.ones(4), z(8), -jnp.ones(16), jnp.ones(16),
                            z(32)]).astype(F32)
    ang = positions.astype(F32)[:, :, None] * freq
    return jnp.concatenate([jnp.cos(ang), sign * jnp.sin(ang)], axis=2)


def _overlap_t():
    c = jnp.arange(N_CMP_PAD)[None, :]
    m = jnp.arange(N_SLC)[:, None]
    ov = ((c * CMP_STRIDE < m * SLC_LEN + SLC_LEN)
          & (c * CMP_STRIDE + CMP_LEN - 1 >= m * SLC_LEN) & (c < N_CMP))
    return jnp.concatenate([jnp.zeros((64, N_CMP_PAD), BF16), ov.astype(BF16)], axis=0)


def kernel(x, positions, ffn1_norm, ffn1_wg, ffn1_wu, ffn1_wd, mix_norm, w_in, mla_q_norm, mla_kv_norm, mla_w_uq, mla_w_ukv, nsa_pe_k, nsa_pe_v, nsa_phi_k1, nsa_phi_k2, nsa_phi_v1, nsa_phi_v2, diff_lq1, diff_lk1, diff_lq2, diff_lk2, diff_sub_norm, w_out, ffn2_norm, ffn2_wg, ffn2_wu, ffn2_wd, final_norm):
    xf = x.reshape(N_TOK, D_MODEL)
    rope_tab = _rope_table(positions)
    ovt = _overlap_t()
    bf = lambda t: t.astype(BF16)
    row = lambda t: t.reshape(DEPTH, 1, t.shape[-1])
    pad_to = lambda t, axis, n: jnp.pad(
        t, [(0, n - t.shape[a]) if a == axis else (0, 0) for a in range(t.ndim)])

    ffn1 = (row(ffn1_norm), bf(ffn1_wg), bf(ffn1_wu), bf(ffn1_wd))
    ffn2 = (row(ffn2_norm), bf(ffn2_wg), bf(ffn2_wu), bf(ffn2_wd))
    w_e = _expand_w_in(w_in)
    w_o = _expand_w_out(w_out)
    wuq = pad_to(pad_to(mla_w_uq.reshape(DEPTH, A_Q_RANK, A_HEADS, A_NOPE + A_ROPE), 3, LANE), 1, 256)
    wuq = bf(wuq.reshape(DEPTH, 256, A_HEADS * LANE))
    wkv = mla_w_ukv.reshape(DEPTH, A_KV_RANK, A_HEADS, A_NOPE + A_V)
    wukv = bf(jnp.concatenate(
        [pad_to(wkv[..., :A_NOPE], 3, LANE).reshape(DEPTH, A_KV_RANK, A_HEADS * LANE),
         pad_to(wkv[..., A_NOPE:], 3, LANE).reshape(DEPTH, A_KV_RANK, A_HEADS * LANE)], axis=2))
    qn = pad_to(row(mla_q_norm), 2, 256)
    kvn = row(mla_kv_norm)

    k1 = nsa_phi_k1.reshape(DEPTH, CMP_LEN, B_DH, B_DH)
    v1 = nsa_phi_v1.reshape(DEPTH, CMP_LEN, B_DH, B_DH)
    w1 = bf(jnp.concatenate([pad_to(k1, 3, LANE), jnp.pad(v1, ((0, 0), (0, 0), (0, 0), (64, 0)))],
                            axis=2))
    w1top = w1[:, :16].reshape(DEPTH, 16 * LANE, LANE)
    w1bot = w1[:, 16:].reshape(DEPTH, 16 * LANE, LANE)
    pe = jnp.concatenate([nsa_pe_k, nsa_pe_v], axis=2)
    petop = pe[:, :16].reshape(DEPTH, 1, 16 * LANE)
    pebot = pe[:, 16:].reshape(DEPTH, 1, 16 * LANE)
    w2k = bf(pad_to(pad_to(nsa_phi_k2, 1, LANE), 2, LANE))
    w2v = bf(jnp.pad(nsa_phi_v2, ((0, 0), (64, 0), (0, 64))))
    lqk = pad_to(pad_to(jnp.stack([diff_lq1, diff_lk1, diff_lq2, diff_lk2], axis=1), 1, 8), 2, LANE)
    sn = jnp.tile(row(diff_sub_norm), (1, 1, C_HPS))

    b3 = lambda t: t.reshape(BATCH, SEQ, t.shape[-1])
    flat = lambda t: t.reshape(N_TOK, t.shape[-1])
    rope3 = rope_tab
    for l in range(DEPTH):
        xf = _ffn_call(l, xf, *ffn1)
        (qa, ka, va, qb, kvc, ks, vs, kw, gt, qc, kd) = _proj_call(
            l, xf, row(mix_norm), w_e, rope_tab, qn, kvn, wuq, wukv)
        oa = _mla_call(b3(qa), rope3, b3(ka), b3(va))
        x16 = kvc.reshape(B_KV_GROUPS, BATCH, N_CMP_PAD, 16 * LANE)
        ob = _nsa_call(l, b3(qb), rope3, x16, b3(ks), b3(vs), b3(kw), gt,
                       petop, pebot, w1top, w1bot, w2k, w2v, ovt)
        oc = _diff_call(l, b3(qc), rope3, b3(kd), lqk, sn)
        fg = final_norm.reshape(1, D_MODEL) if l == DEPTH - 1 else None
        xf = _ffn_call(l, xf, *ffn2, mix=(flat(oa), flat(ob), flat(oc), w_o), final_gain=fg)
    return xf.reshape(BATCH, SEQ, D_MODEL)
```

```python
import functools
import math

import jax
import jax.numpy as jnp
from jax import lax
from jax.experimental import pallas as pl
from jax.experimental.pallas import tpu as pltpu

F32 = jnp.float32
BF16 = jnp.bfloat16

D_MODEL = 1024
BATCH = 4
SEQ = 4096
DEPTH = 2
N_TOK = BATCH * SEQ
EPS = 1e-6
ROPE_THETA = 500000.0
D_FF = 2816

A_HEADS = 6
A_Q_RANK = 192
A_KV_RANK = 128
A_NOPE = 64
A_ROPE = 32
A_V = 64

B_HEADS = 6
B_KV_GROUPS = 2
B_HPG = 3
B_DH = 64
B_ROT = 16
CMP_LEN = 32
CMP_STRIDE = 16
N_CMP = (SEQ - CMP_LEN) // CMP_STRIDE + 1
N_CMP_PAD = SEQ // CMP_STRIDE
SLC_LEN = 64
N_SLC = SEQ // SLC_LEN
SLC_TOPN = 16
WINDOW = 512

C_HEADS = 4
C_DH = 32
C_ROT = 8

A_COLS = A_Q_RANK + A_KV_RANK + A_ROPE
B_COLS = B_HEADS * B_DH + 6 * B_KV_GROUPS * B_DH + 3 * B_HEADS
C_COLS = 3 * C_HEADS * 2 * C_DH

LANE = 128
SUBLANES = 8
VROW = 64
VT_ROWS = 72
MASKED = -1e30
SEL_BIG = 1e9
LOG2E = 1.0 / math.log(2.0)

TM = 512
TF = 256
NFC = D_FF // TF
TQ = 512
TK = 512
NQ = SEQ // TQ
NCH = N_TOK // TK
GATE_ROWS = 16

CE_A = 0
CE_BQ = 384
CE_KVC = 1152
CE_KSVS = 1408
CE_KWVW = 1664
CE_CQ = 1920
CE_CKV = 2432
CE = 2944

A_HPS = 3
A_OW = -(-A_HPS * A_V // LANE) * LANE

WO_B = A_HEADS // A_HPS * A_OW
WO_C = WO_B + B_KV_GROUPS * 2 * LANE
WO_ROWS = WO_C + C_HEADS * 2 * C_DH

VMEM_LIMIT = 56 * 1024 * 1024


def _const_spec(shape):
    nd = len(shape)
    return pl.BlockSpec(shape, lambda *_: (0,) * nd, pipeline_mode=pl.Buffered(1))


def _layer_spec(shape, l):
    nd = len(shape)
    return pl.BlockSpec((None,) + tuple(shape), lambda *_: (l,) + (0,) * nd,
                        pipeline_mode=pl.Buffered(1))


def _rms(x, g, n):
    ms = jnp.sum(x * x, axis=-1, keepdims=True) * (1.0 / n)
    return x * lax.rsqrt(ms + EPS) * g


def _dot(a, b):
    return jnp.dot(a, b, preferred_element_type=F32)


def _ffn_kernel(*refs, mix, final):
    it = iter(refs)
    x_ref = next(it)
    if mix:
        oa_ref, ob_ref, oc_ref, wo_ref = next(it), next(it), next(it), next(it)
    g_ref, wg_ref, wu_ref, wd_ref = next(it), next(it), next(it), next(it)
    if final:
        fg_ref = next(it)
    o_ref = next(it)
    acc_ref = next(it)

    x = x_ref[...]
    if mix:
        x = x + _dot(oa_ref[...], wo_ref[0:WO_B, :])
        x = x + _dot(ob_ref[...], wo_ref[WO_B:WO_C, :])
        x = x + _dot(oc_ref[...], wo_ref[WO_C:WO_ROWS, :])
    xn = _rms(x, g_ref[...], D_MODEL).astype(BF16)
    for c in range(NFC):
        g = _dot(xn, wg_ref[:, c * TF:(c + 1) * TF])
        u = _dot(xn, wu_ref[:, c * TF:(c + 1) * TF])
        h = (g * jax.nn.sigmoid(g) * u).astype(BF16)
        d = _dot(h, wd_ref[c * TF:(c + 1) * TF, :])
        if c == 0:
            acc_ref[...] = d
        else:
            acc_ref[...] += d
    y = x + 0.5 * acc_ref[...]
    if final:
        y = _rms(y, fg_ref[...], D_MODEL)
    o_ref[...] = y


def _ffn_call(l, x, gain, wg, wu, wd, mix=None, final_gain=None):
    tok = lambda w: pl.BlockSpec((TM, w), lambda i: (i, 0))
    args = [x]
    specs = [tok(D_MODEL)]
    if mix is not None:
        oa, ob, oc, wo = mix
        args += [oa, ob, oc, wo]
        specs += [tok(WO_B), tok(WO_C - WO_B), tok(WO_ROWS - WO_C),
                  _layer_spec((WO_ROWS, D_MODEL), l)]
    args += [gain, wg, wu, wd]
    specs += [_layer_spec((1, D_MODEL), l), _layer_spec((D_MODEL, D_FF), l),
              _layer_spec((D_MODEL, D_FF), l), _layer_spec((D_FF, D_MODEL), l)]
    if final_gain is not None:
        args.append(final_gain)
        specs.append(_const_spec((1, D_MODEL)))
    kern = functools.partial(_ffn_kernel, mix=mix is not None, final=final_gain is not None)
    return pl.pallas_call(
        kern,
        grid=(N_TOK // TM,),
        in_specs=specs,
        out_specs=tok(D_MODEL),
        out_shape=jax.ShapeDtypeStruct((N_TOK, D_MODEL), F32),
        scratch_shapes=[pltpu.VMEM((TM, D_MODEL), F32)],
        compiler_params=pltpu.CompilerParams(
            dimension_semantics=("arbitrary",), vmem_limit_bytes=VMEM_LIMIT),
        name="ffn",
    )(*args)


ROPE_HALF = (16, 8, 4)


def _rope_fn(table):
    lane = lax.broadcasted_iota(jnp.int32, (table.shape[0], LANE), 1)
    between = lambda a, b: (lane >= a) & (lane < b)
    rot = (between(64, 96), lane < 16, between(16, 24) | between(48, 56))
    second = (between(80, 96), between(8, 16), between(20, 24) | between(52, 56))
    cache = {}

    def rope(v, typ):
        if typ not in cache:
            cache[typ] = (jnp.where(rot[typ], table[:, 0:LANE], 1.0),
                          jnp.where(rot[typ], table[:, LANE:2 * LANE], 0.0))
        cos, sin = cache[typ]
        r = ROPE_HALF[typ]
        partner = jnp.where(second[typ], pltpu.roll(v, r, 1), pltpu.roll(v, LANE - r, 1))
        return v * cos + partner * sin

    return rope


def _rope_q(q_ref, rope_ref, typ):
    rope = _rope_fn(rope_ref[...])
    nb = q_ref.shape[1] // LANE
    return [rope(q_ref[:, b * LANE:(b + 1) * LANE].astype(F32), typ).astype(BF16)
            for b in range(nb)]


def _proj_kernel(x_ref, g_ref, w_ref, rope_ref, qn_ref, kvn_ref, wuq_ref, wukv_ref,
                 qa_ref, ka_ref, va_ref, qb_ref, kvc_ref, ks_ref, vs_ref, kw_ref,
                 gt_ref, qc_ref, kd_ref):
    xn = _rms(x_ref[...], g_ref[...], D_MODEL).astype(BF16)
    lane = lax.broadcasted_iota(jnp.int32, (TM, LANE), 1)
    lo = lane < 64
    rope = _rope_fn(rope_ref[...])
    ones_row = (lane == VROW).astype(F32)

    def blk(v, b):
        return v[:, b * LANE:(b + 1) * LANE]

    ha = _dot(xn, w_ref[:, CE_A:CE_A + 384])
    a1 = blk(ha, 1)
    cq = jnp.concatenate([blk(ha, 0), jnp.where(lo, a1, 0.0)], axis=1)
    cqn = _rms(cq, qn_ref[...], A_Q_RANK).astype(BF16)
    ckvn = _rms(blk(ha, 2), kvn_ref[...], A_KV_RANK).astype(BF16)
    kpe = jnp.where((lane >= 64) & (lane < 96), rope(a1, 0), 0.0)
    gt_ref[0] = jax.nn.sigmoid(a1).T[96:128, :]
    q = _dot(cqn, wuq_ref[...])
    kv = _dot(ckvn, wukv_ref[...])
    qa_ref[...] = (q * ((A_NOPE + A_ROPE) ** -0.5 * LOG2E)).astype(BF16)
    for h in range(A_HEADS):
        ka_ref[:, h * LANE:(h + 1) * LANE] = (blk(kv, h) + kpe).astype(BF16)
        va_ref[:, h * LANE:(h + 1) * LANE] = (blk(kv, A_HEADS + h) + ones_row).astype(BF16)

    qb_ref[...] = (_dot(xn, w_ref[:, CE_BQ:CE_BQ + 768]) * (B_DH ** -0.5 * LOG2E)).astype(BF16)
    hk = _dot(xn, w_ref[:, CE_KVC:CE_KVC + 256])
    for g in range(B_KV_GROUPS):
        kvc_ref[g] = rope(blk(hk, g), 1)
    tpos = (pl.program_id(0) % (SEQ // TM)) * TM + lax.broadcasted_iota(jnp.int32, (TM, LANE), 0)
    onehot = ((lane - 64) == (tpos // SLC_LEN)).astype(F32)
    hk = _dot(xn, w_ref[:, CE_KSVS:CE_KSVS + 256])
    vs_ref[...] = hk.astype(BF16)
    for g in range(B_KV_GROUPS):
        ks_ref[:, g * LANE:(g + 1) * LANE] = jnp.where(lo, rope(blk(hk, g), 1), onehot).astype(BF16)
    hk = _dot(xn, w_ref[:, CE_KWVW:CE_KWVW + 256])
    for g in range(B_KV_GROUPS):
        kw_ref[:, g * LANE:(g + 1) * LANE] = rope(blk(hk, g), 1).astype(BF16)

    qc_ref[...] = (_dot(xn, w_ref[:, CE_CQ:CE_CQ + 512]) * (C_DH ** -0.5 * LOG2E)).astype(BF16)
    hc = _dot(xn, w_ref[:, CE_CKV:CE_CKV + 512])
    for h in range(C_HEADS):
        kd_ref[:, h * LANE:(h + 1) * LANE] = rope(blk(hc, h), 2).astype(BF16)


def _proj_call(l, x, gain, w_e, rope_tab, qn, kvn, wuq, wukv):
    tok = lambda w: pl.BlockSpec((TM, w), lambda i: (i, 0))
    bf = lambda w: jax.ShapeDtypeStruct((N_TOK, w), BF16)
    return pl.pallas_call(
        _proj_kernel,
        grid=(N_TOK // TM,),
        in_specs=[tok(D_MODEL), _layer_spec((1, D_MODEL), l), _layer_spec((D_MODEL, CE), l),
                  pl.BlockSpec((None, TM, 2 * LANE), lambda i: (i // (SEQ // TM), i % (SEQ // TM), 0)),
                  _layer_spec((1, 256), l), _layer_spec((1, LANE), l),
                  _layer_spec((256, 768), l), _layer_spec((LANE, 1536), l)],
        out_specs=[tok(768), tok(768), tok(768), tok(768),
                   pl.BlockSpec((B_KV_GROUPS, TM, LANE), lambda i: (0, i, 0)),
                   tok(256), tok(256), tok(256),
                   pl.BlockSpec((1, B_KV_GROUPS * GATE_ROWS, TM), lambda i: (i, 0, 0)),
                   tok(512), tok(512)],
        out_shape=[bf(768), bf(768), bf(768), bf(768),
                   jax.ShapeDtypeStruct((B_KV_GROUPS, N_TOK, LANE), F32),
                   bf(256), bf(256), bf(256),
                   jax.ShapeDtypeStruct((NCH, B_KV_GROUPS * GATE_ROWS, TM), F32),
                   bf(512), bf(512)],
        compiler_params=pltpu.CompilerParams(
            dimension_semantics=("arbitrary",), vmem_limit_bytes=VMEM_LIMIT),
        name="proj",
    )(x, gain, w_e, rope_tab, qn, kvn, wuq, wukv)


def _tile_pos(r):
    kk = lax.broadcasted_iota(jnp.int32, (TK, 1), 0)
    tt = lax.broadcasted_iota(jnp.int32, (1, r), 1) & (TQ - 1)
    return kk, tt


def _dot_nt(a, b):
    return lax.dot_general(a, b, (((1,), (1,)), ((), ())), preferred_element_type=F32)


def _score_stage(q, k, mask, m_prev, s_ref, m_ref):
    s = _dot_nt(k, q)
    if mask is not None:
        s = jnp.where(mask, s, MASKED)
    cm = jnp.max(s, axis=0, keepdims=True)
    m_ref[...] = cm if m_prev is None else jnp.maximum(m_prev[...], cm)
    s_ref[...] = s


def _value_stage(vt, s_ref, m_ref, mb_ref, acc_ref):
    m = m_ref[...]
    alpha = jnp.exp2(mb_ref[...] - m)
    p = jnp.exp2(s_ref[...] - m).astype(BF16)
    mb_ref[...] = m
    acc_ref[...] = alpha * acc_ref[...] + _dot(vt, p)


def _causal_attention(streams, i):
    for qt, k_of, vt_of, (s0, s1, m0, m1, mb, acc) in streams:
        kk, tt = _tile_pos(qt.shape[0])
        _score_stage(qt, k_of(i), kk <= tt, None, s0, m0)
        mb[...] = m0[...]
        acc[...] = jnp.zeros_like(acc)

    def step(score_j, pend_j, a, b):
        for qt, k_of, vt_of, sc in streams:
            bufs, maxs, mb, acc = (sc[0], sc[1]), (sc[2], sc[3]), sc[4], sc[5]
            _score_stage(qt, k_of(score_j), None, maxs[a], bufs[b], maxs[b])
            _value_stage(vt_of(pend_j), bufs[a], maxs[a], mb, acc)

    def fold(pend_j, a):
        for qt, k_of, vt_of, sc in streams:
            _value_stage(vt_of(pend_j), sc[a], sc[2 + a], sc[4], sc[5])

    def body(jj, carry):
        step(2 * jj, jnp.where(jj == 0, i, 2 * jj - 1), 0, 1)
        step(2 * jj + 1, 2 * jj, 1, 0)
        return carry

    npairs = i // 2
    lax.fori_loop(0, npairs, body, 0)
    pend = jnp.where(npairs == 0, i, 2 * npairs - 1)

    @pl.when(i % 2 == 1)
    def _():
        step(i - 1, pend, 0, 1)
        fold(i - 1, 1)

    @pl.when(i % 2 == 0)
    def _():
        fold(pend, 0)

    return [st[3][5][...] for st in streams]


def _normalise(acc):
    return acc[0:VROW] / acc[VROW:VROW + 1, :]


def _pipe_scratch(r):
    return ([pltpu.VMEM((TK, r), F32)] * 2 + [pltpu.VMEM((1, r), F32)] * 3
            + [pltpu.VMEM((VT_ROWS, r), F32)])


N_PIPE = 6


def _k_chunk(k_ref, j, sl=slice(None)):
    return k_ref[pl.ds(pl.multiple_of(j * TK, TK), TK), sl]


def _q_spec(nb):
    return pl.BlockSpec((None, TQ, nb * LANE), lambda b, h, i: (b, i, h))


def _kv_spec(nb, buffers=None):
    mode = {} if buffers is None else {"pipeline_mode": pl.Buffered(buffers)}
    return pl.BlockSpec((None, SEQ, nb * LANE), lambda b, h, i: (b, 0, h), **mode)


_ROPE_SPEC = pl.BlockSpec((None, TQ, 2 * LANE), lambda b, h, i: (b, i, 0))


def _vt_lower(chunk):
    return chunk.T[0:VT_ROWS]


def _vt_upper(chunk):
    ones = (lax.broadcasted_iota(jnp.int32, (VT_ROWS - VROW, TK), 0) == 0).astype(BF16)
    return jnp.concatenate([chunk.T[VROW:2 * VROW], ones], axis=0)


def _attn_params():
    return pltpu.CompilerParams(dimension_semantics=("arbitrary",) * 3,
                                vmem_limit_bytes=VMEM_LIMIT)


def _mla_kernel(q_ref, rope_ref, k_ref, v_ref, o_ref, *scratch):
    i = pl.program_id(2)
    qs = _rope_q(q_ref, rope_ref, 0)
    streams = []
    for h in range(A_HPS):
        sl = slice(h * LANE, (h + 1) * LANE)
        streams.append((qs[h],
                        lambda j, sl=sl: _k_chunk(k_ref, j, sl),
                        lambda j, sl=sl: _vt_lower(_k_chunk(v_ref, j, sl)),
                        scratch[N_PIPE * h:N_PIPE * (h + 1)]))
    accs = _causal_attention(streams, i)
    outs = [_normalise(a) for a in accs]
    if A_OW > A_HPS * A_V:
        outs.append(jnp.zeros((A_OW - A_HPS * A_V, TQ), F32))
    o_ref[...] = jnp.concatenate(outs, axis=0).T.astype(BF16)


def _mla_call(qa, rope_tab, ka, va):
    return pl.pallas_call(
        _mla_kernel,
        grid=(BATCH, A_HEADS // A_HPS, NQ),
        in_specs=[_q_spec(A_HPS), _ROPE_SPEC, _kv_spec(A_HPS), _kv_spec(A_HPS)],
        out_specs=pl.BlockSpec((None, TQ, A_OW), lambda b, h, i: (b, i, h)),
        out_shape=jax.ShapeDtypeStruct((BATCH, SEQ, WO_B), BF16),
        scratch_shapes=_pipe_scratch(TQ) * A_HPS,
        compiler_params=_attn_params(),
        name="mla_attn",
    )(qa, rope_tab, ka, va)


C_HPS = 4


def _diff_kernel(q_ref, rope_ref, kv_ref, lqk_ref, sn_ref, o_ref, *scratch, lam_init):
    i = pl.program_id(2)
    lane = lax.broadcasted_iota(jnp.int32, (TQ, LANE), 1)
    qs = _rope_q(q_ref, rope_ref, 2)
    streams = []
    for h in range(C_HPS):
        sl = slice(h * LANE, (h + 1) * LANE)
        q = qs[h]
        zero = jnp.zeros_like(q)
        q2 = jnp.concatenate([jnp.where(lane < C_DH, q, zero),
                              jnp.where((lane >= C_DH) & (lane < 2 * C_DH), q, zero)], axis=0)
        streams.append((q2,
                        lambda j, sl=sl: _k_chunk(kv_ref, j, sl),
                        lambda j, sl=sl: _vt_upper(_k_chunk(kv_ref, j, sl)),
                        scratch[N_PIPE * h:N_PIPE * (h + 1)]))
    accs = _causal_attention(streams, i)
    lqk = lqk_ref[...]
    lam = (jnp.exp(jnp.sum(lqk[0:1] * lqk[1:2], axis=-1, keepdims=True))
           - jnp.exp(jnp.sum(lqk[2:3] * lqk[3:4], axis=-1, keepdims=True)) + lam_init)
    diffs = []
    for a in accs:
        o = _normalise(a)
        diffs.append(o[:, 0:TQ] - lam * o[:, TQ:2 * TQ])
    d = jnp.concatenate(diffs, axis=0).T
    d2 = d * d
    hl = lax.broadcasted_iota(jnp.int32, d.shape, 1) // (2 * C_DH)
    ms = jnp.zeros_like(d)
    for h in range(C_HPS):
        ms_h = jnp.sum(jnp.where(hl == h, d2, 0.0), axis=-1, keepdims=True)
        ms = jnp.where(hl == h, ms_h, ms)
    y = d * lax.rsqrt(ms * (1.0 / (2 * C_DH)) + EPS) * sn_ref[...] * (1.0 - lam_init)
    o_ref[...] = y.astype(BF16)


def _diff_call(l, qc, rope_tab, kd, lqk, sn):
    lam_init = 0.8 - 0.6 * math.exp(-0.3 * l)
    return pl.pallas_call(
        functools.partial(_diff_kernel, lam_init=lam_init),
        grid=(BATCH, C_HEADS // C_HPS, NQ),
        in_specs=[_q_spec(C_HPS), _ROPE_SPEC, _kv_spec(C_HPS),
                  _layer_spec((8, LANE), l), _layer_spec((1, C_HPS * 2 * C_DH), l)],
        out_specs=pl.BlockSpec((None, TQ, C_HPS * 2 * C_DH), lambda b, h, i: (b, i, h)),
        out_shape=jax.ShapeDtypeStruct((BATCH, SEQ, C_HEADS * 2 * C_DH), BF16),
        scratch_shapes=_pipe_scratch(2 * TQ) * C_HPS,
        compiler_params=_attn_params(),
        name="diff_attn",
    )(qc, rope_tab, kd, lqk, sn)


B_GPS = 2


def _nsa_kernel(q_ref, rope_ref, x16_ref, ks_ref, vs_ref, kw_ref, gt_ref,
                petop_ref, pebot_ref, w1top_ref, w1bot_ref, w2k_ref, w2v_ref, ovt_ref,
                o_ref, kc_s, vct_s, *slc_scratch):
    i = pl.program_id(2)

    @pl.when(i == 0)
    def _():
        for g in range(B_GPS):
            x = x16_ref[g]
            yt = _dot((x + petop_ref[...]).astype(BF16), w1top_ref[...])
            yb = _dot((x + pebot_ref[...]).astype(BF16), w1bot_ref[...])
            pre = yt + pltpu.roll(yb, N_CMP_PAD - 1, 0)
            act = (pre * jax.nn.sigmoid(pre)).astype(BF16)
            kc_s[g] = _dot(act, w2k_ref[...]).astype(BF16)
            vct_s[g] = _dot(act, w2v_ref[...]).T[0:VROW].astype(BF16)

    qs = _rope_q(q_ref, rope_ref, 1)
    fronts = [_nsa_front(i, g, qs[g * B_HPG:(g + 1) * B_HPG], kw_ref, kc_s, vct_s, ovt_ref)
              for g in range(B_GPS)]

    streams = []
    for g in range(B_GPS):
        sl = slice(g * LANE, (g + 1) * LANE)
        streams.append((fronts[g][0],
                        lambda j, sl=sl: _k_chunk(ks_ref, j, sl),
                        lambda j, sl=sl: _vt_upper(_k_chunk(vs_ref, j, sl)),
                        slc_scratch[N_PIPE * g:N_PIPE * (g + 1)]))
    accs = _causal_attention(streams, i)

    for g in range(B_GPS):
        _, o_cmp, o_win = fronts[g]
        o_slc = _normalise(accs[g])
        gate = gt_ref[g * GATE_ROWS:(g + 1) * GATE_ROWS, :]
        outs = []
        for h in range(B_HPG):
            sl = slice(h * TQ, (h + 1) * TQ)
            outs.append(gate[3 * h:3 * h + 1] * o_cmp[:, sl]
                        + gate[3 * h + 1:3 * h + 2] * o_slc[:, sl]
                        + gate[3 * h + 2:3 * h + 3] * o_win[:, sl])
        base = g * 2 * LANE
        o_ref[:, base:base + LANE] = jnp.concatenate(outs[0:2], axis=0).T.astype(BF16)
        o_ref[:, base + LANE:base + 2 * LANE] = jnp.concatenate(
            [outs[2], jnp.zeros((VROW, TQ), F32)], axis=0).T.astype(BF16)


def _nsa_front(i, g, q_heads, kw_ref, kc_s, vct_s, ovt_ref):
    r3 = B_HPG * TQ
    gl = slice(g * LANE, (g + 1) * LANE)
    q3 = jnp.concatenate(q_heads, axis=0)

    kk, tt = _tile_pos(r3)
    c0 = jnp.maximum(i - 1, 0)
    kv_diag = _k_chunk(kw_ref, i, gl)
    kv_prev = _k_chunk(kw_ref, c0, gl)
    s_diag = jnp.where(kk <= tt, _dot_nt(kv_diag, q3), MASKED)
    s_prev = jnp.where(tt + jnp.where(i >= 1, 0, TQ) < kk, _dot_nt(kv_prev, q3), MASKED)
    m_win = jnp.maximum(jnp.max(s_diag, axis=0, keepdims=True),
                        jnp.max(s_prev, axis=0, keepdims=True))
    o_win = _normalise(_dot(_vt_upper(kv_diag), jnp.exp2(s_diag - m_win).astype(BF16))
                       + _dot(_vt_upper(kv_prev), jnp.exp2(s_prev - m_win).astype(BF16)))

    s = _dot_nt(kc_s[g], q3)
    c_end = lax.broadcasted_iota(jnp.int32, (N_CMP_PAD, 1), 0) * CMP_STRIDE + (CMP_LEN - 1)
    valid = c_end <= i * TQ + tt
    smax = jnp.max(jnp.where(valid, s, MASKED), axis=0, keepdims=True)
    e = jnp.where(valid, jnp.exp2(s - smax), 0.0)
    den = jnp.sum(e, axis=0, keepdims=True)
    p = e / jnp.where(den > 0.0, den, 1.0)
    o_cmp = _dot(vct_s[g], p.astype(BF16))

    psum = p[:, 0:TQ] + p[:, TQ:2 * TQ] + p[:, 2 * TQ:3 * TQ]
    p_hi = psum.astype(BF16)
    p_lo = (psum - p_hi.astype(F32)).astype(BF16)
    ovt = ovt_ref[...]
    imp = (_dot(ovt, p_hi) + _dot(ovt, p_lo))[64:128]
    m_idx = lax.broadcasted_iota(jnp.int32, (N_SLC, TQ), 0)
    cur = (i * TQ + lax.broadcasted_iota(jnp.int32, (N_SLC, TQ), 1)) // SLC_LEN
    forced = (m_idx == 0) | (m_idx == cur) | (m_idx == cur - 1)
    score = jnp.where(forced, 1e9, jnp.where(m_idx <= cur, imp, -1.0))
    sub = lax.broadcasted_iota(jnp.int32, (SUBLANES, TQ), 0)
    bias = [jnp.zeros((64, TQ), F32)]
    for v in range(N_SLC // SUBLANES):
        tile = score[v * SUBLANES:(v + 1) * SUBLANES, :]
        rank = jnp.zeros((SUBLANES, TQ), jnp.int32)
        for mp in range(N_SLC):
            srow = score[mp:mp + 1, :]
            if mp < v * SUBLANES:
                ahead = srow >= tile
            elif mp >= (v + 1) * SUBLANES:
                ahead = srow > tile
            else:
                ahead = (srow > tile) | ((srow == tile) & (sub > mp - v * SUBLANES))
            rank = rank + jnp.where(ahead, 1, 0)
        bias.append(jnp.where(rank < SLC_TOPN, 0.0, -SEL_BIG))
    bias = jnp.concatenate(bias, axis=0).T.astype(BF16)
    q_sel = q3 + jnp.concatenate([bias] * B_HPG, axis=0)
    return q_sel, o_cmp, o_win


def _nsa_call(l, qb, rope_tab, x16, ks, vs, kw, gt, petop, pebot, w1top, w1bot, w2k, w2v, ovt):
    r3 = B_HPG * TQ
    return pl.pallas_call(
        _nsa_kernel,
        grid=(BATCH, B_KV_GROUPS // B_GPS, NQ),
        in_specs=[_q_spec(B_GPS * B_HPG), _ROPE_SPEC,
                  pl.BlockSpec((B_GPS, None, N_CMP_PAD, 16 * LANE), lambda b, g, i: (g, b, 0, 0),
                               pipeline_mode=pl.Buffered(1)),
                  _kv_spec(B_GPS, 1), _kv_spec(B_GPS, 1), _kv_spec(B_GPS, 1),
                  pl.BlockSpec((None, B_GPS * GATE_ROWS, TQ), lambda b, g, i: (b * NQ + i, g, 0)),
                  _layer_spec((1, 16 * LANE), l), _layer_spec((1, 16 * LANE), l),
                  _layer_spec((16 * LANE, LANE), l), _layer_spec((16 * LANE, LANE), l),
                  _layer_spec((LANE, LANE), l), _layer_spec((LANE, LANE), l),
                  _const_spec((LANE, N_CMP_PAD))],
        out_specs=pl.BlockSpec((None, TQ, B_GPS * 2 * LANE), lambda b, g, i: (b, i, g)),
        out_shape=jax.ShapeDtypeStruct((BATCH, SEQ, B_KV_GROUPS * 2 * LANE), BF16),
        scratch_shapes=[pltpu.VMEM((B_GPS, N_CMP_PAD, LANE), BF16),
                        pltpu.VMEM((B_GPS, VROW, N_CMP_PAD), BF16)]
        + _pipe_scratch(r3) * B_GPS,
        compiler_params=_attn_params(),
        name="nsa_attn",
    )(qb, rope_tab, x16, ks, vs, kw, gt, petop, pebot, w1top, w1bot, w2k, w2v, ovt)


def _expand_w_in(w):
    a = w[..., :A_COLS]
    b = w[..., A_COLS:A_COLS + B_COLS]
    c = w[..., A_COLS + B_COLS:]
    z = lambda n: jnp.zeros(w.shape[:-1] + (n,), w.dtype)
    gate = lambda g: [b[..., 1152 + g * 9:1161 + g * 9], z(GATE_ROWS - 9)]
    parts = [a[..., 0:192], a[..., 320:352]] + gate(0) + gate(1) + [a[..., 192:320]]
    for h in range(B_HEADS):
        parts += [b[..., h * 64:(h + 1) * 64], z(64)]
    for base in (384, 640, 896):
        for g in range(B_KV_GROUPS):
            parts += [b[..., base + g * 64:base + g * 64 + 64],
                      b[..., base + 128 + g * 64:base + 192 + g * 64]]

    def rot_mid(t, o):
        return [t[..., o + 8:o + 24], t[..., o:o + 8], t[..., o + 24:o + 32]]

    for h in range(C_HEADS):
        parts += rot_mid(c, h * 64) + rot_mid(c, h * 64 + 32) + [z(64)]
    for h in range(C_HEADS):
        parts += (rot_mid(c, 256 + h * 64) + rot_mid(c, 288 + h * 64)
                  + [c[..., 512 + h * 64:576 + h * 64]])
    return jnp.concatenate(parts, axis=-1).astype(BF16)


def _expand_w_out(w):
    z = lambda n: jnp.zeros((DEPTH, n, D_MODEL), w.dtype)
    parts = []
    aw = A_HPS * A_V
    for s in range(A_HEADS // A_HPS):
        parts.append(w[:, s * aw:(s + 1) * aw])
        if A_OW > aw:
            parts.append(z(A_OW - aw))
    b0 = A_HEADS * A_V
    gw = B_HPG * B_DH
    for g in range(B_KV_GROUPS):
        parts += [w[:, b0 + g * gw:b0 + (g + 1) * gw], z(2 * LANE - gw)]
    parts.append(w[:, b0 + B_KV_GROUPS * gw:])
    return jnp.concatenate(parts, axis=1).astype(BF16)


def _rope_table(positions):
    def inv(rot):
        return 1.0 / (ROPE_THETA ** (jnp.arange(0, rot, 2, dtype=F32) / rot))

    fa, fb, fc = inv(A_ROPE), inv(B_ROT), inv(C_ROT)
    z = lambda n: jnp.zeros((n,), F32)
    freq = jnp.concatenate([fb, fb, fc, fc, z(24), fc, fc, z(8), fa, fa, z(32)])
    sign = jnp.concatenate([-jnp.ones(8), jnp.ones(8), -jnp.ones(4), jnp.ones(4), z(24),
                            -jnp.ones(4), jnp.ones(4), z(8), -jnp.ones(16), jnp.ones(16),
                            z(32)]).astype(F32)
    ang = positions.astype(F32)[:, :, None] * freq
    return jnp.concatenate([jnp.cos(ang), sign * jnp.sin(ang)], axis=2)


def _overlap_t():
    c = jnp.arange(N_CMP_PAD)[None, :]
    m = jnp.arange(N_SLC)[:, None]
    ov = ((c * CMP_STRIDE < m * SLC_LEN + SLC_LEN)
          & (c * CMP_STRIDE + CMP_LEN - 1 >= m * SLC_LEN) & (c < N_CMP))
    return jnp.concatenate([jnp.zeros((64, N_CMP_PAD), BF16), ov.astype(BF16)], axis=0)


def kernel(x, positions, ffn1_norm, ffn1_wg, ffn1_wu, ffn1_wd, mix_norm, w_in, mla_q_norm, mla_kv_norm, mla_w_uq, mla_w_ukv, nsa_pe_k, nsa_pe_v, nsa_phi_k1, nsa_phi_k2, nsa_phi_v1, nsa_phi_v2, diff_lq1, diff_lk1, diff_lq2, diff_lk2, diff_sub_norm, w_out, ffn2_norm, ffn2_wg, ffn2_wu, ffn2_wd, final_norm):
    xf = x.reshape(N_TOK, D_MODEL)
    rope_tab = _rope_table(positions)
    ovt = _overlap_t()
    bf = lambda t: t.astype(BF16)
    row = lambda t: t.reshape(DEPTH, 1, t.shape[-1])
    pad_to = lambda t, axis, n: jnp.pad(
        t, [(0, n - t.shape[a]) if a == axis else (0, 0) for a in range(t.ndim)])

    ffn1 = (row(ffn1_norm), bf(ffn1_wg), bf(ffn1_wu), bf(ffn1_wd))
    ffn2 = (row(ffn2_norm), bf(ffn2_wg), bf(ffn2_wu), bf(ffn2_wd))
    w_e = _expand_w_in(w_in)
    w_o = _expand_w_out(w_out)
    wuq = pad_to(pad_to(mla_w_uq.reshape(DEPTH, A_Q_RANK, A_HEADS, A_NOPE + A_ROPE), 3, LANE), 1, 256)
    wuq = bf(wuq.reshape(DEPTH, 256, A_HEADS * LANE))
    wkv = mla_w_ukv.reshape(DEPTH, A_KV_RANK, A_HEADS, A_NOPE + A_V)
    wukv = bf(jnp.concatenate(
        [pad_to(wkv[..., :A_NOPE], 3, LANE).reshape(DEPTH, A_KV_RANK, A_HEADS * LANE),
         pad_to(wkv[..., A_NOPE:], 3, LANE).reshape(DEPTH, A_KV_RANK, A_HEADS * LANE)], axis=2))
    qn = pad_to(row(mla_q_norm), 2, 256)
    kvn = row(mla_kv_norm)

    k1 = nsa_phi_k1.reshape(DEPTH, CMP_LEN, B_DH, B_DH)
    v1 = nsa_phi_v1.reshape(DEPTH, CMP_LEN, B_DH, B_DH)
    w1 = bf(jnp.concatenate([pad_to(k1, 3, LANE), jnp.pad(v1, ((0, 0), (0, 0), (0, 0), (64, 0)))],
                            axis=2))
    w1top = w1[:, :16].reshape(DEPTH, 16 * LANE, LANE)
    w1bot = w1[:, 16:].reshape(DEPTH, 16 * LANE, LANE)
    pe = jnp.concatenate([nsa_pe_k, nsa_pe_v], axis=2)
    petop = pe[:, :16].reshape(DEPTH, 1, 16 * LANE)
    pebot = pe[:, 16:].reshape(DEPTH, 1, 16 * LANE)
    w2k = bf(pad_to(pad_to(nsa_phi_k2, 1, LANE), 2, LANE))
    w2v = bf(jnp.pad(nsa_phi_v2, ((0, 0), (64, 0), (0, 64))))
    lqk = pad_to(pad_to(jnp.stack([diff_lq1, diff_lk1, diff_lq2, diff_lk2], axis=1), 1, 8), 2, LANE)
    sn = jnp.tile(row(diff_sub_norm), (1, 1, C_HPS))

    b3 = lambda t: t.reshape(BATCH, SEQ, t.shape[-1])
    flat = lambda t: t.reshape(N_TOK, t.shape[-1])
    rope3 = rope_tab
    for l in range(DEPTH):
        xf = _ffn_call(l, xf, *ffn1)
        (qa, ka, va, qb, kvc, ks, vs, kw, gt, qc, kd) = _proj_call(
            l, xf, row(mix_norm), w_e, rope_tab, qn, kvn, wuq, wukv)
        oa = _mla_call(b3(qa), rope3, b3(ka), b3(va))
        x16 = kvc.reshape(B_KV_GROUPS, BATCH, N_CMP_PAD, 16 * LANE)
        ob = _nsa_call(l, b3(qb), rope3, x16, b3(ks), b3(vs), b3(kw), gt,
                       petop, pebot, w1top, w1bot, w2k, w2v, ovt)
        oc = _diff_call(l, b3(qc), rope3, b3(kd), lqk, sn)
        fg = final_norm.reshape(1, D_MODEL) if l == DEPTH - 1 else None
        xf = _ffn_call(l, xf, *ffn2, mix=(flat(oa), flat(ob), flat(oc), w_o), final_gain=fg)
    return xf.reshape(BATCH, SEQ, D_MODEL)
```

```python
import functools
import math

import jax
import jax.numpy as jnp
from jax import lax
from jax.experimental import pallas as pl
from jax.experimental.pallas import tpu as pltpu

F32 = jnp.float32
BF16 = jnp.bfloat16

D_MODEL = 1024
BATCH = 4
SEQ = 4096
DEPTH = 2
N_TOK = BATCH * SEQ
EPS = 1e-6
ROPE_THETA = 500000.0
D_FF = 2816

A_HEADS = 6
A_Q_RANK = 192
A_KV_RANK = 128
A_NOPE = 64
A_ROPE = 32
A_V = 64

B_HEADS = 6
B_KV_GROUPS = 2
B_HPG = 3
B_DH = 64
B_ROT = 16
CMP_LEN = 32
CMP_STRIDE = 16
N_CMP = (SEQ - CMP_LEN) // CMP_STRIDE + 1
N_CMP_PAD = SEQ // CMP_STRIDE
SLC_LEN = 64
N_SLC = SEQ // SLC_LEN
SLC_TOPN = 16
WINDOW = 512

C_HEADS = 4
C_DH = 32
C_ROT = 8

A_COLS = A_Q_RANK + A_KV_RANK + A_ROPE
B_COLS = B_HEADS * B_DH + 6 * B_KV_GROUPS * B_DH + 3 * B_HEADS
C_COLS = 3 * C_HEADS * 2 * C_DH

LANE = 128
SUBLANES = 8
VROW = 64
VT_ROWS = 72
MASKED = -1e30
SEL_BIG = 1e9
LOG2E = 1.0 / math.log(2.0)

TM = 512
TF = 256
NFC = D_FF // TF
TQ = 512
TK = 512
NQ = SEQ // TQ
NCH = N_TOK // TK
GATE_ROWS = 16

CE_A = 0
CE_BQ = 384
CE_KVC = 1152
CE_KSVS = 1408
CE_KWVW = 1664
CE_CQ = 1920
CE_CKV = 2432
CE = 2944

A_HPS = 6
A_OW = -(-A_HPS * A_V // LANE) * LANE

WO_B = A_HEADS // A_HPS * A_OW
WO_C = WO_B + B_KV_GROUPS * 2 * LANE
WO_ROWS = WO_C + C_HEADS * 2 * C_DH

VMEM_LIMIT = 56 * 1024 * 1024


def _const_spec(shape):
    nd = len(shape)
    return pl.BlockSpec(shape, lambda *_: (0,) * nd, pipeline_mode=pl.Buffered(1))


def _layer_spec(shape, l):
    nd = len(shape)
    return pl.BlockSpec((None,) + tuple(shape), lambda *_: (l,) + (0,) * nd,
                        pipeline_mode=pl.Buffered(1))


def _rms(x, g, n):
    ms = jnp.sum(x * x, axis=-1, keepdims=True) * (1.0 / n)
    return x * lax.rsqrt(ms + EPS) * g


def _dot(a, b):
    return jnp.dot(a, b, preferred_element_type=F32)


def _ffn_kernel(*refs, mix, final):
    it = iter(refs)
    x_ref = next(it)
    if mix:
        oa_ref, ob_ref, oc_ref, wo_ref = next(it), next(it), next(it), next(it)
    g_ref, wg_ref, wu_ref, wd_ref = next(it), next(it), next(it), next(it)
    if final:
        fg_ref = next(it)
    o_ref = next(it)
    acc_ref = next(it)

    x = x_ref[...]
    if mix:
        x = x + _dot(oa_ref[...], wo_ref[0:WO_B, :])
        x = x + _dot(ob_ref[...], wo_ref[WO_B:WO_C, :])
        x = x + _dot(oc_ref[...], wo_ref[WO_C:WO_ROWS, :])
    xn = _rms(x, g_ref[...], D_MODEL).astype(BF16)
    for c in range(NFC):
        g = _dot(xn, wg_ref[:, c * TF:(c + 1) * TF])
        u = _dot(xn, wu_ref[:, c * TF:(c + 1) * TF])
        h = (g * jax.nn.sigmoid(g) * u).astype(BF16)
        d = _dot(h, wd_ref[c * TF:(c + 1) * TF, :])
        if c == 0:
            acc_ref[...] = d
        else:
            acc_ref[...] += d
    y = x + 0.5 * acc_ref[...]
    if final:
        y = _rms(y, fg_ref[...], D_MODEL)
    o_ref[...] = y


def _ffn_call(l, x, gain, wg, wu, wd, mix=None, final_gain=None):
    tok = lambda w: pl.BlockSpec((TM, w), lambda i: (i, 0))
    args = [x]
    specs = [tok(D_MODEL)]
    if mix is not None:
        oa, ob, oc, wo = mix
        args += [oa, ob, oc, wo]
        specs += [tok(WO_B), tok(WO_C - WO_B), tok(WO_ROWS - WO_C),
                  _layer_spec((WO_ROWS, D_MODEL), l)]
    args += [gain, wg, wu, wd]
    specs += [_layer_spec((1, D_MODEL), l), _layer_spec((D_MODEL, D_FF), l),
              _layer_spec((D_MODEL, D_FF), l), _layer_spec((D_FF, D_MODEL), l)]
    if final_gain is not None:
        args.append(final_gain)
        specs.append(_const_spec((1, D_MODEL)))
    kern = functools.partial(_ffn_kernel, mix=mix is not None, final=final_gain is not None)
    return pl.pallas_call(
        kern,
        grid=(N_TOK // TM,),
        in_specs=specs,
        out_specs=tok(D_MODEL),
        out_shape=jax.ShapeDtypeStruct((N_TOK, D_MODEL), F32),
        scratch_shapes=[pltpu.VMEM((TM, D_MODEL), F32)],
        compiler_params=pltpu.CompilerParams(
            dimension_semantics=("arbitrary",), vmem_limit_bytes=VMEM_LIMIT),
        name="ffn",
    )(*args)


ROPE_HALF = (16, 8, 4)


def _rope_fn(table):
    lane = lax.broadcasted_iota(jnp.int32, (table.shape[0], LANE), 1)
    between = lambda a, b: (lane >= a) & (lane < b)
    rot = (between(64, 96), lane < 16, between(16, 24) | between(48, 56))
    second = (between(80, 96), between(8, 16), between(20, 24) | between(52, 56))
    cache = {}

    def rope(v, typ):
        if typ not in cache:
            cache[typ] = (jnp.where(rot[typ], table[:, 0:LANE], 1.0),
                          jnp.where(rot[typ], table[:, LANE:2 * LANE], 0.0))
        cos, sin = cache[typ]
        r = ROPE_HALF[typ]
        partner = jnp.where(second[typ], pltpu.roll(v, r, 1), pltpu.roll(v, LANE - r, 1))
        return v * cos + partner * sin

    return rope


def _rope_q(q_ref, rope_ref, typ):
    rope = _rope_fn(rope_ref[...])
    nb = q_ref.shape[1] // LANE
    return [rope(q_ref[:, b * LANE:(b + 1) * LANE].astype(F32), typ).astype(BF16)
            for b in range(nb)]


def _proj_kernel(x_ref, g_ref, w_ref, rope_ref, qn_ref, kvn_ref, wuq_ref, wukv_ref,
                 qa_ref, ka_ref, va_ref, qb_ref, kvc_ref, ks_ref, vs_ref, kw_ref,
                 gt_ref, qc_ref, kd_ref):
    xn = _rms(x_ref[...], g_ref[...], D_MODEL).astype(BF16)
    lane = lax.broadcasted_iota(jnp.int32, (TM, LANE), 1)
    lo = lane < 64
    rope = _rope_fn(rope_ref[...])
    ones_row = (lane == VROW).astype(F32)

    def blk(v, b):
        return v[:, b * LANE:(b + 1) * LANE]

    ha = _dot(xn, w_ref[:, CE_A:CE_A + 384])
    a1 = blk(ha, 1)
    cq = jnp.concatenate([blk(ha, 0), jnp.where(lo, a1, 0.0)], axis=1)
    cqn = _rms(cq, qn_ref[...], A_Q_RANK).astype(BF16)
    ckvn = _rms(blk(ha, 2), kvn_ref[...], A_KV_RANK).astype(BF16)
    kpe = jnp.where((lane >= 64) & (lane < 96), rope(a1, 0), 0.0)
    gt_ref[0] = jax.nn.sigmoid(a1).T[96:128, :]
    q = _dot(cqn, wuq_ref[...])
    kv = _dot(ckvn, wukv_ref[...])
    qa_ref[...] = (q * ((A_NOPE + A_ROPE) ** -0.5 * LOG2E)).astype(BF16)
    for h in range(A_HEADS):
        ka_ref[:, h * LANE:(h + 1) * LANE] = (blk(kv, h) + kpe).astype(BF16)
        va_ref[:, h * LANE:(h + 1) * LANE] = (blk(kv, A_HEADS + h) + ones_row).astype(BF16)

    qb_ref[...] = (_dot(xn, w_ref[:, CE_BQ:CE_BQ + 768]) * (B_DH ** -0.5 * LOG2E)).astype(BF16)
    hk = _dot(xn, w_ref[:, CE_KVC:CE_KVC + 256])
    for g in range(B_KV_GROUPS):
        kvc_ref[g] = rope(blk(hk, g), 1)
    tpos = (pl.program_id(0) % (SEQ // TM)) * TM + lax.broadcasted_iota(jnp.int32, (TM, LANE), 0)
    onehot = ((lane - 64) == (tpos // SLC_LEN)).astype(F32)
    hk = _dot(xn, w_ref[:, CE_KSVS:CE_KSVS + 256])
    vs_ref[...] = hk.astype(BF16)
    for g in range(B_KV_GROUPS):
        ks_ref[:, g * LANE:(g + 1) * LANE] = jnp.where(lo, rope(blk(hk, g), 1), onehot).astype(BF16)
    hk = _dot(xn, w_ref[:, CE_KWVW:CE_KWVW + 256])
    for g in range(B_KV_GROUPS):
        kw_ref[:, g * LANE:(g + 1) * LANE] = rope(blk(hk, g), 1).astype(BF16)

    qc_ref[...] = (_dot(xn, w_ref[:, CE_CQ:CE_CQ + 512]) * (C_DH ** -0.5 * LOG2E)).astype(BF16)
    hc = _dot(xn, w_ref[:, CE_CKV:CE_CKV + 512])
    for h in range(C_HEADS):
        kd_ref[:, h * LANE:(h + 1) * LANE] = rope(blk(hc, h), 2).astype(BF16)


def _proj_call(l, x, gain, w_e, rope_tab, qn, kvn, wuq, wukv):
    tok = lambda w: pl.BlockSpec((TM, w), lambda i: (i, 0))
    bf = lambda w: jax.ShapeDtypeStruct((N_TOK, w), BF16)
    return pl.pallas_call(
        _proj_kernel,
        grid=(N_TOK // TM,),
        in_specs=[tok(D_MODEL), _layer_spec((1, D_MODEL), l), _layer_spec((D_MODEL, CE), l),
                  pl.BlockSpec((None, TM, 2 * LANE), lambda i: (i // (SEQ // TM), i % (SEQ // TM), 0)),
                  _layer_spec((1, 256), l), _layer_spec((1, LANE), l),
                  _layer_spec((256, 768), l), _layer_spec((LANE, 1536), l)],
        out_specs=[tok(768), tok(768), tok(768), tok(768),
                   pl.BlockSpec((B_KV_GROUPS, TM, LANE), lambda i: (0, i, 0)),
                   tok(256), tok(256), tok(256),
                   pl.BlockSpec((1, B_KV_GROUPS * GATE_ROWS, TM), lambda i: (i, 0, 0)),
                   tok(512), tok(512)],
        out_shape=[bf(768), bf(768), bf(768), bf(768),
                   jax.ShapeDtypeStruct((B_KV_GROUPS, N_TOK, LANE), F32),
                   bf(256), bf(256), bf(256),
                   jax.ShapeDtypeStruct((NCH, B_KV_GROUPS * GATE_ROWS, TM), F32),
                   bf(512), bf(512)],
        compiler_params=pltpu.CompilerParams(
            dimension_semantics=("arbitrary",), vmem_limit_bytes=VMEM_LIMIT),
        name="proj",
    )(x, gain, w_e, rope_tab, qn, kvn, wuq, wukv)


def _tile_pos(r):
    kk = lax.broadcasted_iota(jnp.int32, (TK, 1), 0)
    tt = lax.broadcasted_iota(jnp.int32, (1, r), 1) & (TQ - 1)
    return kk, tt


def _dot_nt(a, b):
    return lax.dot_general(a, b, (((1,), (1,)), ((), ())), preferred_element_type=F32)


def _score_stage(q, k, mask, m_prev, s_ref, m_ref):
    s = _dot_nt(k, q)
    if mask is not None:
        s = jnp.where(mask, s, MASKED)
    cm = jnp.max(s, axis=0, keepdims=True)
    m_ref[...] = cm if m_prev is None else jnp.maximum(m_prev[...], cm)
    s_ref[...] = s


def _value_stage(vt, s_ref, m_ref, mb_ref, acc_ref):
    m = m_ref[...]
    alpha = jnp.exp2(mb_ref[...] - m)
    p = jnp.exp2(s_ref[...] - m).astype(BF16)
    mb_ref[...] = m
    acc_ref[...] = alpha * acc_ref[...] + _dot(vt, p)


def _causal_attention(streams, i):
    for qt, k_of, vt_of, (s0, s1, m0, m1, mb, acc) in streams:
        kk, tt = _tile_pos(qt.shape[0])
        _score_stage(qt, k_of(i), kk <= tt, None, s0, m0)
        mb[...] = m0[...]
        acc[...] = jnp.zeros_like(acc)

    def step(score_j, pend_j, a, b):
        for qt, k_of, vt_of, sc in streams:
            bufs, maxs, mb, acc = (sc[0], sc[1]), (sc[2], sc[3]), sc[4], sc[5]
            _score_stage(qt, k_of(score_j), None, maxs[a], bufs[b], maxs[b])
            _value_stage(vt_of(pend_j), bufs[a], maxs[a], mb, acc)

    def fold(pend_j, a):
        for qt, k_of, vt_of, sc in streams:
            _value_stage(vt_of(pend_j), sc[a], sc[2 + a], sc[4], sc[5])

    def body(jj, carry):
        step(2 * jj, jnp.where(jj == 0, i, 2 * jj - 1), 0, 1)
        step(2 * jj + 1, 2 * jj, 1, 0)
        return carry

    npairs = i // 2
    lax.fori_loop(0, npairs, body, 0)
    pend = jnp.where(npairs == 0, i, 2 * npairs - 1)

    @pl.when(i % 2 == 1)
    def _():
        step(i - 1, pend, 0, 1)
        fold(i - 1, 1)

    @pl.when(i % 2 == 0)
    def _():
        fold(pend, 0)

    return [st[3][5][...] for st in streams]


def _normalise(acc):
    return acc[0:VROW] / acc[VROW:VROW + 1, :]


def _pipe_scratch(r):
    return ([pltpu.VMEM((TK, r), F32)] * 2 + [pltpu.VMEM((1, r), F32)] * 3
            + [pltpu.VMEM((VT_ROWS, r), F32)])


N_PIPE = 6


def _k_chunk(k_ref, j, sl=slice(None)):
    return k_ref[pl.ds(pl.multiple_of(j * TK, TK), TK), sl]


def _q_spec(nb):
    return pl.BlockSpec((None, TQ, nb * LANE), lambda b, h, i: (b, i, h))


def _kv_spec(nb, buffers=None):
    mode = {} if buffers is None else {"pipeline_mode": pl.Buffered(buffers)}
    return pl.BlockSpec((None, SEQ, nb * LANE), lambda b, h, i: (b, 0, h), **mode)


_ROPE_SPEC = pl.BlockSpec((None, TQ, 2 * LANE), lambda b, h, i: (b, i, 0))


def _vt_lower(chunk):
    return chunk.T[0:VT_ROWS]


def _vt_upper(chunk):
    ones = (lax.broadcasted_iota(jnp.int32, (VT_ROWS - VROW, TK), 0) == 0).astype(BF16)
    return jnp.concatenate([chunk.T[VROW:2 * VROW], ones], axis=0)


def _attn_params():
    return pltpu.CompilerParams(dimension_semantics=("arbitrary",) * 3,
                                vmem_limit_bytes=VMEM_LIMIT)


def _mla_kernel(q_ref, rope_ref, k_ref, v_ref, o_ref, *scratch):
    i = pl.program_id(2)
    qs = _rope_q(q_ref, rope_ref, 0)
    streams = []
    for h in range(A_HPS):
        sl = slice(h * LANE, (h + 1) * LANE)
        streams.append((qs[h],
                        lambda j, sl=sl: _k_chunk(k_ref, j, sl),
                        lambda j, sl=sl: _vt_lower(_k_chunk(v_ref, j, sl)),
                        scratch[N_PIPE * h:N_PIPE * (h + 1)]))
    accs = _causal_attention(streams, i)
    outs = [_normalise(a) for a in accs]
    if A_OW > A_HPS * A_V:
        outs.append(jnp.zeros((A_OW - A_HPS * A_V, TQ), F32))
    o_ref[...] = jnp.concatenate(outs, axis=0).T.astype(BF16)


def _mla_call(qa, rope_tab, ka, va):
    return pl.pallas_call(
        _mla_kernel,
        grid=(BATCH, A_HEADS // A_HPS, NQ),
        in_specs=[_q_spec(A_HPS), _ROPE_SPEC, _kv_spec(A_HPS), _kv_spec(A_HPS)],
        out_specs=pl.BlockSpec((None, TQ, A_OW), lambda b, h, i: (b, i, h)),
        out_shape=jax.ShapeDtypeStruct((BATCH, SEQ, WO_B), BF16),
        scratch_shapes=_pipe_scratch(TQ) * A_HPS,
        compiler_params=_attn_params(),
        name="mla_attn",
    )(qa, rope_tab, ka, va)


C_HPS = 4


def _diff_kernel(q_ref, rope_ref, kv_ref, lqk_ref, sn_ref, o_ref, *scratch, lam_init):
    i = pl.program_id(2)
    lane = lax.broadcasted_iota(jnp.int32, (TQ, LANE), 1)
    qs = _rope_q(q_ref, rope_ref, 2)
    streams = []
    for h in range(C_HPS):
        sl = slice(h * LANE, (h + 1) * LANE)
        q = qs[h]
        zero = jnp.zeros_like(q)
        q2 = jnp.concatenate([jnp.where(lane < C_DH, q, zero),
                              jnp.where((lane >= C_DH) & (lane < 2 * C_DH), q, zero)], axis=0)
        streams.append((q2,
                        lambda j, sl=sl: _k_chunk(kv_ref, j, sl),
                        lambda j, sl=sl: _vt_upper(_k_chunk(kv_ref, j, sl)),
                        scratch[N_PIPE * h:N_PIPE * (h + 1)]))
    accs = _causal_attention(streams, i)
    lqk = lqk_ref[...]
    lam = (jnp.exp(jnp.sum(lqk[0:1] * lqk[1:2], axis=-1, keepdims=True))
           - jnp.exp(jnp.sum(lqk[2:3] * lqk[3:4], axis=-1, keepdims=True)) + lam_init)
    diffs = []
    for a in accs:
        o = _normalise(a)
        diffs.append(o[:, 0:TQ] - lam * o[:, TQ:2 * TQ])
    d = jnp.concatenate(diffs, axis=0).T
    d2 = d * d
    hl = lax.broadcasted_iota(jnp.int32, d.shape, 1) // (2 * C_DH)
    ms = jnp.zeros_like(d)
    for h in range(C_HPS):
        ms_h = jnp.sum(jnp.where(hl == h, d2, 0.0), axis=-1, keepdims=True)
        ms = jnp.where(hl == h, ms_h, ms)
    y = d * lax.rsqrt(ms * (1.0 / (2 * C_DH)) + EPS) * sn_ref[...] * (1.0 - lam_init)
    o_ref[...] = y.astype(BF16)


def _diff_call(l, qc, rope_tab, kd, lqk, sn):
    lam_init = 0.8 - 0.6 * math.exp(-0.3 * l)
    return pl.pallas_call(
        functools.partial(_diff_kernel, lam_init=lam_init),
        grid=(BATCH, C_HEADS // C_HPS, NQ),
        in_specs=[_q_spec(C_HPS), _ROPE_SPEC, _kv_spec(C_HPS),
                  _layer_spec((8, LANE), l), _layer_spec((1, C_HPS * 2 * C_DH), l)],
        out_specs=pl.BlockSpec((None, TQ, C_HPS * 2 * C_DH), lambda b, h, i: (b, i, h)),
        out_shape=jax.ShapeDtypeStruct((BATCH, SEQ, C_HEADS * 2 * C_DH), BF16),
        scratch_shapes=_pipe_scratch(2 * TQ) * C_HPS,
        compiler_params=_attn_params(),
        name="diff_attn",
    )(qc, rope_tab, kd, lqk, sn)


B_GPS = 2


def _nsa_kernel(q_ref, rope_ref, kvc_ref, ks_ref, vs_ref, kw_ref, gt_ref,
                pe_ref, w1_ref, w2k_ref, w2v_ref, ovt_ref,
                o_ref, kc_s, vct_s, *slc_scratch):
    i = pl.program_id(2)

    @pl.when(i == 0)
    def _():
        half = CMP_LEN // 2
        for g in range(B_GPS):
            yt = jnp.zeros((N_CMP_PAD, LANE), F32)
            yb = jnp.zeros((N_CMP_PAD, LANE), F32)
            for l in range(half):
                x = kvc_ref[g, pl.ds(l, N_CMP_PAD, stride=CMP_STRIDE), :]
                yt = yt + _dot((x + pe_ref[l:l + 1, :]).astype(BF16), w1_ref[l])
                yb = yb + _dot((x + pe_ref[half + l:half + l + 1, :]).astype(BF16),
                               w1_ref[half + l])
            pre = yt + pltpu.roll(yb, N_CMP_PAD - 1, 0)
            act = (pre * jax.nn.sigmoid(pre)).astype(BF16)
            kc_s[g] = _dot(act, w2k_ref[...]).astype(BF16)
            vct_s[g] = _dot(act, w2v_ref[...]).T[0:VROW].astype(BF16)

    qs = _rope_q(q_ref, rope_ref, 1)
    fronts = [_nsa_front(i, g, qs[g * B_HPG:(g + 1) * B_HPG], kw_ref, kc_s, vct_s, ovt_ref)
              for g in range(B_GPS)]

    streams = []
    for g in range(B_GPS):
        sl = slice(g * LANE, (g + 1) * LANE)
        streams.append((fronts[g][0],
                        lambda j, sl=sl: _k_chunk(ks_ref, j, sl),
                        lambda j, sl=sl: _vt_upper(_k_chunk(vs_ref, j, sl)),
                        slc_scratch[N_PIPE * g:N_PIPE * (g + 1)]))
    accs = _causal_attention(streams, i)

    for g in range(B_GPS):
        _, o_cmp, o_win = fronts[g]
        o_slc = _normalise(accs[g])
        gate = gt_ref[g * GATE_ROWS:(g + 1) * GATE_ROWS, :]
        outs = []
        for h in range(B_HPG):
            sl = slice(h * TQ, (h + 1) * TQ)
            outs.append(gate[3 * h:3 * h + 1] * o_cmp[:, sl]
                        + gate[3 * h + 1:3 * h + 2] * o_slc[:, sl]
                        + gate[3 * h + 2:3 * h + 3] * o_win[:, sl])
        base = g * 2 * LANE
        o_ref[:, base:base + LANE] = jnp.concatenate(outs[0:2], axis=0).T.astype(BF16)
        o_ref[:, base + LANE:base + 2 * LANE] = jnp.concatenate(
            [outs[2], jnp.zeros((VROW, TQ), F32)], axis=0).T.astype(BF16)


def _nsa_front(i, g, q_heads, kw_ref, kc_s, vct_s, ovt_ref):
    r3 = B_HPG * TQ
    gl = slice(g * LANE, (g + 1) * LANE)
    q3 = jnp.concatenate(q_heads, axis=0)

    kk, tt = _tile_pos(r3)
    c0 = jnp.maximum(i - 1, 0)
    kv_diag = _k_chunk(kw_ref, i, gl)
    kv_prev = _k_chunk(kw_ref, c0, gl)
    s_diag = jnp.where(kk <= tt, _dot_nt(kv_diag, q3), MASKED)
    s_prev = jnp.where(tt + jnp.where(i >= 1, 0, TQ) < kk, _dot_nt(kv_prev, q3), MASKED)
    m_win = jnp.maximum(jnp.max(s_diag, axis=0, keepdims=True),
                        jnp.max(s_prev, axis=0, keepdims=True))
    o_win = _normalise(_dot(_vt_upper(kv_diag), jnp.exp2(s_diag - m_win).astype(BF16))
                       + _dot(_vt_upper(kv_prev), jnp.exp2(s_prev - m_win).astype(BF16)))

    s = _dot_nt(kc_s[g], q3)
    c_end = lax.broadcasted_iota(jnp.int32, (N_CMP_PAD, 1), 0) * CMP_STRIDE + (CMP_LEN - 1)
    valid = c_end <= i * TQ + tt
    smax = jnp.max(jnp.where(valid, s, MASKED), axis=0, keepdims=True)
    e = jnp.where(valid, jnp.exp2(s - smax), 0.0)
    den = jnp.sum(e, axis=0, keepdims=True)
    p = e / jnp.where(den > 0.0, den, 1.0)
    o_cmp = _dot(vct_s[g], p.astype(BF16))

    psum = p[:, 0:TQ] + p[:, TQ:2 * TQ] + p[:, 2 * TQ:3 * TQ]
    p_hi = psum.astype(BF16)
    p_lo = (psum - p_hi.astype(F32)).astype(BF16)
    ovt = ovt_ref[...]
    imp = (_dot(ovt, p_hi) + _dot(ovt, p_lo))[64:128]
    m_idx = lax.broadcasted_iota(jnp.int32, (N_SLC, TQ), 0)
    cur = (i * TQ + lax.broadcasted_iota(jnp.int32, (N_SLC, TQ), 1)) // SLC_LEN
    forced = (m_idx == 0) | (m_idx == cur) | (m_idx == cur - 1)
    score = jnp.where(forced, 1e9, jnp.where(m_idx <= cur, imp, -1.0))
    sub = lax.broadcasted_iota(jnp.int32, (SUBLANES, TQ), 0)
    bias = [jnp.zeros((64, TQ), F32)]
    for v in range(N_SLC // SUBLANES):
        tile = score[v * SUBLANES:(v + 1) * SUBLANES, :]
        rank = jnp.zeros((SUBLANES, TQ), jnp.int32)
        for mp in range(N_SLC):
            srow = score[mp:mp + 1, :]
            if mp < v * SUBLANES:
                ahead = srow >= tile
            elif mp >= (v + 1) * SUBLANES:
                ahead = srow > tile
            else:
                ahead = (srow > tile) | ((srow == tile) & (sub > mp - v * SUBLANES))
            rank = rank + jnp.where(ahead, 1, 0)
        bias.append(jnp.where(rank < SLC_TOPN, 0.0, -SEL_BIG))
    bias = jnp.concatenate(bias, axis=0).T.astype(BF16)
    q_sel = q3 + jnp.concatenate([bias] * B_HPG, axis=0)
    return q_sel, o_cmp, o_win


def _nsa_call(l, qb, rope_tab, kvc, ks, vs, kw, gt, pe, w1, w2k, w2v, ovt):
    r3 = B_HPG * TQ
    return pl.pallas_call(
        _nsa_kernel,
        grid=(BATCH, B_KV_GROUPS // B_GPS, NQ),
        in_specs=[_q_spec(B_GPS * B_HPG), _ROPE_SPEC,
                  pl.BlockSpec((B_GPS, None, SEQ, LANE), lambda b, g, i: (g, b, 0, 0),
                               pipeline_mode=pl.Buffered(1)),
                  _kv_spec(B_GPS, 1), _kv_spec(B_GPS, 1), _kv_spec(B_GPS, 1),
                  pl.BlockSpec((None, B_GPS * GATE_ROWS, TQ), lambda b, g, i: (b * NQ + i, g, 0)),
                  _layer_spec((CMP_LEN, LANE), l), _layer_spec((CMP_LEN, LANE, LANE), l),
                  _layer_spec((LANE, LANE), l), _layer_spec((LANE, LANE), l),
                  _const_spec((LANE, N_CMP_PAD))],
        out_specs=pl.BlockSpec((None, TQ, B_GPS * 2 * LANE), lambda b, g, i: (b, i, g)),
        out_shape=jax.ShapeDtypeStruct((BATCH, SEQ, B_KV_GROUPS * 2 * LANE), BF16),
        scratch_shapes=[pltpu.VMEM((B_GPS, N_CMP_PAD, LANE), BF16),
                        pltpu.VMEM((B_GPS, VROW, N_CMP_PAD), BF16)]
        + _pipe_scratch(r3) * B_GPS,
        compiler_params=_attn_params(),
        name="nsa_attn",
    )(qb, rope_tab, kvc, ks, vs, kw, gt, pe, w1, w2k, w2v, ovt)


def _expand_w_in(w):
    a = w[..., :A_COLS]
    b = w[..., A_COLS:A_COLS + B_COLS]
    c = w[..., A_COLS + B_COLS:]
    z = lambda n: jnp.zeros(w.shape[:-1] + (n,), w.dtype)
    gate = lambda g: [b[..., 1152 + g * 9:1161 + g * 9], z(GATE_ROWS - 9)]
    parts = [a[..., 0:192], a[..., 320:352]] + gate(0) + gate(1) + [a[..., 192:320]]
    for h in range(B_HEADS):
        parts += [b[..., h * 64:(h + 1) * 64], z(64)]
    for base in (384, 640, 896):
        for g in range(B_KV_GROUPS):
            parts += [b[..., base + g * 64:base + g * 64 + 64],
                      b[..., base + 128 + g * 64:base + 192 + g * 64]]

    def rot_mid(t, o):
        return [t[..., o + 8:o + 24], t[..., o:o + 8], t[..., o + 24:o + 32]]

    for h in range(C_HEADS):
        parts += rot_mid(c, h * 64) + rot_mid(c, h * 64 + 32) + [z(64)]
    for h in range(C_HEADS):
        parts += (rot_mid(c, 256 + h * 64) + rot_mid(c, 288 + h * 64)
                  + [c[..., 512 + h * 64:576 + h * 64]])
    return jnp.concatenate(parts, axis=-1).astype(BF16)


def _expand_w_out(w):
    z = lambda n: jnp.zeros((DEPTH, n, D_MODEL), w.dtype)
    parts = []
    aw = A_HPS * A_V
    for s in range(A_HEADS // A_HPS):
        parts.append(w[:, s * aw:(s + 1) * aw])
        if A_OW > aw:
            parts.append(z(A_OW - aw))
    b0 = A_HEADS * A_V
    gw = B_HPG * B_DH
    for g in range(B_KV_GROUPS):
        parts += [w[:, b0 + g * gw:b0 + (g + 1) * gw], z(2 * LANE - gw)]
    parts.append(w[:, b0 + B_KV_GROUPS * gw:])
    return jnp.concatenate(parts, axis=1).astype(BF16)


def _rope_table(positions):
    def inv(rot):
        return 1.0 / (ROPE_THETA ** (jnp.arange(0, rot, 2, dtype=F32) / rot))

    fa, fb, fc = inv(A_ROPE), inv(B_ROT), inv(C_ROT)
    z = lambda n: jnp.zeros((n,), F32)
    freq = jnp.concatenate([fb, fb, fc, fc, z(24), fc, fc, z(8), fa, fa, z(32)])
    sign = jnp.concatenate([-jnp.ones(8), jnp.ones(8), -jnp.ones(4), jnp.ones(4), z(24),
                            -jnp.ones(4), jnp.ones(4), z(8), -jnp.ones(16), jnp.ones(16),
                            z(32)]).astype(F32)
    ang = positions.astype(F32)[:, :, None] * freq
    return jnp.concatenate([jnp.cos(ang), sign * jnp.sin(ang)], axis=2)


def _overlap_t():
    c = jnp.arange(N_CMP_PAD)[None, :]
    m = jnp.arange(N_SLC)[:, None]
    ov = ((c * CMP_STRIDE < m * SLC_LEN + SLC_LEN)
          & (c * CMP_STRIDE + CMP_LEN - 1 >= m * SLC_LEN) & (c < N_CMP))
    return jnp.concatenate([jnp.zeros((64, N_CMP_PAD), BF16), ov.astype(BF16)], axis=0)


def kernel(x, positions, ffn1_norm, ffn1_wg, ffn1_wu, ffn1_wd, mix_norm, w_in, mla_q_norm, mla_kv_norm, mla_w_uq, mla_w_ukv, nsa_pe_k, nsa_pe_v, nsa_phi_k1, nsa_phi_k2, nsa_phi_v1, nsa_phi_v2, diff_lq1, diff_lk1, diff_lq2, diff_lk2, diff_sub_norm, w_out, ffn2_norm, ffn2_wg, ffn2_wu, ffn2_wd, final_norm):
    xf = x.reshape(N_TOK, D_MODEL)
    rope_tab = _rope_table(positions)
    ovt = _overlap_t()
    bf = lambda t: t.astype(BF16)
    row = lambda t: t.reshape(DEPTH, 1, t.shape[-1])
    pad_to = lambda t, axis, n: jnp.pad(
        t, [(0, n - t.shape[a]) if a == axis else (0, 0) for a in range(t.ndim)])

    ffn1 = (row(ffn1_norm), bf(ffn1_wg), bf(ffn1_wu), bf(ffn1_wd))
    ffn2 = (row(ffn2_norm), bf(ffn2_wg), bf(ffn2_wu), bf(ffn2_wd))
    w_e = _expand_w_in(w_in)
    w_o = _expand_w_out(w_out)
    wuq = pad_to(pad_to(mla_w_uq.reshape(DEPTH, A_Q_RANK, A_HEADS, A_NOPE + A_ROPE), 3, LANE), 1, 256)
    wuq = bf(wuq.reshape(DEPTH, 256, A_HEADS * LANE))
    wkv = mla_w_ukv.reshape(DEPTH, A_KV_RANK, A_HEADS, A_NOPE + A_V)
    wukv = bf(jnp.concatenate(
        [pad_to(wkv[..., :A_NOPE], 3, LANE).reshape(DEPTH, A_KV_RANK, A_HEADS * LANE),
         pad_to(wkv[..., A_NOPE:], 3, LANE).reshape(DEPTH, A_KV_RANK, A_HEADS * LANE)], axis=2))
    qn = pad_to(row(mla_q_norm), 2, 256)
    kvn = row(mla_kv_norm)

    k1 = nsa_phi_k1.reshape(DEPTH, CMP_LEN, B_DH, B_DH)
    v1 = nsa_phi_v1.reshape(DEPTH, CMP_LEN, B_DH, B_DH)
    w1 = bf(jnp.concatenate([pad_to(k1, 3, LANE), jnp.pad(v1, ((0, 0), (0, 0), (0, 0), (64, 0)))],
                            axis=2))
    pe = jnp.concatenate([nsa_pe_k, nsa_pe_v], axis=2)
    w2k = bf(pad_to(pad_to(nsa_phi_k2, 1, LANE), 2, LANE))
    w2v = bf(jnp.pad(nsa_phi_v2, ((0, 0), (64, 0), (0, 64))))
    lqk = pad_to(pad_to(jnp.stack([diff_lq1, diff_lk1, diff_lq2, diff_lk2], axis=1), 1, 8), 2, LANE)
    sn = jnp.tile(row(diff_sub_norm), (1, 1, C_HPS))

    b3 = lambda t: t.reshape(BATCH, SEQ, t.shape[-1])
    flat = lambda t: t.reshape(N_TOK, t.shape[-1])
    rope3 = rope_tab
    for l in range(DEPTH):
        xf = _ffn_call(l, xf, *ffn1)
        (qa, ka, va, qb, kvc, ks, vs, kw, gt, qc, kd) = _proj_call(
            l, xf, row(mix_norm), w_e, rope_tab, qn, kvn, wuq, wukv)
        oa = _mla_call(b3(qa), rope3, b3(ka), b3(va))
        ob = _nsa_call(l, b3(qb), rope3, kvc.reshape(B_KV_GROUPS, BATCH, SEQ, LANE),
                       b3(ks), b3(vs), b3(kw), gt, pe, w1, w2k, w2v, ovt)
        oc = _diff_call(l, b3(qc), rope3, b3(kd), lqk, sn)
        fg = final_norm.reshape(1, D_MODEL) if l == DEPTH - 1 else None
        xf = _ffn_call(l, xf, *ffn2, mix=(flat(oa), flat(ob), flat(oc), w_o), final_gain=fg)
    return xf.reshape(BATCH, SEQ, D_MODEL)
```

```python
import functools
import math

import jax
import jax.numpy as jnp
from jax import lax
from jax.experimental import pallas as pl
from jax.experimental.pallas import tpu as pltpu

F32 = jnp.float32
BF16 = jnp.bfloat16

D_MODEL = 1024
BATCH = 4
SEQ = 4096
DEPTH = 2
N_TOK = BATCH * SEQ
EPS = 1e-6
ROPE_THETA = 500000.0
D_FF = 2816

A_HEADS = 6
A_Q_RANK = 192
A_KV_RANK = 128
A_NOPE = 64
A_ROPE = 32
A_V = 64

B_HEADS = 6
B_KV_GROUPS = 2
B_HPG = 3
B_DH = 64
B_ROT = 16
CMP_LEN = 32
CMP_STRIDE = 16
N_CMP = (SEQ - CMP_LEN) // CMP_STRIDE + 1
N_CMP_PAD = SEQ // CMP_STRIDE
SLC_LEN = 64
N_SLC = SEQ // SLC_LEN
SLC_TOPN = 16
WINDOW = 512

C_HEADS = 4
C_DH = 32
C_ROT = 8

A_COLS = A_Q_RANK + A_KV_RANK + A_ROPE
B_COLS = B_HEADS * B_DH + 6 * B_KV_GROUPS * B_DH + 3 * B_HEADS
C_COLS = 3 * C_HEADS * 2 * C_DH

LANE = 128
SUBLANES = 8
VROW = 64
VT_ROWS = 72
MASKED = -1e30
SEL_BIG = 1e9
LOG2E = 1.0 / math.log(2.0)

TM = 512
TF = 256
NFC = D_FF // TF
TQ = 512
TK = 512
NQ = SEQ // TQ
NCH = N_TOK // TK
GATE_ROWS = 16

CE_A = 0
CE_BQ = 384
CE_KVC = 1152
CE_KSVS = 1408
CE_KWVW = 1664
CE_CQ = 1920
CE_CKV = 2432
CE = 2944

A_HPS = 6
A_OW = -(-A_HPS * A_V // LANE) * LANE

WO_B = A_HEADS // A_HPS * A_OW
WO_C = WO_B + B_KV_GROUPS * 2 * LANE
WO_ROWS = WO_C + C_HEADS * 2 * C_DH

VMEM_LIMIT = 56 * 1024 * 1024


def _const_spec(shape):
    nd = len(shape)
    return pl.BlockSpec(shape, lambda *_: (0,) * nd, pipeline_mode=pl.Buffered(1))


def _layer_spec(shape, l):
    nd = len(shape)
    return pl.BlockSpec((None,) + tuple(shape), lambda *_: (l,) + (0,) * nd,
                        pipeline_mode=pl.Buffered(1))


def _rms(x, g, n):
    ms = jnp.sum(x * x, axis=-1, keepdims=True) * (1.0 / n)
    return x * lax.rsqrt(ms + EPS) * g


def _dot(a, b):
    return jnp.dot(a, b, preferred_element_type=F32)


def _ffn_kernel(*refs, mix, final):
    it = iter(refs)
    x_ref = next(it)
    if mix:
        oa_ref, ob_ref, oc_ref, wo_ref = next(it), next(it), next(it), next(it)
    g_ref, wg_ref, wu_ref, wd_ref = next(it), next(it), next(it), next(it)
    if final:
        fg_ref = next(it)
    o_ref = next(it)
    acc_ref = next(it)

    x = x_ref[...]
    if mix:
        x = x + _dot(oa_ref[...], wo_ref[0:WO_B, :])
        x = x + _dot(ob_ref[...], wo_ref[WO_B:WO_C, :])
        x = x + _dot(oc_ref[...], wo_ref[WO_C:WO_ROWS, :])
    xn = _rms(x, g_ref[...], D_MODEL).astype(BF16)
    for c in range(NFC):
        g = _dot(xn, wg_ref[:, c * TF:(c + 1) * TF])
        u = _dot(xn, wu_ref[:, c * TF:(c + 1) * TF])
        h = (g * jax.nn.sigmoid(g) * u).astype(BF16)
        d = _dot(h, wd_ref[c * TF:(c + 1) * TF, :])
        if c == 0:
            acc_ref[...] = d
        else:
            acc_ref[...] += d
    y = x + 0.5 * acc_ref[...]
    if final:
        y = _rms(y, fg_ref[...], D_MODEL)
    o_ref[...] = y


def _ffn_call(l, x, gain, wg, wu, wd, mix=None, final_gain=None):
    tok = lambda w: pl.BlockSpec((TM, w), lambda i: (i, 0))
    args = [x]
    specs = [tok(D_MODEL)]
    if mix is not None:
        oa, ob, oc, wo = mix
        args += [oa, ob, oc, wo]
        specs += [tok(WO_B), tok(WO_C - WO_B), tok(WO_ROWS - WO_C),
                  _layer_spec((WO_ROWS, D_MODEL), l)]
    args += [gain, wg, wu, wd]
    specs += [_layer_spec((1, D_MODEL), l), _layer_spec((D_MODEL, D_FF), l),
              _layer_spec((D_MODEL, D_FF), l), _layer_spec((D_FF, D_MODEL), l)]
    if final_gain is not None:
        args.append(final_gain)
        specs.append(_const_spec((1, D_MODEL)))
    kern = functools.partial(_ffn_kernel, mix=mix is not None, final=final_gain is not None)
    return pl.pallas_call(
        kern,
        grid=(N_TOK // TM,),
        in_specs=specs,
        out_specs=tok(D_MODEL),
        out_shape=jax.ShapeDtypeStruct((N_TOK, D_MODEL), F32),
        scratch_shapes=[pltpu.VMEM((TM, D_MODEL), F32)],
        compiler_params=pltpu.CompilerParams(
            dimension_semantics=("arbitrary",), vmem_limit_bytes=VMEM_LIMIT),
        name="ffn",
    )(*args)


ROPE_HALF = (16, 8, 4)


def _rope_fn(table):
    lane = lax.broadcasted_iota(jnp.int32, (table.shape[0], LANE), 1)
    between = lambda a, b: (lane >= a) & (lane < b)
    rot = (between(64, 96), lane < 16, between(16, 24) | between(48, 56))
    second = (between(80, 96), between(8, 16), between(20, 24) | between(52, 56))
    cache = {}

    def rope(v, typ):
        if typ not in cache:
            cache[typ] = (jnp.where(rot[typ], table[:, 0:LANE], 1.0),
                          jnp.where(rot[typ], table[:, LANE:2 * LANE], 0.0))
        cos, sin = cache[typ]
        r = ROPE_HALF[typ]
        partner = jnp.where(second[typ], pltpu.roll(v, r, 1), pltpu.roll(v, LANE - r, 1))
        return v * cos + partner * sin

    return rope


def _rope_q(q_ref, rope_ref, typ):
    rope = _rope_fn(rope_ref[...])
    nb = q_ref.shape[1] // LANE
    return [rope(q_ref[:, b * LANE:(b + 1) * LANE].astype(F32), typ).astype(BF16)
            for b in range(nb)]


def _proj_kernel(x_ref, g_ref, w_ref, rope_ref, qn_ref, kvn_ref, wuq_ref, wukv_ref,
                 qa_ref, ka_ref, va_ref, qb_ref, kvc_ref, ks_ref, vs_ref, kw_ref,
                 gt_ref, qc_ref, kd_ref):
    xn = _rms(x_ref[...], g_ref[...], D_MODEL).astype(BF16)
    lane = lax.broadcasted_iota(jnp.int32, (TM, LANE), 1)
    lo = lane < 64
    rope = _rope_fn(rope_ref[...])
    ones_row = (lane == VROW).astype(F32)

    def blk(v, b):
        return v[:, b * LANE:(b + 1) * LANE]

    ha = _dot(xn, w_ref[:, CE_A:CE_A + 384])
    a1 = blk(ha, 1)
    cq = jnp.concatenate([blk(ha, 0), jnp.where(lo, a1, 0.0)], axis=1)
    cqn = _rms(cq, qn_ref[...], A_Q_RANK).astype(BF16)
    ckvn = _rms(blk(ha, 2), kvn_ref[...], A_KV_RANK).astype(BF16)
    kpe = jnp.where((lane >= 64) & (lane < 96), rope(a1, 0), 0.0)
    gt_ref[0] = jax.nn.sigmoid(a1).T[96:128, :]
    q = _dot(cqn, wuq_ref[...])
    kv = _dot(ckvn, wukv_ref[...])
    qa_ref[...] = (q * ((A_NOPE + A_ROPE) ** -0.5 * LOG2E)).astype(BF16)
    for h in range(A_HEADS):
        ka_ref[:, h * LANE:(h + 1) * LANE] = (blk(kv, h) + kpe).astype(BF16)
        va_ref[:, h * LANE:(h + 1) * LANE] = (blk(kv, A_HEADS + h) + ones_row).astype(BF16)

    qb_ref[...] = (_dot(xn, w_ref[:, CE_BQ:CE_BQ + 768]) * (B_DH ** -0.5 * LOG2E)).astype(BF16)
    hk = _dot(xn, w_ref[:, CE_KVC:CE_KVC + 256])
    for g in range(B_KV_GROUPS):
        kvc_ref[g] = rope(blk(hk, g), 1)
    tpos = (pl.program_id(0) % (SEQ // TM)) * TM + lax.broadcasted_iota(jnp.int32, (TM, LANE), 0)
    onehot = ((lane - 64) == (tpos // SLC_LEN)).astype(F32)
    hk = _dot(xn, w_ref[:, CE_KSVS:CE_KSVS + 256])
    vs_ref[...] = hk.astype(BF16)
    for g in range(B_KV_GROUPS):
        ks_ref[:, g * LANE:(g + 1) * LANE] = jnp.where(lo, rope(blk(hk, g), 1), onehot).astype(BF16)
    hk = _dot(xn, w_ref[:, CE_KWVW:CE_KWVW + 256])
    for g in range(B_KV_GROUPS):
        kw_ref[:, g * LANE:(g + 1) * LANE] = rope(blk(hk, g), 1).astype(BF16)

    qc_ref[...] = (_dot(xn, w_ref[:, CE_CQ:CE_CQ + 512]) * (C_DH ** -0.5 * LOG2E)).astype(BF16)
    hc = _dot(xn, w_ref[:, CE_CKV:CE_CKV + 512])
    for h in range(C_HEADS):
        kd_ref[:, h * LANE:(h + 1) * LANE] = rope(blk(hc, h), 2).astype(BF16)


def _proj_call(l, x, gain, w_e, rope_tab, qn, kvn, wuq, wukv):
    tok = lambda w: pl.BlockSpec((TM, w), lambda i: (i, 0))
    bf = lambda w: jax.ShapeDtypeStruct((N_TOK, w), BF16)
    return pl.pallas_call(
        _proj_kernel,
        grid=(N_TOK // TM,),
        in_specs=[tok(D_MODEL), _layer_spec((1, D_MODEL), l), _layer_spec((D_MODEL, CE), l),
                  pl.BlockSpec((None, TM, 2 * LANE), lambda i: (i // (SEQ // TM), i % (SEQ // TM), 0)),
                  _layer_spec((1, 256), l), _layer_spec((1, LANE), l),
                  _layer_spec((256, 768), l), _layer_spec((LANE, 1536), l)],
        out_specs=[tok(768), tok(768), tok(768), tok(768),
                   pl.BlockSpec((B_KV_GROUPS, TM, LANE), lambda i: (0, i, 0)),
                   tok(256), tok(256), tok(256),
                   pl.BlockSpec((1, B_KV_GROUPS * GATE_ROWS, TM), lambda i: (i, 0, 0)),
                   tok(512), tok(512)],
        out_shape=[bf(768), bf(768), bf(768), bf(768),
                   jax.ShapeDtypeStruct((B_KV_GROUPS, N_TOK, LANE), F32),
                   bf(256), bf(256), bf(256),
                   jax.ShapeDtypeStruct((NCH, B_KV_GROUPS * GATE_ROWS, TM), F32),
                   bf(512), bf(512)],
        compiler_params=pltpu.CompilerParams(
            dimension_semantics=("arbitrary",), vmem_limit_bytes=VMEM_LIMIT),
        name="proj",
    )(x, gain, w_e, rope_tab, qn, kvn, wuq, wukv)


def _tile_pos(r):
    kk = lax.broadcasted_iota(jnp.int32, (TK, 1), 0)
    tt = lax.broadcasted_iota(jnp.int32, (1, r), 1) & (TQ - 1)
    return kk, tt


def _dot_nt(a, b):
    return lax.dot_general(a, b, (((1,), (1,)), ((), ())), preferred_element_type=F32)


def _score_stage(q, k, mask, m_prev, s_ref, m_ref):
    s = _dot_nt(k, q)
    if mask is not None:
        s = jnp.where(mask, s, MASKED)
    cm = jnp.max(s, axis=0, keepdims=True)
    m_ref[...] = cm if m_prev is None else jnp.maximum(m_prev[...], cm)
    s_ref[...] = s


def _value_stage(vt, s_ref, m_ref, mb_ref, acc_ref):
    m = m_ref[...]
    alpha = jnp.exp2(mb_ref[...] - m)
    p = jnp.exp2(s_ref[...] - m).astype(BF16)
    mb_ref[...] = m
    acc_ref[...] = alpha * acc_ref[...] + _dot(vt, p)


def _causal_attention(streams, i):
    for qt, k_of, vt_of, (s0, s1, m0, m1, mb, acc) in streams:
        kk, tt = _tile_pos(qt.shape[0])
        _score_stage(qt, k_of(i), kk <= tt, None, s0, m0)
        mb[...] = m0[...]
        acc[...] = jnp.zeros_like(acc)

    def step(score_j, pend_j, a, b):
        for qt, k_of, vt_of, sc in streams:
            bufs, maxs, mb, acc = (sc[0], sc[1]), (sc[2], sc[3]), sc[4], sc[5]
            _score_stage(qt, k_of(score_j), None, maxs[a], bufs[b], maxs[b])
            _value_stage(vt_of(pend_j), bufs[a], maxs[a], mb, acc)

    def fold(pend_j, a):
        for qt, k_of, vt_of, sc in streams:
            _value_stage(vt_of(pend_j), sc[a], sc[2 + a], sc[4], sc[5])

    def body(jj, carry):
        step(2 * jj, jnp.where(jj == 0, i, 2 * jj - 1), 0, 1)
        step(2 * jj + 1, 2 * jj, 1, 0)
        return carry

    npairs = i // 2
    lax.fori_loop(0, npairs, body, 0)
    pend = jnp.where(npairs == 0, i, 2 * npairs - 1)

    @pl.when(i % 2 == 1)
    def _():
        step(i - 1, pend, 0, 1)
        fold(i - 1, 1)

    @pl.when(i % 2 == 0)
    def _():
        fold(pend, 0)

    return [st[3][5][...] for st in streams]


def _normalise(acc):
    return acc[0:VROW] / acc[VROW:VROW + 1, :]


def _pipe_scratch(r):
    return ([pltpu.VMEM((TK, r), F32)] * 2 + [pltpu.VMEM((1, r), F32)] * 3
            + [pltpu.VMEM((VT_ROWS, r), F32)])


N_PIPE = 6


def _k_chunk(k_ref, j, sl=slice(None)):
    return k_ref[pl.ds(pl.multiple_of(j * TK, TK), TK), sl]


def _q_spec(nb):
    return pl.BlockSpec((None, TQ, nb * LANE), lambda b, h, i: (b, i, h))


def _kv_spec(nb, buffers=None):
    mode = {} if buffers is None else {"pipeline_mode": pl.Buffered(buffers)}
    return pl.BlockSpec((None, SEQ, nb * LANE), lambda b, h, i: (b, 0, h), **mode)


_ROPE_SPEC = pl.BlockSpec((None, TQ, 2 * LANE), lambda b, h, i: (b, i, 0))


def _vt_lower(chunk):
    return chunk.T[0:VT_ROWS]


def _vt_upper(chunk):
    ones = (lax.broadcasted_iota(jnp.int32, (VT_ROWS - VROW, chunk.shape[0]), 0) == 0).astype(BF16)
    return jnp.concatenate([chunk.T[VROW:2 * VROW], ones], axis=0)


def _attn_params():
    return pltpu.CompilerParams(dimension_semantics=("arbitrary",) * 3,
                                vmem_limit_bytes=VMEM_LIMIT)


def _mla_kernel(q_ref, rope_ref, k_ref, v_ref, o_ref, *scratch):
    i = pl.program_id(2)
    qs = _rope_q(q_ref, rope_ref, 0)
    streams = []
    for h in range(A_HPS):
        sl = slice(h * LANE, (h + 1) * LANE)
        streams.append((qs[h],
                        lambda j, sl=sl: _k_chunk(k_ref, j, sl),
                        lambda j, sl=sl: _vt_lower(_k_chunk(v_ref, j, sl)),
                        scratch[N_PIPE * h:N_PIPE * (h + 1)]))
    accs = _causal_attention(streams, i)
    outs = [_normalise(a) for a in accs]
    if A_OW > A_HPS * A_V:
        outs.append(jnp.zeros((A_OW - A_HPS * A_V, TQ), F32))
    o_ref[...] = jnp.concatenate(outs, axis=0).T.astype(BF16)


def _mla_call(qa, rope_tab, ka, va):
    return pl.pallas_call(
        _mla_kernel,
        grid=(BATCH, A_HEADS // A_HPS, NQ),
        in_specs=[_q_spec(A_HPS), _ROPE_SPEC, _kv_spec(A_HPS), _kv_spec(A_HPS)],
        out_specs=pl.BlockSpec((None, TQ, A_OW), lambda b, h, i: (b, i, h)),
        out_shape=jax.ShapeDtypeStruct((BATCH, SEQ, WO_B), BF16),
        scratch_shapes=_pipe_scratch(TQ) * A_HPS,
        compiler_params=_attn_params(),
        name="mla_attn",
    )(qa, rope_tab, ka, va)


C_HPS = 4


def _diff_kernel(q_ref, rope_ref, kv_ref, lqk_ref, sn_ref, o_ref, *scratch, lam_init):
    i = pl.program_id(2)
    lane = lax.broadcasted_iota(jnp.int32, (TQ, LANE), 1)
    qs = _rope_q(q_ref, rope_ref, 2)
    streams = []
    for h in range(C_HPS):
        sl = slice(h * LANE, (h + 1) * LANE)
        q = qs[h]
        zero = jnp.zeros_like(q)
        q2 = jnp.concatenate([jnp.where(lane < C_DH, q, zero),
                              jnp.where((lane >= C_DH) & (lane < 2 * C_DH), q, zero)], axis=0)
        streams.append((q2,
                        lambda j, sl=sl: _k_chunk(kv_ref, j, sl),
                        lambda j, sl=sl: _vt_upper(_k_chunk(kv_ref, j, sl)),
                        scratch[N_PIPE * h:N_PIPE * (h + 1)]))
    accs = _causal_attention(streams, i)
    lqk = lqk_ref[...]
    lam = (jnp.exp(jnp.sum(lqk[0:1] * lqk[1:2], axis=-1, keepdims=True))
           - jnp.exp(jnp.sum(lqk[2:3] * lqk[3:4], axis=-1, keepdims=True)) + lam_init)
    diffs = []
    for a in accs:
        o = _normalise(a)
        diffs.append(o[:, 0:TQ] - lam * o[:, TQ:2 * TQ])
    d = jnp.concatenate(diffs, axis=0).T
    d2 = d * d
    hl = lax.broadcasted_iota(jnp.int32, d.shape, 1) // (2 * C_DH)
    ms = jnp.zeros_like(d)
    for h in range(C_HPS):
        ms_h = jnp.sum(jnp.where(hl == h, d2, 0.0), axis=-1, keepdims=True)
        ms = jnp.where(hl == h, ms_h, ms)
    y = d * lax.rsqrt(ms * (1.0 / (2 * C_DH)) + EPS) * sn_ref[...] * (1.0 - lam_init)
    o_ref[...] = y.astype(BF16)


def _diff_call(l, qc, rope_tab, kd, lqk, sn):
    lam_init = 0.8 - 0.6 * math.exp(-0.3 * l)
    return pl.pallas_call(
        functools.partial(_diff_kernel, lam_init=lam_init),
        grid=(BATCH, C_HEADS // C_HPS, NQ),
        in_specs=[_q_spec(C_HPS), _ROPE_SPEC, _kv_spec(C_HPS),
                  _layer_spec((8, LANE), l), _layer_spec((1, C_HPS * 2 * C_DH), l)],
        out_specs=pl.BlockSpec((None, TQ, C_HPS * 2 * C_DH), lambda b, h, i: (b, i, h)),
        out_shape=jax.ShapeDtypeStruct((BATCH, SEQ, C_HEADS * 2 * C_DH), BF16),
        scratch_shapes=_pipe_scratch(2 * TQ) * C_HPS,
        compiler_params=_attn_params(),
        name="diff_attn",
    )(qc, rope_tab, kd, lqk, sn)


B_GPS = 2


def _nsa_kernel(q_ref, rope_ref, kvc_ref, ks_ref, vs_ref, kw_ref, gt_ref,
                pe_ref, w1_ref, w2k_ref, w2v_ref, ovt_ref,
                o_ref, kc_s, vct_s, *slc_scratch):
    i = pl.program_id(2)

    @pl.when(i == 0)
    def _():
        half = CMP_LEN // 2
        for g in range(B_GPS):
            yt = jnp.zeros((N_CMP_PAD, LANE), F32)
            yb = jnp.zeros((N_CMP_PAD, LANE), F32)
            for l in range(half):
                x = kvc_ref[g, pl.ds(l, N_CMP_PAD, stride=CMP_STRIDE), :]
                yt = yt + _dot((x + pe_ref[l:l + 1, :]).astype(BF16), w1_ref[l])
                yb = yb + _dot((x + pe_ref[half + l:half + l + 1, :]).astype(BF16),
                               w1_ref[half + l])
            pre = yt + pltpu.roll(yb, N_CMP_PAD - 1, 0)
            act = (pre * jax.nn.sigmoid(pre)).astype(BF16)
            kc_s[g] = _dot(act, w2k_ref[...]).astype(BF16)
            vct_s[g] = _dot(act, w2v_ref[...]).T[0:VROW].astype(BF16)

    qs = _rope_q(q_ref, rope_ref, 1)
    fronts = [_nsa_front(i, g, qs[g * B_HPG:(g + 1) * B_HPG], kw_ref, kc_s, vct_s, ovt_ref)
              for g in range(B_GPS)]

    streams = []
    for g in range(B_GPS):
        sl = slice(g * LANE, (g + 1) * LANE)
        streams.append((fronts[g][0],
                        lambda j, sl=sl: _k_chunk(ks_ref, j, sl),
                        lambda j, sl=sl: _vt_upper(_k_chunk(vs_ref, j, sl)),
                        slc_scratch[N_PIPE * g:N_PIPE * (g + 1)]))
    accs = _causal_attention(streams, i)

    for g in range(B_GPS):
        _, o_cmp, o_win = fronts[g]
        o_slc = _normalise(accs[g])
        gate = gt_ref[g * GATE_ROWS:(g + 1) * GATE_ROWS, :]
        outs = []
        for h in range(B_HPG):
            sl = slice(h * TQ, (h + 1) * TQ)
            outs.append(gate[3 * h:3 * h + 1] * o_cmp[:, sl]
                        + gate[3 * h + 1:3 * h + 2] * o_slc[:, sl]
                        + gate[3 * h + 2:3 * h + 3] * o_win[:, sl])
        base = g * 2 * LANE
        o_ref[:, base:base + LANE] = jnp.concatenate(outs[0:2], axis=0).T.astype(BF16)
        o_ref[:, base + LANE:base + 2 * LANE] = jnp.concatenate(
            [outs[2], jnp.zeros((VROW, TQ), F32)], axis=0).T.astype(BF16)


def _nsa_front(i, g, q_heads, kw_ref, kc_s, vct_s, ovt_ref):
    r3 = B_HPG * TQ
    gl = slice(g * LANE, (g + 1) * LANE)
    q3 = jnp.concatenate(q_heads, axis=0)

    half = TQ // 2
    kq = lax.broadcasted_iota(jnp.int32, (half, 1), 0)
    tq = lax.broadcasted_iota(jnp.int32, (1, B_HPG * half), 1) & (half - 1)
    no_prev = jnp.where(i >= 1, 0, half)
    o_halves = []
    for hf in range(2):
        qh = jnp.concatenate([q[hf * half:(hf + 1) * half] for q in q_heads], axis=0)
        subs = [kw_ref[pl.ds(pl.multiple_of(jnp.maximum(2 * i - 2 + hf + d, 0) * half, half), half),
                       gl] for d in range(3)]
        s0 = jnp.where(tq + no_prev < kq, _dot_nt(subs[0], qh), MASKED)
        s1 = _dot_nt(subs[1], qh)
        if hf == 0:
            s1 = s1 + jnp.where(i >= 1, 0.0, MASKED)
        s2 = jnp.where(kq <= tq, _dot_nt(subs[2], qh), MASKED)
        scores = (s0, s1, s2)
        m_win = functools.reduce(jnp.maximum, [jnp.max(s, axis=0, keepdims=True) for s in scores])
        acc = sum(_dot(_vt_upper(k), jnp.exp2(s - m_win).astype(BF16))
                  for k, s in zip(subs, scores))
        o_halves.append(_normalise(acc))
    o_win = jnp.concatenate([o_halves[hf][:, h * half:(h + 1) * half]
                             for h in range(B_HPG) for hf in range(2)], axis=1)

    kk, tt = _tile_pos(r3)
    s = _dot_nt(kc_s[g], q3)
    c_end = lax.broadcasted_iota(jnp.int32, (N_CMP_PAD, 1), 0) * CMP_STRIDE + (CMP_LEN - 1)
    valid = c_end <= i * TQ + tt
    smax = jnp.max(jnp.where(valid, s, MASKED), axis=0, keepdims=True)
    e = jnp.where(valid, jnp.exp2(s - smax), 0.0)
    den = jnp.sum(e, axis=0, keepdims=True)
    p = e / jnp.where(den > 0.0, den, 1.0)
    o_cmp = _dot(vct_s[g], p.astype(BF16))

    psum = p[:, 0:TQ] + p[:, TQ:2 * TQ] + p[:, 2 * TQ:3 * TQ]
    p_hi = psum.astype(BF16)
    p_lo = (psum - p_hi.astype(F32)).astype(BF16)
    ovt = ovt_ref[...]
    imp = (_dot(ovt, p_hi) + _dot(ovt, p_lo))[64:128]
    m_idx = lax.broadcasted_iota(jnp.int32, (N_SLC, TQ), 0)
    cur = (i * TQ + lax.broadcasted_iota(jnp.int32, (N_SLC, TQ), 1)) // SLC_LEN
    forced = (m_idx == 0) | (m_idx == cur) | (m_idx == cur - 1)
    score = jnp.where(forced, 1e9, jnp.where(m_idx <= cur, imp, -1.0))
    sub = lax.broadcasted_iota(jnp.int32, (SUBLANES, TQ), 0)
    bias = [jnp.zeros((64, TQ), F32)]
    for v in range(N_SLC // SUBLANES):
        tile = score[v * SUBLANES:(v + 1) * SUBLANES, :]
        rank = jnp.zeros((SUBLANES, TQ), jnp.int32)
        for mp in range(N_SLC):
            srow = score[mp:mp + 1, :]
            if mp < v * SUBLANES:
                ahead = srow >= tile
            elif mp >= (v + 1) * SUBLANES:
                ahead = srow > tile
            else:
                ahead = (srow > tile) | ((srow == tile) & (sub > mp - v * SUBLANES))
            rank = rank + jnp.where(ahead, 1, 0)
        bias.append(jnp.where(rank < SLC_TOPN, 0.0, -SEL_BIG))
    bias = jnp.concatenate(bias, axis=0).T.astype(BF16)
    q_sel = q3 + jnp.concatenate([bias] * B_HPG, axis=0)
    return q_sel, o_cmp, o_win


def _nsa_call(l, qb, rope_tab, kvc, ks, vs, kw, gt, pe, w1, w2k, w2v, ovt):
    r3 = B_HPG * TQ
    return pl.pallas_call(
        _nsa_kernel,
        grid=(BATCH, B_KV_GROUPS // B_GPS, NQ),
        in_specs=[_q_spec(B_GPS * B_HPG), _ROPE_SPEC,
                  pl.BlockSpec((B_GPS, None, SEQ, LANE), lambda b, g, i: (g, b, 0, 0),
                               pipeline_mode=pl.Buffered(1)),
                  _kv_spec(B_GPS, 1), _kv_spec(B_GPS, 1), _kv_spec(B_GPS, 1),
                  pl.BlockSpec((None, B_GPS * GATE_ROWS, TQ), lambda b, g, i: (b * NQ + i, g, 0)),
                  _layer_spec((CMP_LEN, LANE), l), _layer_spec((CMP_LEN, LANE, LANE), l),
                  _layer_spec((LANE, LANE), l), _layer_spec((LANE, LANE), l),
                  _const_spec((LANE, N_CMP_PAD))],
        out_specs=pl.BlockSpec((None, TQ, B_GPS * 2 * LANE), lambda b, g, i: (b, i, g)),
        out_shape=jax.ShapeDtypeStruct((BATCH, SEQ, B_KV_GROUPS * 2 * LANE), BF16),
        scratch_shapes=[pltpu.VMEM((B_GPS, N_CMP_PAD, LANE), BF16),
                        pltpu.VMEM((B_GPS, VROW, N_CMP_PAD), BF16)]
        + _pipe_scratch(r3) * B_GPS,
        compiler_params=_attn_params(),
        name="nsa_attn",
    )(qb, rope_tab, kvc, ks, vs, kw, gt, pe, w1, w2k, w2v, ovt)


def _expand_w_in(w):
    a = w[..., :A_COLS]
    b = w[..., A_COLS:A_COLS + B_COLS]
    c = w[..., A_COLS + B_COLS:]
    z = lambda n: jnp.zeros(w.shape[:-1] + (n,), w.dtype)
    gate = lambda g: [b[..., 1152 + g * 9:1161 + g * 9], z(GATE_ROWS - 9)]
    parts = [a[..., 0:192], a[..., 320:352]] + gate(0) + gate(1) + [a[..., 192:320]]
    for h in range(B_HEADS):
        parts += [b[..., h * 64:(h + 1) * 64], z(64)]
    for base in (384, 640, 896):
        for g in range(B_KV_GROUPS):
            parts += [b[..., base + g * 64:base + g * 64 + 64],
                      b[..., base + 128 + g * 64:base + 192 + g * 64]]

    def rot_mid(t, o):
        return [t[..., o + 8:o + 24], t[..., o:o + 8], t[..., o + 24:o + 32]]

    for h in range(C_HEADS):
        parts += rot_mid(c, h * 64) + rot_mid(c, h * 64 + 32) + [z(64)]
    for h in range(C_HEADS):
        parts += (rot_mid(c, 256 + h * 64) + rot_mid(c, 288 + h * 64)
                  + [c[..., 512 + h * 64:576 + h * 64]])
    return jnp.concatenate(parts, axis=-1).astype(BF16)


def _expand_w_out(w):
    z = lambda n: jnp.zeros((DEPTH, n, D_MODEL), w.dtype)
    parts = []
    aw = A_HPS * A_V
    for s in range(A_HEADS // A_HPS):
        parts.append(w[:, s * aw:(s + 1) * aw])
        if A_OW > aw:
            parts.append(z(A_OW - aw))
    b0 = A_HEADS * A_V
    gw = B_HPG * B_DH
    for g in range(B_KV_GROUPS):
        parts += [w[:, b0 + g * gw:b0 + (g + 1) * gw], z(2 * LANE - gw)]
    parts.append(w[:, b0 + B_KV_GROUPS * gw:])
    return jnp.concatenate(parts, axis=1).astype(BF16)


def _rope_table(positions):
    def inv(rot):
        return 1.0 / (ROPE_THETA ** (jnp.arange(0, rot, 2, dtype=F32) / rot))

    fa, fb, fc = inv(A_ROPE), inv(B_ROT), inv(C_ROT)
    z = lambda n: jnp.zeros((n,), F32)
    freq = jnp.concatenate([fb, fb, fc, fc, z(24), fc, fc, z(8), fa, fa, z(32)])
    sign = jnp.concatenate([-jnp.ones(8), jnp.ones(8), -jnp.ones(4), jnp.ones(4), z(24),
                            -jnp.ones(4), jnp.ones(4), z(8), -jnp.ones(16), jnp.ones(16),
                            z(32)]).astype(F32)
    ang = positions.astype(F32)[:, :, None] * freq
    return jnp.concatenate([jnp.cos(ang), sign * jnp.sin(ang)], axis=2)


def _overlap_t():
    c = jnp.arange(N_CMP_PAD)[None, :]
    m = jnp.arange(N_SLC)[:, None]
    ov = ((c * CMP_STRIDE < m * SLC_LEN + SLC_LEN)
          & (c * CMP_STRIDE + CMP_LEN - 1 >= m * SLC_LEN) & (c < N_CMP))
    return jnp.concatenate([jnp.zeros((64, N_CMP_PAD), BF16), ov.astype(BF16)], axis=0)


def kernel(x, positions, ffn1_norm, ffn1_wg, ffn1_wu, ffn1_wd, mix_norm, w_in, mla_q_norm, mla_kv_norm, mla_w_uq, mla_w_ukv, nsa_pe_k, nsa_pe_v, nsa_phi_k1, nsa_phi_k2, nsa_phi_v1, nsa_phi_v2, diff_lq1, diff_lk1, diff_lq2, diff_lk2, diff_sub_norm, w_out, ffn2_norm, ffn2_wg, ffn2_wu, ffn2_wd, final_norm):
    xf = x.reshape(N_TOK, D_MODEL)
    rope_tab = _rope_table(positions)
    ovt = _overlap_t()
    bf = lambda t: t.astype(BF16)
    row = lambda t: t.reshape(DEPTH, 1, t.shape[-1])
    pad_to = lambda t, axis, n: jnp.pad(
        t, [(0, n - t.shape[a]) if a == axis else (0, 0) for a in range(t.ndim)])

    ffn1 = (row(ffn1_norm), bf(ffn1_wg), bf(ffn1_wu), bf(ffn1_wd))
    ffn2 = (row(ffn2_norm), bf(ffn2_wg), bf(ffn2_wu), bf(ffn2_wd))
    w_e = _expand_w_in(w_in)
    w_o = _expand_w_out(w_out)
    wuq = pad_to(pad_to(mla_w_uq.reshape(DEPTH, A_Q_RANK, A_HEADS, A_NOPE + A_ROPE), 3, LANE), 1, 256)
    wuq = bf(wuq.reshape(DEPTH, 256, A_HEADS * LANE))
    wkv = mla_w_ukv.reshape(DEPTH, A_KV_RANK, A_HEADS, A_NOPE + A_V)
    wukv = bf(jnp.concatenate(
        [pad_to(wkv[..., :A_NOPE], 3, LANE).reshape(DEPTH, A_KV_RANK, A_HEADS * LANE),
         pad_to(wkv[..., A_NOPE:], 3, LANE).reshape(DEPTH, A_KV_RANK, A_HEADS * LANE)], axis=2))
    qn = pad_to(row(mla_q_norm), 2, 256)
    kvn = row(mla_kv_norm)

    k1 = nsa_phi_k1.reshape(DEPTH, CMP_LEN, B_DH, B_DH)
    v1 = nsa_phi_v1.reshape(DEPTH, CMP_LEN, B_DH, B_DH)
    w1 = bf(jnp.concatenate([pad_to(k1, 3, LANE), jnp.pad(v1, ((0, 0), (0, 0), (0, 0), (64, 0)))],
                            axis=2))
    pe = jnp.concatenate([nsa_pe_k, nsa_pe_v], axis=2)
    w2k = bf(pad_to(pad_to(nsa_phi_k2, 1, LANE), 2, LANE))
    w2v = bf(jnp.pad(nsa_phi_v2, ((0, 0), (64, 0), (0, 64))))
    lqk = pad_to(pad_to(jnp.stack([diff_lq1, diff_lk1, diff_lq2, diff_lk2], axis=1), 1, 8), 2, LANE)
    sn = jnp.tile(row(diff_sub_norm), (1, 1, C_HPS))

    b3 = lambda t: t.reshape(BATCH, SEQ, t.shape[-1])
    flat = lambda t: t.reshape(N_TOK, t.shape[-1])
    rope3 = rope_tab
    for l in range(DEPTH):
        xf = _ffn_call(l, xf, *ffn1)
        (qa, ka, va, qb, kvc, ks, vs, kw, gt, qc, kd) = _proj_call(
            l, xf, row(mix_norm), w_e, rope_tab, qn, kvn, wuq, wukv)
        oa = _mla_call(b3(qa), rope3, b3(ka), b3(va))
        ob = _nsa_call(l, b3(qb), rope3, kvc.reshape(B_KV_GROUPS, BATCH, SEQ, LANE),
                       b3(ks), b3(vs), b3(kw), gt, pe, w1, w2k, w2v, ovt)
        oc = _diff_call(l, b3(qc), rope3, b3(kd), lqk, sn)
        fg = final_norm.reshape(1, D_MODEL) if l == DEPTH - 1 else None
        xf = _ffn_call(l, xf, *ffn2, mix=(flat(oa), flat(ob), flat(oc), w_o), final_gain=fg)
    return xf.reshape(BATCH, SEQ, D_MODEL)
```

```python
import functools
import math

import jax
import jax.numpy as jnp
from jax import lax
from jax.experimental import pallas as pl
from jax.experimental.pallas import tpu as pltpu

F32 = jnp.float32
BF16 = jnp.bfloat16

D_MODEL = 1024
BATCH = 4
SEQ = 4096
DEPTH = 2
N_TOK = BATCH * SEQ
EPS = 1e-6
ROPE_THETA = 500000.0
D_FF = 2816

A_HEADS = 6
A_Q_RANK = 192
A_KV_RANK = 128
A_NOPE = 64
A_ROPE = 32
A_V = 64

B_HEADS = 6
B_KV_GROUPS = 2
B_HPG = 3
B_DH = 64
B_ROT = 16
CMP_LEN = 32
CMP_STRIDE = 16
N_CMP = (SEQ - CMP_LEN) // CMP_STRIDE + 1
N_CMP_PAD = SEQ // CMP_STRIDE
SLC_LEN = 64
N_SLC = SEQ // SLC_LEN
SLC_TOPN = 16
WINDOW = 512

C_HEADS = 4
C_DH = 32
C_ROT = 8

A_COLS = A_Q_RANK + A_KV_RANK + A_ROPE
B_COLS = B_HEADS * B_DH + 6 * B_KV_GROUPS * B_DH + 3 * B_HEADS
C_COLS = 3 * C_HEADS * 2 * C_DH

LANE = 128
SUBLANES = 8
VROW = 64
VT_ROWS = 72
MASKED = -1e30
SEL_BIG = 1e9
LOG2E = 1.0 / math.log(2.0)

TM = 512
TF = 256
NFC = D_FF // TF
TQ = 512
TK = 512
NQ = SEQ // TQ
NCH = N_TOK // TK
GATE_ROWS = 16

CE_A = 0
CE_BQ = 384
CE_KVC = 768
CE_KSVS = 1024
CE_KWVW = 1280
CE_CQ = 1536
CE_CKV = 1792
CE = 2304
QB_W = B_HEADS * B_DH
QC_W = C_HEADS * 2 * C_DH

A_HPS = 6
A_OW = -(-A_HPS * A_V // LANE) * LANE

WO_B = A_HEADS // A_HPS * A_OW
WO_C = WO_B + B_KV_GROUPS * 2 * LANE
WO_ROWS = WO_C + C_HEADS * 2 * C_DH

VMEM_LIMIT = 56 * 1024 * 1024


def _const_spec(shape):
    nd = len(shape)
    return pl.BlockSpec(shape, lambda *_: (0,) * nd, pipeline_mode=pl.Buffered(1))


def _layer_spec(shape, l):
    nd = len(shape)
    return pl.BlockSpec((None,) + tuple(shape), lambda *_: (l,) + (0,) * nd,
                        pipeline_mode=pl.Buffered(1))


def _rms(x, g, n):
    ms = jnp.sum(x * x, axis=-1, keepdims=True) * (1.0 / n)
    return x * lax.rsqrt(ms + EPS) * g


def _dot(a, b):
    return jnp.dot(a, b, preferred_element_type=F32)


def _ffn_kernel(*refs, mix, final):
    it = iter(refs)
    x_ref = next(it)
    if mix:
        oa_ref, ob_ref, oc_ref, wo_ref = next(it), next(it), next(it), next(it)
    g_ref, wg_ref, wu_ref, wd_ref = next(it), next(it), next(it), next(it)
    if final:
        fg_ref = next(it)
    o_ref = next(it)
    acc_ref = next(it)

    x = x_ref[...]
    if mix:
        x = x + _dot(oa_ref[...], wo_ref[0:WO_B, :])
        x = x + _dot(ob_ref[...], wo_ref[WO_B:WO_C, :])
        x = x + _dot(oc_ref[...], wo_ref[WO_C:WO_ROWS, :])
    xn = _rms(x, g_ref[...], D_MODEL).astype(BF16)
    for c in range(NFC):
        g = _dot(xn, wg_ref[:, c * TF:(c + 1) * TF])
        u = _dot(xn, wu_ref[:, c * TF:(c + 1) * TF])
        h = (g * jax.nn.sigmoid(g) * u).astype(BF16)
        d = _dot(h, wd_ref[c * TF:(c + 1) * TF, :])
        if c == 0:
            acc_ref[...] = d
        else:
            acc_ref[...] += d
    y = x + 0.5 * acc_ref[...]
    if final:
        y = _rms(y, fg_ref[...], D_MODEL)
    o_ref[...] = y


def _ffn_call(l, x, gain, wg, wu, wd, mix=None, final_gain=None):
    tok = lambda w: pl.BlockSpec((TM, w), lambda i: (i, 0))
    args = [x]
    specs = [tok(D_MODEL)]
    if mix is not None:
        oa, ob, oc, wo = mix
        args += [oa, ob, oc, wo]
        specs += [tok(WO_B), tok(WO_C - WO_B), tok(WO_ROWS - WO_C),
                  _layer_spec((WO_ROWS, D_MODEL), l)]
    args += [gain, wg, wu, wd]
    specs += [_layer_spec((1, D_MODEL), l), _layer_spec((D_MODEL, D_FF), l),
              _layer_spec((D_MODEL, D_FF), l), _layer_spec((D_FF, D_MODEL), l)]
    if final_gain is not None:
        args.append(final_gain)
        specs.append(_const_spec((1, D_MODEL)))
    kern = functools.partial(_ffn_kernel, mix=mix is not None, final=final_gain is not None)
    return pl.pallas_call(
        kern,
        grid=(N_TOK // TM,),
        in_specs=specs,
        out_specs=tok(D_MODEL),
        out_shape=jax.ShapeDtypeStruct((N_TOK, D_MODEL), F32),
        scratch_shapes=[pltpu.VMEM((TM, D_MODEL), F32)],
        compiler_params=pltpu.CompilerParams(
            dimension_semantics=("arbitrary",), vmem_limit_bytes=VMEM_LIMIT),
        name="ffn",
    )(*args)


ROPE_HALF = (16, 8, 4)


def _rope_fn(table):
    lane = lax.broadcasted_iota(jnp.int32, (table.shape[0], LANE), 1)
    between = lambda a, b: (lane >= a) & (lane < b)
    rot = (between(64, 96), lane < 16, between(16, 24) | between(48, 56))
    second = (between(80, 96), between(8, 16), between(20, 24) | between(52, 56))
    cache = {}

    def rope(v, typ):
        if typ not in cache:
            cache[typ] = (jnp.where(rot[typ], table[:, 0:LANE], 1.0),
                          jnp.where(rot[typ], table[:, LANE:2 * LANE], 0.0))
        cos, sin = cache[typ]
        r = ROPE_HALF[typ]
        partner = jnp.where(second[typ], pltpu.roll(v, r, 1), pltpu.roll(v, LANE - r, 1))
        return v * cos + partner * sin

    return rope


def _rope_q(q_ref, rope_ref, typ, packed=False):
    rope = _rope_fn(rope_ref[...])
    nb = q_ref.shape[1] // LANE
    blocks = [q_ref[:, b * LANE:(b + 1) * LANE].astype(F32) for b in range(nb)]
    if packed:
        lo = lax.broadcasted_iota(jnp.int32, (q_ref.shape[0], LANE), 1) < 64
        blocks = [h for b in blocks
                  for h in (jnp.where(lo, b, 0.0), pltpu.roll(jnp.where(lo, 0.0, b), 64, 1))]
    return [rope(b, typ).astype(BF16) for b in blocks]


def _proj_kernel(x_ref, g_ref, w_ref, rope_ref, qn_ref, kvn_ref, wuq_ref, wukv_ref,
                 qa_ref, ka_ref, va_ref, qb_ref, kvc_ref, ks_ref, vs_ref, kw_ref,
                 gt_ref, qc_ref, kd_ref):
    xn = _rms(x_ref[...], g_ref[...], D_MODEL).astype(BF16)
    lane = lax.broadcasted_iota(jnp.int32, (TM, LANE), 1)
    lo = lane < 64
    rope = _rope_fn(rope_ref[...])
    ones_row = (lane == VROW).astype(F32)

    def blk(v, b):
        return v[:, b * LANE:(b + 1) * LANE]

    ha = _dot(xn, w_ref[:, CE_A:CE_A + 384])
    a1 = blk(ha, 1)
    cq = jnp.concatenate([blk(ha, 0), jnp.where(lo, a1, 0.0)], axis=1)
    cqn = _rms(cq, qn_ref[...], A_Q_RANK).astype(BF16)
    ckvn = _rms(blk(ha, 2), kvn_ref[...], A_KV_RANK).astype(BF16)
    kpe = jnp.where((lane >= 64) & (lane < 96), rope(a1, 0), 0.0)
    gt_ref[0] = jax.nn.sigmoid(a1).T[96:128, :]
    q = _dot(cqn, wuq_ref[...])
    kv = _dot(ckvn, wukv_ref[...])
    qa_ref[...] = (q * ((A_NOPE + A_ROPE) ** -0.5 * LOG2E)).astype(BF16)
    for h in range(A_HEADS):
        ka_ref[:, h * LANE:(h + 1) * LANE] = (blk(kv, h) + kpe).astype(BF16)
        va_ref[:, h * LANE:(h + 1) * LANE] = (blk(kv, A_HEADS + h) + ones_row).astype(BF16)

    qb_ref[...] = (_dot(xn, w_ref[:, CE_BQ:CE_BQ + QB_W]) * (B_DH ** -0.5 * LOG2E)).astype(BF16)
    hk = _dot(xn, w_ref[:, CE_KVC:CE_KVC + 256])
    for g in range(B_KV_GROUPS):
        kvc_ref[g] = rope(blk(hk, g), 1)
    tpos = (pl.program_id(0) % (SEQ // TM)) * TM + lax.broadcasted_iota(jnp.int32, (TM, LANE), 0)
    onehot = ((lane - 64) == (tpos // SLC_LEN)).astype(F32)
    hk = _dot(xn, w_ref[:, CE_KSVS:CE_KSVS + 256])
    vs_ref[...] = hk.astype(BF16)
    for g in range(B_KV_GROUPS):
        ks_ref[:, g * LANE:(g + 1) * LANE] = jnp.where(lo, rope(blk(hk, g), 1), onehot).astype(BF16)
    hk = _dot(xn, w_ref[:, CE_KWVW:CE_KWVW + 256])
    for g in range(B_KV_GROUPS):
        kw_ref[:, g * LANE:(g + 1) * LANE] = rope(blk(hk, g), 1).astype(BF16)

    qc_ref[...] = (_dot(xn, w_ref[:, CE_CQ:CE_CQ + QC_W]) * (C_DH ** -0.5 * LOG2E)).astype(BF16)
    hc = _dot(xn, w_ref[:, CE_CKV:CE_CKV + 512])
    for h in range(C_HEADS):
        kd_ref[:, h * LANE:(h + 1) * LANE] = rope(blk(hc, h), 2).astype(BF16)


def _proj_call(l, x, gain, w_e, rope_tab, qn, kvn, wuq, wukv):
    tok = lambda w: pl.BlockSpec((TM, w), lambda i: (i, 0))
    bf = lambda w: jax.ShapeDtypeStruct((N_TOK, w), BF16)
    return pl.pallas_call(
        _proj_kernel,
        grid=(N_TOK // TM,),
        in_specs=[tok(D_MODEL), _layer_spec((1, D_MODEL), l), _layer_spec((D_MODEL, CE), l),
                  pl.BlockSpec((None, TM, 2 * LANE), lambda i: (i // (SEQ // TM), i % (SEQ // TM), 0)),
                  _layer_spec((1, 256), l), _layer_spec((1, LANE), l),
                  _layer_spec((256, 768), l), _layer_spec((LANE, 1536), l)],
        out_specs=[tok(768), tok(768), tok(768), tok(QB_W),
                   pl.BlockSpec((B_KV_GROUPS, TM, LANE), lambda i: (0, i, 0)),
                   tok(256), tok(256), tok(256),
                   pl.BlockSpec((1, B_KV_GROUPS * GATE_ROWS, TM), lambda i: (i, 0, 0)),
                   tok(QC_W), tok(512)],
        out_shape=[bf(768), bf(768), bf(768), bf(QB_W),
                   jax.ShapeDtypeStruct((B_KV_GROUPS, N_TOK, LANE), F32),
                   bf(256), bf(256), bf(256),
                   jax.ShapeDtypeStruct((NCH, B_KV_GROUPS * GATE_ROWS, TM), F32),
                   bf(QC_W), bf(512)],
        compiler_params=pltpu.CompilerParams(
            dimension_semantics=("arbitrary",), vmem_limit_bytes=VMEM_LIMIT),
        name="proj",
    )(x, gain, w_e, rope_tab, qn, kvn, wuq, wukv)


def _tile_pos(r):
    kk = lax.broadcasted_iota(jnp.int32, (TK, 1), 0)
    tt = lax.broadcasted_iota(jnp.int32, (1, r), 1) & (TQ - 1)
    return kk, tt


def _dot_nt(a, b):
    return lax.dot_general(a, b, (((1,), (1,)), ((), ())), preferred_element_type=F32)


def _score_stage(q, k, mask, m_prev, s_ref, m_ref):
    s = _dot_nt(k, q)
    if mask is not None:
        s = jnp.where(mask, s, MASKED)
    cm = jnp.max(s, axis=0, keepdims=True)
    m_ref[...] = cm if m_prev is None else jnp.maximum(m_prev[...], cm)
    s_ref[...] = s


def _value_stage(vt, s_ref, m_ref, mb_ref, acc_ref):
    m = m_ref[...]
    alpha = jnp.exp2(mb_ref[...] - m)
    p = jnp.exp2(s_ref[...] - m).astype(BF16)
    mb_ref[...] = m
    acc_ref[...] = alpha * acc_ref[...] + _dot(vt, p)


def _causal_attention(streams, i):
    for qt, k_of, vt_of, (s0, s1, m0, m1, mb, acc) in streams:
        kk, tt = _tile_pos(qt.shape[0])
        _score_stage(qt, k_of(i), kk <= tt, None, s0, m0)
        mb[...] = m0[...]
        acc[...] = jnp.zeros_like(acc)

    def step(score_j, pend_j, a, b):
        for qt, k_of, vt_of, sc in streams:
            bufs, maxs, mb, acc = (sc[0], sc[1]), (sc[2], sc[3]), sc[4], sc[5]
            _score_stage(qt, k_of(score_j), None, maxs[a], bufs[b], maxs[b])
            _value_stage(vt_of(pend_j), bufs[a], maxs[a], mb, acc)

    def fold(pend_j, a):
        for qt, k_of, vt_of, sc in streams:
            _value_stage(vt_of(pend_j), sc[a], sc[2 + a], sc[4], sc[5])

    def body(jj, carry):
        step(2 * jj, jnp.where(jj == 0, i, 2 * jj - 1), 0, 1)
        step(2 * jj + 1, 2 * jj, 1, 0)
        return carry

    npairs = i // 2
    lax.fori_loop(0, npairs, body, 0)
    pend = jnp.where(npairs == 0, i, 2 * npairs - 1)

    @pl.when(i % 2 == 1)
    def _():
        step(i - 1, pend, 0, 1)
        fold(i - 1, 1)

    @pl.when(i % 2 == 0)
    def _():
        fold(pend, 0)

    return [st[3][5][...] for st in streams]


def _normalise(acc):
    return acc[0:VROW] / acc[VROW:VROW + 1, :]


def _pipe_scratch(r):
    return ([pltpu.VMEM((TK, r), F32)] * 2 + [pltpu.VMEM((1, r), F32)] * 3
            + [pltpu.VMEM((VT_ROWS, r), F32)])


N_PIPE = 6


def _k_chunk(k_ref, j, sl=slice(None)):
    return k_ref[pl.ds(pl.multiple_of(j * TK, TK), TK), sl]


def _q_spec(nb):
    return pl.BlockSpec((None, TQ, nb * LANE), lambda b, h, i: (b, i, h))


def _kv_spec(nb, buffers=None):
    mode = {} if buffers is None else {"pipeline_mode": pl.Buffered(buffers)}
    return pl.BlockSpec((None, SEQ, nb * LANE), lambda b, h, i: (b, 0, h), **mode)


_ROPE_SPEC = pl.BlockSpec((None, TQ, 2 * LANE), lambda b, h, i: (b, i, 0))


def _vt_lower(chunk):
    return chunk.T[0:VT_ROWS]


def _vt_upper(chunk):
    ones = (lax.broadcasted_iota(jnp.int32, (VT_ROWS - VROW, chunk.shape[0]), 0) == 0).astype(BF16)
    return jnp.concatenate([chunk.T[VROW:2 * VROW], ones], axis=0)


def _attn_params():
    return pltpu.CompilerParams(dimension_semantics=("arbitrary",) * 3,
                                vmem_limit_bytes=VMEM_LIMIT)


def _mla_kernel(q_ref, rope_ref, k_ref, v_ref, o_ref, *scratch):
    i = pl.program_id(2)
    qs = _rope_q(q_ref, rope_ref, 0)
    streams = []
    for h in range(A_HPS):
        sl = slice(h * LANE, (h + 1) * LANE)
        streams.append((qs[h],
                        lambda j, sl=sl: _k_chunk(k_ref, j, sl),
                        lambda j, sl=sl: _vt_lower(_k_chunk(v_ref, j, sl)),
                        scratch[N_PIPE * h:N_PIPE * (h + 1)]))
    accs = _causal_attention(streams, i)
    outs = [_normalise(a) for a in accs]
    if A_OW > A_HPS * A_V:
        outs.append(jnp.zeros((A_OW - A_HPS * A_V, TQ), F32))
    o_ref[...] = jnp.concatenate(outs, axis=0).T.astype(BF16)


def _mla_call(qa, rope_tab, ka, va):
    return pl.pallas_call(
        _mla_kernel,
        grid=(BATCH, A_HEADS // A_HPS, NQ),
        in_specs=[_q_spec(A_HPS), _ROPE_SPEC, _kv_spec(A_HPS), _kv_spec(A_HPS)],
        out_specs=pl.BlockSpec((None, TQ, A_OW), lambda b, h, i: (b, i, h)),
        out_shape=jax.ShapeDtypeStruct((BATCH, SEQ, WO_B), BF16),
        scratch_shapes=_pipe_scratch(TQ) * A_HPS,
        compiler_params=_attn_params(),
        name="mla_attn",
    )(qa, rope_tab, ka, va)


C_HPS = 4


def _diff_kernel(q_ref, rope_ref, kv_ref, lqk_ref, sn_ref, o_ref, *scratch, lam_init):
    i = pl.program_id(2)
    lane = lax.broadcasted_iota(jnp.int32, (TQ, LANE), 1)
    qs = _rope_q(q_ref, rope_ref, 2, packed=True)
    streams = []
    for h in range(C_HPS):
        sl = slice(h * LANE, (h + 1) * LANE)
        q = qs[h]
        zero = jnp.zeros_like(q)
        q2 = jnp.concatenate([jnp.where(lane < C_DH, q, zero),
                              jnp.where((lane >= C_DH) & (lane < 2 * C_DH), q, zero)], axis=0)
        streams.append((q2,
                        lambda j, sl=sl: _k_chunk(kv_ref, j, sl),
                        lambda j, sl=sl: _vt_upper(_k_chunk(kv_ref, j, sl)),
                        scratch[N_PIPE * h:N_PIPE * (h + 1)]))
    accs = _causal_attention(streams, i)
    lqk = lqk_ref[...]
    lam = (jnp.exp(jnp.sum(lqk[0:1] * lqk[1:2], axis=-1, keepdims=True))
           - jnp.exp(jnp.sum(lqk[2:3] * lqk[3:4], axis=-1, keepdims=True)) + lam_init)
    diffs = []
    for a in accs:
        o = _normalise(a)
        diffs.append(o[:, 0:TQ] - lam * o[:, TQ:2 * TQ])
    d = jnp.concatenate(diffs, axis=0).T
    d2 = d * d
    hl = lax.broadcasted_iota(jnp.int32, d.shape, 1) // (2 * C_DH)
    ms = jnp.zeros_like(d)
    for h in range(C_HPS):
        ms_h = jnp.sum(jnp.where(hl == h, d2, 0.0), axis=-1, keepdims=True)
        ms = jnp.where(hl == h, ms_h, ms)
    y = d * lax.rsqrt(ms * (1.0 / (2 * C_DH)) + EPS) * sn_ref[...] * (1.0 - lam_init)
    o_ref[...] = y.astype(BF16)


def _diff_call(l, qc, rope_tab, kd, lqk, sn):
    lam_init = 0.8 - 0.6 * math.exp(-0.3 * l)
    return pl.pallas_call(
        functools.partial(_diff_kernel, lam_init=lam_init),
        grid=(BATCH, C_HEADS // C_HPS, NQ),
        in_specs=[_q_spec(C_HPS // 2), _ROPE_SPEC, _kv_spec(C_HPS),
                  _layer_spec((8, LANE), l), _layer_spec((1, C_HPS * 2 * C_DH), l)],
        out_specs=pl.BlockSpec((None, TQ, C_HPS * 2 * C_DH), lambda b, h, i: (b, i, h)),
        out_shape=jax.ShapeDtypeStruct((BATCH, SEQ, C_HEADS * 2 * C_DH), BF16),
        scratch_shapes=_pipe_scratch(2 * TQ) * C_HPS,
        compiler_params=_attn_params(),
        name="diff_attn",
    )(qc, rope_tab, kd, lqk, sn)


B_GPS = 2
assert (B_GPS * B_HPG) % 2 == 0 and C_HPS % 2 == 0


def _nsa_kernel(q_ref, rope_ref, kvc_ref, ks_ref, vs_ref, kw_ref, gt_ref,
                pe_ref, w1_ref, w2k_ref, w2v_ref, ovt_ref,
                o_ref, kc_s, vct_s, *slc_scratch):
    i = pl.program_id(2)

    @pl.when(i == 0)
    def _():
        half = CMP_LEN // 2
        for g in range(B_GPS):
            yt = jnp.zeros((N_CMP_PAD, LANE), F32)
            yb = jnp.zeros((N_CMP_PAD, LANE), F32)
            for l in range(half):
                x = kvc_ref[g, pl.ds(l, N_CMP_PAD, stride=CMP_STRIDE), :]
                yt = yt + _dot((x + pe_ref[l:l + 1, :]).astype(BF16), w1_ref[l])
                yb = yb + _dot((x + pe_ref[half + l:half + l + 1, :]).astype(BF16),
                               w1_ref[half + l])
            pre = yt + pltpu.roll(yb, N_CMP_PAD - 1, 0)
            act = (pre * jax.nn.sigmoid(pre)).astype(BF16)
            kc_s[g] = _dot(act, w2k_ref[...]).astype(BF16)
            vct_s[g] = _dot(act, w2v_ref[...]).T[0:VROW].astype(BF16)

    qs = _rope_q(q_ref, rope_ref, 1, packed=True)
    fronts = [_nsa_front(i, g, qs[g * B_HPG:(g + 1) * B_HPG], kw_ref, kc_s, vct_s, ovt_ref)
              for g in range(B_GPS)]

    streams = []
    for g in range(B_GPS):
        sl = slice(g * LANE, (g + 1) * LANE)
        streams.append((fronts[g][0],
                        lambda j, sl=sl: _k_chunk(ks_ref, j, sl),
                        lambda j, sl=sl: _vt_upper(_k_chunk(vs_ref, j, sl)),
                        slc_scratch[N_PIPE * g:N_PIPE * (g + 1)]))
    accs = _causal_attention(streams, i)

    for g in range(B_GPS):
        _, o_cmp, o_win = fronts[g]
        o_slc = _normalise(accs[g])
        gate = gt_ref[g * GATE_ROWS:(g + 1) * GATE_ROWS, :]
        outs = []
        for h in range(B_HPG):
            sl = slice(h * TQ, (h + 1) * TQ)
            outs.append(gate[3 * h:3 * h + 1] * o_cmp[:, sl]
                        + gate[3 * h + 1:3 * h + 2] * o_slc[:, sl]
                        + gate[3 * h + 2:3 * h + 3] * o_win[:, sl])
        base = g * 2 * LANE
        o_ref[:, base:base + LANE] = jnp.concatenate(outs[0:2], axis=0).T.astype(BF16)
        o_ref[:, base + LANE:base + 2 * LANE] = jnp.concatenate(
            [outs[2], jnp.zeros((VROW, TQ), F32)], axis=0).T.astype(BF16)


def _nsa_front(i, g, q_heads, kw_ref, kc_s, vct_s, ovt_ref):
    r3 = B_HPG * TQ
    gl = slice(g * LANE, (g + 1) * LANE)
    q3 = jnp.concatenate(q_heads, axis=0)

    half = TQ // 2
    kq = lax.broadcasted_iota(jnp.int32, (half, 1), 0)
    tq = lax.broadcasted_iota(jnp.int32, (1, B_HPG * half), 1) & (half - 1)
    no_prev = jnp.where(i >= 1, 0, half)
    o_halves = []
    for hf in range(2):
        qh = jnp.concatenate([q[hf * half:(hf + 1) * half] for q in q_heads], axis=0)
        subs = [kw_ref[pl.ds(pl.multiple_of(jnp.maximum(2 * i - 2 + hf + d, 0) * half, half), half),
                       gl] for d in range(3)]
        s0 = jnp.where(tq + no_prev < kq, _dot_nt(subs[0], qh), MASKED)
        s1 = _dot_nt(subs[1], qh)
        if hf == 0:
            s1 = s1 + jnp.where(i >= 1, 0.0, MASKED)
        s2 = jnp.where(kq <= tq, _dot_nt(subs[2], qh), MASKED)
        scores = (s0, s1, s2)
        m_win = functools.reduce(jnp.maximum, [jnp.max(s, axis=0, keepdims=True) for s in scores])
        acc = sum(_dot(_vt_upper(k), jnp.exp2(s - m_win).astype(BF16))
                  for k, s in zip(subs, scores))
        o_halves.append(_normalise(acc))
    o_win = jnp.concatenate([o_halves[hf][:, h * half:(h + 1) * half]
                             for h in range(B_HPG) for hf in range(2)], axis=1)

    kk, tt = _tile_pos(r3)
    s = _dot_nt(kc_s[g], q3)
    c_end = lax.broadcasted_iota(jnp.int32, (N_CMP_PAD, 1), 0) * CMP_STRIDE + (CMP_LEN - 1)
    valid = c_end <= i * TQ + tt
    smax = jnp.max(jnp.where(valid, s, MASKED), axis=0, keepdims=True)
    e = jnp.where(valid, jnp.exp2(s - smax), 0.0)
    den = jnp.sum(e, axis=0, keepdims=True)
    p = e / jnp.where(den > 0.0, den, 1.0)
    o_cmp = _dot(vct_s[g], p.astype(BF16))

    psum = p[:, 0:TQ] + p[:, TQ:2 * TQ] + p[:, 2 * TQ:3 * TQ]
    p_hi = psum.astype(BF16)
    p_lo = (psum - p_hi.astype(F32)).astype(BF16)
    ovt = ovt_ref[...]
    imp = (_dot(ovt, p_hi) + _dot(ovt, p_lo))[64:128]
    m_idx = lax.broadcasted_iota(jnp.int32, (N_SLC, TQ), 0)
    cur = (i * TQ + lax.broadcasted_iota(jnp.int32, (N_SLC, TQ), 1)) // SLC_LEN
    forced = (m_idx == 0) | (m_idx == cur) | (m_idx == cur - 1)
    score = jnp.where(forced, 1e9, jnp.where(m_idx <= cur, imp, -1.0))
    sub = lax.broadcasted_iota(jnp.int32, (SUBLANES, TQ), 0)
    bias = [jnp.zeros((64, TQ), F32)]
    for v in range(N_SLC // SUBLANES):
        tile = score[v * SUBLANES:(v + 1) * SUBLANES, :]
        rank = jnp.zeros((SUBLANES, TQ), jnp.int32)
        for mp in range(N_SLC):
            srow = score[mp:mp + 1, :]
            if mp < v * SUBLANES:
                ahead = srow >= tile
            elif mp >= (v + 1) * SUBLANES:
                ahead = srow > tile
            else:
                ahead = (srow > tile) | ((srow == tile) & (sub > mp - v * SUBLANES))
            rank = rank + jnp.where(ahead, 1, 0)
        bias.append(jnp.where(rank < SLC_TOPN, 0.0, -SEL_BIG))
    bias = jnp.concatenate(bias, axis=0).T.astype(BF16)
    q_sel = q3 + jnp.concatenate([bias] * B_HPG, axis=0)
    return q_sel, o_cmp, o_win


def _nsa_call(l, qb, rope_tab, kvc, ks, vs, kw, gt, pe, w1, w2k, w2v, ovt):
    r3 = B_HPG * TQ
    return pl.pallas_call(
        _nsa_kernel,
        grid=(BATCH, B_KV_GROUPS // B_GPS, NQ),
        in_specs=[_q_spec(B_GPS * B_HPG // 2), _ROPE_SPEC,
                  pl.BlockSpec((B_GPS, None, SEQ, LANE), lambda b, g, i: (g, b, 0, 0),
                               pipeline_mode=pl.Buffered(1)),
                  _kv_spec(B_GPS, 1), _kv_spec(B_GPS, 1), _kv_spec(B_GPS, 1),
                  pl.BlockSpec((None, B_GPS * GATE_ROWS, TQ), lambda b, g, i: (b * NQ + i, g, 0)),
                  _layer_spec((CMP_LEN, LANE), l), _layer_spec((CMP_LEN, LANE, LANE), l),
                  _layer_spec((LANE, LANE), l), _layer_spec((LANE, LANE), l),
                  _const_spec((LANE, N_CMP_PAD))],
        out_specs=pl.BlockSpec((None, TQ, B_GPS * 2 * LANE), lambda b, g, i: (b, i, g)),
        out_shape=jax.ShapeDtypeStruct((BATCH, SEQ, B_KV_GROUPS * 2 * LANE), BF16),
        scratch_shapes=[pltpu.VMEM((B_GPS, N_CMP_PAD, LANE), BF16),
                        pltpu.VMEM((B_GPS, VROW, N_CMP_PAD), BF16)]
        + _pipe_scratch(r3) * B_GPS,
        compiler_params=_attn_params(),
        name="nsa_attn",
    )(qb, rope_tab, kvc, ks, vs, kw, gt, pe, w1, w2k, w2v, ovt)


def _expand_w_in(w):
    a = w[..., :A_COLS]
    b = w[..., A_COLS:A_COLS + B_COLS]
    c = w[..., A_COLS + B_COLS:]
    z = lambda n: jnp.zeros(w.shape[:-1] + (n,), w.dtype)
    gate = lambda g: [b[..., 1152 + g * 9:1161 + g * 9], z(GATE_ROWS - 9)]
    parts = [a[..., 0:192], a[..., 320:352]] + gate(0) + gate(1) + [a[..., 192:320]]
    parts.append(b[..., 0:QB_W])
    for base in (384, 640, 896):
        for g in range(B_KV_GROUPS):
            parts += [b[..., base + g * 64:base + g * 64 + 64],
                      b[..., base + 128 + g * 64:base + 192 + g * 64]]

    def rot_mid(t, o):
        return [t[..., o + 8:o + 24], t[..., o:o + 8], t[..., o + 24:o + 32]]

    for h in range(C_HEADS):
        parts += rot_mid(c, h * 64) + rot_mid(c, h * 64 + 32)
    for h in range(C_HEADS):
        parts += (rot_mid(c, 256 + h * 64) + rot_mid(c, 288 + h * 64)
                  + [c[..., 512 + h * 64:576 + h * 64]])
    return jnp.concatenate(parts, axis=-1).astype(BF16)


def _expand_w_out(w):
    z = lambda n: jnp.zeros((DEPTH, n, D_MODEL), w.dtype)
    parts = []
    aw = A_HPS * A_V
    for s in range(A_HEADS // A_HPS):
        parts.append(w[:, s * aw:(s + 1) * aw])
        if A_OW > aw:
            parts.append(z(A_OW - aw))
    b0 = A_HEADS * A_V
    gw = B_HPG * B_DH
    for g in range(B_KV_GROUPS):
        parts += [w[:, b0 + g * gw:b0 + (g + 1) * gw], z(2 * LANE - gw)]
    parts.append(w[:, b0 + B_KV_GROUPS * gw:])
    return jnp.concatenate(parts, axis=1).astype(BF16)


def _rope_table(positions):
    def inv(rot):
        return 1.0 / (ROPE_THETA ** (jnp.arange(0, rot, 2, dtype=F32) / rot))

    fa, fb, fc = inv(A_ROPE), inv(B_ROT), inv(C_ROT)
    z = lambda n: jnp.zeros((n,), F32)
    freq = jnp.concatenate([fb, fb, fc, fc, z(24), fc, fc, z(8), fa, fa, z(32)])
    sign = jnp.concatenate([-jnp.ones(8), jnp.ones(8), -jnp.ones(4), jnp.ones(4), z(24),
                            -jnp.ones(4), jnp.ones(4), z(8), -jnp.ones(16), jnp.ones(16),
                            z(32)]).astype(F32)
    ang = positions.astype(F32)[:, :, None] * freq
    return jnp.concatenate([jnp.cos(ang), sign * jnp.sin(ang)], axis=2)


def _overlap_t():
    c = jnp.arange(N_CMP_PAD)[None, :]
    m = jnp.arange(N_SLC)[:, None]
    ov = ((c * CMP_STRIDE < m * SLC_LEN + SLC_LEN)
          & (c * CMP_STRIDE + CMP_LEN - 1 >= m * SLC_LEN) & (c < N_CMP))
    return jnp.concatenate([jnp.zeros((64, N_CMP_PAD), BF16), ov.astype(BF16)], axis=0)


def kernel(x, positions, ffn1_norm, ffn1_wg, ffn1_wu, ffn1_wd, mix_norm, w_in, mla_q_norm, mla_kv_norm, mla_w_uq, mla_w_ukv, nsa_pe_k, nsa_pe_v, nsa_phi_k1, nsa_phi_k2, nsa_phi_v1, nsa_phi_v2, diff_lq1, diff_lk1, diff_lq2, diff_lk2, diff_sub_norm, w_out, ffn2_norm, ffn2_wg, ffn2_wu, ffn2_wd, final_norm):
    xf = x.reshape(N_TOK, D_MODEL)
    rope_tab = _rope_table(positions)
    ovt = _overlap_t()
    bf = lambda t: t.astype(BF16)
    row = lambda t: t.reshape(DEPTH, 1, t.shape[-1])
    pad_to = lambda t, axis, n: jnp.pad(
        t, [(0, n - t.shape[a]) if a == axis else (0, 0) for a in range(t.ndim)])

    ffn1 = (row(ffn1_norm), bf(ffn1_wg), bf(ffn1_wu), bf(ffn1_wd))
    ffn2 = (row(ffn2_norm), bf(ffn2_wg), bf(ffn2_wu), bf(ffn2_wd))
    w_e = _expand_w_in(w_in)
    w_o = _expand_w_out(w_out)
    wuq = pad_to(pad_to(mla_w_uq.reshape(DEPTH, A_Q_RANK, A_HEADS, A_NOPE + A_ROPE), 3, LANE), 1, 256)
    wuq = bf(wuq.reshape(DEPTH, 256, A_HEADS * LANE))
    wkv = mla_w_ukv.reshape(DEPTH, A_KV_RANK, A_HEADS, A_NOPE + A_V)
    wukv = bf(jnp.concatenate(
        [pad_to(wkv[..., :A_NOPE], 3, LANE).reshape(DEPTH, A_KV_RANK, A_HEADS * LANE),
         pad_to(wkv[..., A_NOPE:], 3, LANE).reshape(DEPTH, A_KV_RANK, A_HEADS * LANE)], axis=2))
    qn = pad_to(row(mla_q_norm), 2, 256)
    kvn = row(mla_kv_norm)

    k1 = nsa_phi_k1.reshape(DEPTH, CMP_LEN, B_DH, B_DH)
    v1 = nsa_phi_v1.reshape(DEPTH, CMP_LEN, B_DH, B_DH)
    w1 = bf(jnp.concatenate([pad_to(k1, 3, LANE), jnp.pad(v1, ((0, 0), (0, 0), (0, 0), (64, 0)))],
                            axis=2))
    pe = jnp.concatenate([nsa_pe_k, nsa_pe_v], axis=2)
    w2k = bf(pad_to(pad_to(nsa_phi_k2, 1, LANE), 2, LANE))
    w2v = bf(jnp.pad(nsa_phi_v2, ((0, 0), (64, 0), (0, 64))))
    lqk = pad_to(pad_to(jnp.stack([diff_lq1, diff_lk1, diff_lq2, diff_lk2], axis=1), 1, 8), 2, LANE)
    sn = jnp.tile(row(diff_sub_norm), (1, 1, C_HPS))

    b3 = lambda t: t.reshape(BATCH, SEQ, t.shape[-1])
    flat = lambda t: t.reshape(N_TOK, t.shape[-1])
    rope3 = rope_tab
    for l in range(DEPTH):
        xf = _ffn_call(l, xf, *ffn1)
        (qa, ka, va, qb, kvc, ks, vs, kw, gt, qc, kd) = _proj_call(
            l, xf, row(mix_norm), w_e, rope_tab, qn, kvn, wuq, wukv)
        oa = _mla_call(b3(qa), rope3, b3(ka), b3(va))
        ob = _nsa_call(l, b3(qb), rope3, kvc.reshape(B_KV_GROUPS, BATCH, SEQ, LANE),
                       b3(ks), b3(vs), b3(kw), gt, pe, w1, w2k, w2v, ovt)
        oc = _diff_call(l, b3(qc), rope3, b3(kd), lqk, sn)
        fg = final_norm.reshape(1, D_MODEL) if l == DEPTH - 1 else None
        xf = _ffn_call(l, xf, *ffn2, mix=(flat(oa), flat(ob), flat(oc), w_o), final_gain=fg)
    return xf.reshape(BATCH, SEQ, D_MODEL)
```

```python
import functools
import math

import jax
import jax.numpy as jnp
from jax import lax
from jax.experimental import pallas as pl
from jax.experimental.pallas import tpu as pltpu

F32 = jnp.float32
BF16 = jnp.bfloat16

D_MODEL = 1024
BATCH = 4
SEQ = 4096
DEPTH = 2
N_TOK = BATCH * SEQ
EPS = 1e-6
ROPE_THETA = 500000.0
D_FF = 2816

A_HEADS = 6
A_Q_RANK = 192
A_KV_RANK = 128
A_NOPE = 64
A_ROPE = 32
A_V = 64

B_HEADS = 6
B_KV_GROUPS = 2
B_HPG = 3
B_DH = 64
B_ROT = 16
CMP_LEN = 32
CMP_STRIDE = 16
N_CMP = (SEQ - CMP_LEN) // CMP_STRIDE + 1
N_CMP_PAD = SEQ // CMP_STRIDE
SLC_LEN = 64
N_SLC = SEQ // SLC_LEN
SLC_TOPN = 16
WINDOW = 512

C_HEADS = 4
C_DH = 32
C_ROT = 8

A_COLS = A_Q_RANK + A_KV_RANK + A_ROPE
B_COLS = B_HEADS * B_DH + 6 * B_KV_GROUPS * B_DH + 3 * B_HEADS
C_COLS = 3 * C_HEADS * 2 * C_DH

LANE = 128
SUBLANES = 8
VROW = 64
VT_ROWS = 72
MASKED = -1e30
SEL_BIG = 1e9
LOG2E = 1.0 / math.log(2.0)

TM = 512
TF = 256
NFC = D_FF // TF
TQ = 512
TK = 512
NQ = SEQ // TQ
NCH = N_TOK // TK
GATE_ROWS = 16

CE_A = 0
CE_BQ = 384
CE_KVC = 768
CE_KSVS = 1024
CE_KWVW = 1280
CE_CQ = 1536
CE_CKV = 1792
CE = 2304
QB_W = B_HEADS * B_DH
QC_W = C_HEADS * 2 * C_DH

A_HPS = 6
A_OW = -(-A_HPS * A_V // LANE) * LANE

WO_B = A_HEADS // A_HPS * A_OW
WO_C = WO_B + B_KV_GROUPS * 2 * LANE
WO_ROWS = WO_C + C_HEADS * 2 * C_DH

V7X_VMEM_BYTES = 64 * 1024 * 1024
VMEM_LIMIT = V7X_VMEM_BYTES - 8 * 1024 * 1024


def _const_spec(shape):
    nd = len(shape)
    return pl.BlockSpec(shape, lambda *_: (0,) * nd, pipeline_mode=pl.Buffered(1))


def _layer_spec(shape, l):
    nd = len(shape)
    return pl.BlockSpec((None,) + tuple(shape), lambda *_: (l,) + (0,) * nd,
                        pipeline_mode=pl.Buffered(1))


def _rms(x, g, n):
    ms = jnp.sum(x * x, axis=-1, keepdims=True) * (1.0 / n)
    return x * lax.rsqrt(ms + EPS) * g


def _dot(a, b):
    return jnp.dot(a, b, preferred_element_type=F32)


def _ffn_kernel(*refs, mix, final):
    it = iter(refs)
    x_ref = next(it)
    if mix:
        oa_ref, ob_ref, oc_ref, wo_ref = next(it), next(it), next(it), next(it)
    g_ref, wg_ref, wu_ref, wd_ref = next(it), next(it), next(it), next(it)
    if final:
        fg_ref = next(it)
    o_ref = next(it)
    acc_ref = next(it)

    x = x_ref[...]
    if mix:
        x = x + _dot(oa_ref[...], wo_ref[0:WO_B, :])
        x = x + _dot(ob_ref[...], wo_ref[WO_B:WO_C, :])
        x = x + _dot(oc_ref[...], wo_ref[WO_C:WO_ROWS, :])
    xn = _rms(x, g_ref[...], D_MODEL).astype(BF16)
    for c in range(NFC):
        g = _dot(xn, wg_ref[:, c * TF:(c + 1) * TF])
        u = _dot(xn, wu_ref[:, c * TF:(c + 1) * TF])
        h = (g * jax.nn.sigmoid(g) * u).astype(BF16)
        d = _dot(h, wd_ref[c * TF:(c + 1) * TF, :])
        if c == 0:
            acc_ref[...] = d
        else:
            acc_ref[...] += d
    y = x + 0.5 * acc_ref[...]
    if final:
        y = _rms(y, fg_ref[...], D_MODEL)
    o_ref[...] = y


def _ffn_call(l, x, gain, wg, wu, wd, mix=None, final_gain=None):
    tok = lambda w: pl.BlockSpec((TM, w), lambda i: (i, 0))
    args = [x]
    specs = [tok(D_MODEL)]
    if mix is not None:
        oa, ob, oc, wo = mix
        args += [oa, ob, oc, wo]
        specs += [tok(WO_B), tok(WO_C - WO_B), tok(WO_ROWS - WO_C),
                  _layer_spec((WO_ROWS, D_MODEL), l)]
    args += [gain, wg, wu, wd]
    specs += [_layer_spec((1, D_MODEL), l), _layer_spec((D_MODEL, D_FF), l),
              _layer_spec((D_MODEL, D_FF), l), _layer_spec((D_FF, D_MODEL), l)]
    if final_gain is not None:
        args.append(final_gain)
        specs.append(_const_spec((1, D_MODEL)))
    kern = functools.partial(_ffn_kernel, mix=mix is not None, final=final_gain is not None)
    return pl.pallas_call(
        kern,
        grid=(N_TOK // TM,),
        in_specs=specs,
        out_specs=tok(D_MODEL),
        out_shape=jax.ShapeDtypeStruct((N_TOK, D_MODEL), F32),
        scratch_shapes=[pltpu.VMEM((TM, D_MODEL), F32)],
        compiler_params=pltpu.CompilerParams(
            dimension_semantics=("arbitrary",), vmem_limit_bytes=VMEM_LIMIT),
        name="ffn",
    )(*args)


ROPE_HALF = (16, 8, 4)


def _rope_fn(table):
    lane = lax.broadcasted_iota(jnp.int32, (table.shape[0], LANE), 1)
    between = lambda a, b: (lane >= a) & (lane < b)
    rot = (between(64, 96), lane < 16, between(16, 24) | between(48, 56))
    second = (between(80, 96), between(8, 16), between(20, 24) | between(52, 56))
    cache = {}

    def rope(v, typ):
        if typ not in cache:
            cache[typ] = (jnp.where(rot[typ], table[:, 0:LANE], 1.0),
                          jnp.where(rot[typ], table[:, LANE:2 * LANE], 0.0))
        cos, sin = cache[typ]
        r = ROPE_HALF[typ]
        partner = jnp.where(second[typ], pltpu.roll(v, r, 1), pltpu.roll(v, LANE - r, 1))
        return v * cos + partner * sin

    return rope


def _rope_q(q_ref, rope_ref, typ, packed=False):
    rope = _rope_fn(rope_ref[...])
    nb = q_ref.shape[1] // LANE
    blocks = [q_ref[:, b * LANE:(b + 1) * LANE].astype(F32) for b in range(nb)]
    if packed:
        lo = lax.broadcasted_iota(jnp.int32, (q_ref.shape[0], LANE), 1) < 64
        blocks = [h for b in blocks
                  for h in (jnp.where(lo, b, 0.0), pltpu.roll(jnp.where(lo, 0.0, b), 64, 1))]
    return [rope(b, typ).astype(BF16) for b in blocks]


def _proj_kernel(x_ref, g_ref, w_ref, rope_ref, qn_ref, kvn_ref, wuq_ref, wukv_ref,
                 qa_ref, ka_ref, va_ref, qb_ref, kvc_ref, ks_ref, vs_ref, kw_ref,
                 gt_ref, qc_ref, kd_ref):
    xn = _rms(x_ref[...], g_ref[...], D_MODEL).astype(BF16)
    lane = lax.broadcasted_iota(jnp.int32, (TM, LANE), 1)
    lo = lane < 64
    rope = _rope_fn(rope_ref[...])
    ones_row = (lane == VROW).astype(F32)

    def blk(v, b):
        return v[:, b * LANE:(b + 1) * LANE]

    ha = _dot(xn, w_ref[:, CE_A:CE_A + 384])
    a1 = blk(ha, 1)
    cq = jnp.concatenate([blk(ha, 0), jnp.where(lo, a1, 0.0)], axis=1)
    cqn = _rms(cq, qn_ref[...], A_Q_RANK).astype(BF16)
    ckvn = _rms(blk(ha, 2), kvn_ref[...], A_KV_RANK).astype(BF16)
    kpe = jnp.where((lane >= 64) & (lane < 96), rope(a1, 0), 0.0)
    gt_ref[0] = jax.nn.sigmoid(a1).T[96:128, :]
    q = _dot(cqn, wuq_ref[...])
    kv = _dot(ckvn, wukv_ref[...])
    qa_ref[...] = (q * ((A_NOPE + A_ROPE) ** -0.5 * LOG2E)).astype(BF16)
    for h in range(A_HEADS):
        ka_ref[:, h * LANE:(h + 1) * LANE] = (blk(kv, h) + kpe).astype(BF16)
        va_ref[:, h * LANE:(h + 1) * LANE] = (blk(kv, A_HEADS + h) + ones_row).astype(BF16)

    qb_ref[...] = (_dot(xn, w_ref[:, CE_BQ:CE_BQ + QB_W]) * (B_DH ** -0.5 * LOG2E)).astype(BF16)
    hk = _dot(xn, w_ref[:, CE_KVC:CE_KVC + 256])
    for g in range(B_KV_GROUPS):
        kvc_ref[g] = rope(blk(hk, g), 1)
    tpos = (pl.program_id(0) % (SEQ // TM)) * TM + lax.broadcasted_iota(jnp.int32, (TM, LANE), 0)
    onehot = ((lane - 64) == (tpos // SLC_LEN)).astype(F32)
    hk = _dot(xn, w_ref[:, CE_KSVS:CE_KSVS + 256])
    vs_ref[...] = hk.astype(BF16)
    for g in range(B_KV_GROUPS):
        ks_ref[:, g * LANE:(g + 1) * LANE] = jnp.where(lo, rope(blk(hk, g), 1), onehot).astype(BF16)
    hk = _dot(xn, w_ref[:, CE_KWVW:CE_KWVW + 256])
    for g in range(B_KV_GROUPS):
        kw_ref[:, g * LANE:(g + 1) * LANE] = rope(blk(hk, g), 1).astype(BF16)

    qc_ref[...] = (_dot(xn, w_ref[:, CE_CQ:CE_CQ + QC_W]) * (C_DH ** -0.5 * LOG2E)).astype(BF16)
    hc = _dot(xn, w_ref[:, CE_CKV:CE_CKV + 512])
    for h in range(C_HEADS):
        kd_ref[:, h * LANE:(h + 1) * LANE] = rope(blk(hc, h), 2).astype(BF16)


def _proj_call(l, x, gain, w_e, rope_tab, qn, kvn, wuq, wukv):
    tok = lambda w: pl.BlockSpec((TM, w), lambda i: (i, 0))
    bf = lambda w: jax.ShapeDtypeStruct((N_TOK, w), BF16)
    return pl.pallas_call(
        _proj_kernel,
        grid=(N_TOK // TM,),
        in_specs=[tok(D_MODEL), _layer_spec((1, D_MODEL), l), _layer_spec((D_MODEL, CE), l),
                  pl.BlockSpec((None, TM, 2 * LANE), lambda i: (i // (SEQ // TM), i % (SEQ // TM), 0)),
                  _layer_spec((1, 256), l), _layer_spec((1, LANE), l),
                  _layer_spec((256, 768), l), _layer_spec((LANE, 1536), l)],
        out_specs=[tok(768), tok(768), tok(768), tok(QB_W),
                   pl.BlockSpec((B_KV_GROUPS, TM, LANE), lambda i: (0, i, 0)),
                   tok(256), tok(256), tok(256),
                   pl.BlockSpec((1, B_KV_GROUPS * GATE_ROWS, TM), lambda i: (i, 0, 0)),
                   tok(QC_W), tok(512)],
        out_shape=[bf(768), bf(768), bf(768), bf(QB_W),
                   jax.ShapeDtypeStruct((B_KV_GROUPS, N_TOK, LANE), F32),
                   bf(256), bf(256), bf(256),
                   jax.ShapeDtypeStruct((NCH, B_KV_GROUPS * GATE_ROWS, TM), F32),
                   bf(QC_W), bf(512)],
        compiler_params=pltpu.CompilerParams(
            dimension_semantics=("arbitrary",), vmem_limit_bytes=VMEM_LIMIT),
        name="proj",
    )(x, gain, w_e, rope_tab, qn, kvn, wuq, wukv)


def _tile_pos(r):
    kk = lax.broadcasted_iota(jnp.int32, (TK, 1), 0)
    tt = lax.broadcasted_iota(jnp.int32, (1, r), 1) & (TQ - 1)
    return kk, tt


def _dot_nt(a, b):
    return lax.dot_general(a, b, (((1,), (1,)), ((), ())), preferred_element_type=F32)


def _score_stage(q, k, mask, m_prev, s_ref, m_ref):
    s = _dot_nt(k, q)
    if mask is not None:
        s = jnp.where(mask, s, MASKED)
    cm = jnp.max(s, axis=0, keepdims=True)
    m_ref[...] = cm if m_prev is None else jnp.maximum(m_prev[...], cm)
    s_ref[...] = s


def _value_stage(vt, s_ref, m_ref, mb_ref, acc_ref):
    m = m_ref[...]
    alpha = jnp.exp2(mb_ref[...] - m)
    p = jnp.exp2(s_ref[...] - m).astype(BF16)
    mb_ref[...] = m
    acc_ref[...] = alpha * acc_ref[...] + _dot(vt, p)


def _causal_attention(streams, i):
    for qt, k_of, vt_of, (s0, s1, m0, m1, mb, acc) in streams:
        kk, tt = _tile_pos(qt.shape[0])
        _score_stage(qt, k_of(i), kk <= tt, None, s0, m0)
        mb[...] = m0[...]
        acc[...] = jnp.zeros_like(acc)

    def step(score_j, pend_j, a, b):
        for qt, k_of, vt_of, sc in streams:
            bufs, maxs, mb, acc = (sc[0], sc[1]), (sc[2], sc[3]), sc[4], sc[5]
            _score_stage(qt, k_of(score_j), None, maxs[a], bufs[b], maxs[b])
            _value_stage(vt_of(pend_j), bufs[a], maxs[a], mb, acc)

    def fold(pend_j, a):
        for qt, k_of, vt_of, sc in streams:
            _value_stage(vt_of(pend_j), sc[a], sc[2 + a], sc[4], sc[5])

    def body(jj, carry):
        step(2 * jj, jnp.where(jj == 0, i, 2 * jj - 1), 0, 1)
        step(2 * jj + 1, 2 * jj, 1, 0)
        return carry

    npairs = i // 2
    lax.fori_loop(0, npairs, body, 0)
    pend = jnp.where(npairs == 0, i, 2 * npairs - 1)

    @pl.when(i % 2 == 1)
    def _():
        step(i - 1, pend, 0, 1)
        fold(i - 1, 1)

    @pl.when(i % 2 == 0)
    def _():
        fold(pend, 0)

    return [st[3][5][...] for st in streams]


def _normalise(acc):
    return acc[0:VROW] / acc[VROW:VROW + 1, :]


def _pipe_scratch(r):
    return ([pltpu.VMEM((TK, r), F32)] * 2 + [pltpu.VMEM((1, r), F32)] * 3
            + [pltpu.VMEM((VT_ROWS, r), F32)])


N_PIPE = 6


def _k_chunk(k_ref, j, sl=slice(None)):
    return k_ref[pl.ds(pl.multiple_of(j * TK, TK), TK), sl]


def _q_spec(nb):
    return pl.BlockSpec((None, TQ, nb * LANE), lambda b, h, i: (b, i, h))


def _kv_spec(nb, buffers=None):
    mode = {} if buffers is None else {"pipeline_mode": pl.Buffered(buffers)}
    return pl.BlockSpec((None, SEQ, nb * LANE), lambda b, h, i: (b, 0, h), **mode)


_ROPE_SPEC = pl.BlockSpec((None, TQ, 2 * LANE), lambda b, h, i: (b, i, 0))


def _vt_lower(chunk):
    return chunk.T[0:VT_ROWS]


def _vt_upper(chunk):
    ones = (lax.broadcasted_iota(jnp.int32, (VT_ROWS - VROW, chunk.shape[0]), 0) == 0).astype(BF16)
    return jnp.concatenate([chunk.T[VROW:2 * VROW], ones], axis=0)


def _attn_params():
    return pltpu.CompilerParams(dimension_semantics=("arbitrary",) * 3,
                                vmem_limit_bytes=VMEM_LIMIT)


def _mla_kernel(q_ref, rope_ref, k_ref, v_ref, o_ref, *scratch):
    i = pl.program_id(2)
    qs = _rope_q(q_ref, rope_ref, 0)
    streams = []
    for h in range(A_HPS):
        sl = slice(h * LANE, (h + 1) * LANE)
        streams.append((qs[h],
                        lambda j, sl=sl: _k_chunk(k_ref, j, sl),
                        lambda j, sl=sl: _vt_lower(_k_chunk(v_ref, j, sl)),
                        scratch[N_PIPE * h:N_PIPE * (h + 1)]))
    accs = _causal_attention(streams, i)
    outs = [_normalise(a) for a in accs]
    if A_OW > A_HPS * A_V:
        outs.append(jnp.zeros((A_OW - A_HPS * A_V, TQ), F32))
    o_ref[...] = jnp.concatenate(outs, axis=0).T.astype(BF16)


def _mla_call(qa, rope_tab, ka, va):
    return pl.pallas_call(
        _mla_kernel,
        grid=(BATCH, A_HEADS // A_HPS, NQ),
        in_specs=[_q_spec(A_HPS), _ROPE_SPEC, _kv_spec(A_HPS), _kv_spec(A_HPS)],
        out_specs=pl.BlockSpec((None, TQ, A_OW), lambda b, h, i: (b, i, h)),
        out_shape=jax.ShapeDtypeStruct((BATCH, SEQ, WO_B), BF16),
        scratch_shapes=_pipe_scratch(TQ) * A_HPS,
        compiler_params=_attn_params(),
        name="mla_attn",
    )(qa, rope_tab, ka, va)


C_HPS = 4


def _diff_kernel(q_ref, rope_ref, kv_ref, lqk_ref, sn_ref, o_ref, *scratch, lam_init):
    i = pl.program_id(2)
    lane = lax.broadcasted_iota(jnp.int32, (TQ, LANE), 1)
    qs = _rope_q(q_ref, rope_ref, 2, packed=True)
    streams = []
    for h in range(C_HPS):
        sl = slice(h * LANE, (h + 1) * LANE)
        q = qs[h]
        zero = jnp.zeros_like(q)
        q2 = jnp.concatenate([jnp.where(lane < C_DH, q, zero),
                              jnp.where((lane >= C_DH) & (lane < 2 * C_DH), q, zero)], axis=0)
        streams.append((q2,
                        lambda j, sl=sl: _k_chunk(kv_ref, j, sl),
                        lambda j, sl=sl: _vt_upper(_k_chunk(kv_ref, j, sl)),
                        scratch[N_PIPE * h:N_PIPE * (h + 1)]))
    accs = _causal_attention(streams, i)
    lqk = lqk_ref[...]
    lam = (jnp.exp(jnp.sum(lqk[0:1] * lqk[1:2], axis=-1, keepdims=True))
           - jnp.exp(jnp.sum(lqk[2:3] * lqk[3:4], axis=-1, keepdims=True)) + lam_init)
    diffs = []
    for a in accs:
        o = _normalise(a)
        diffs.append(o[:, 0:TQ] - lam * o[:, TQ:2 * TQ])
    d = jnp.concatenate(diffs, axis=0).T
    d2 = d * d
    hl = lax.broadcasted_iota(jnp.int32, d.shape, 1) // (2 * C_DH)
    ms = jnp.zeros_like(d)
    for h in range(C_HPS):
        ms_h = jnp.sum(jnp.where(hl == h, d2, 0.0), axis=-1, keepdims=True)
        ms = jnp.where(hl == h, ms_h, ms)
    y = d * lax.rsqrt(ms * (1.0 / (2 * C_DH)) + EPS) * sn_ref[...] * (1.0 - lam_init)
    o_ref[...] = y.astype(BF16)


def _diff_call(l, qc, rope_tab, kd, lqk, sn):
    lam_init = 0.8 - 0.6 * math.exp(-0.3 * l)
    return pl.pallas_call(
        functools.partial(_diff_kernel, lam_init=lam_init),
        grid=(BATCH, C_HEADS // C_HPS, NQ),
        in_specs=[_q_spec(C_HPS // 2), _ROPE_SPEC, _kv_spec(C_HPS),
                  _layer_spec((8, LANE), l), _layer_spec((1, C_HPS * 2 * C_DH), l)],
        out_specs=pl.BlockSpec((None, TQ, C_HPS * 2 * C_DH), lambda b, h, i: (b, i, h)),
        out_shape=jax.ShapeDtypeStruct((BATCH, SEQ, C_HEADS * 2 * C_DH), BF16),
        scratch_shapes=_pipe_scratch(2 * TQ) * C_HPS,
        compiler_params=_attn_params(),
        name="diff_attn",
    )(qc, rope_tab, kd, lqk, sn)


B_GPS = 2
assert (B_GPS * B_HPG) % 2 == 0 and C_HPS % 2 == 0


def _nsa_kernel(q_ref, rope_ref, kvc_ref, ks_ref, vs_ref, kw_ref, gt_ref,
                pe_ref, w1_ref, w2k_ref, w2v_ref, ovt_ref,
                o_ref, kc_s, vct_s, *slc_scratch):
    i = pl.program_id(2)

    @pl.when(i == 0)
    def _():
        half = CMP_LEN // 2
        for g in range(B_GPS):
            yt = jnp.zeros((N_CMP_PAD, LANE), F32)
            yb = jnp.zeros((N_CMP_PAD, LANE), F32)
            for l in range(half):
                x = kvc_ref[g, pl.ds(l, N_CMP_PAD, stride=CMP_STRIDE), :]
                yt = yt + _dot((x + pe_ref[l:l + 1, :]).astype(BF16), w1_ref[l])
                yb = yb + _dot((x + pe_ref[half + l:half + l + 1, :]).astype(BF16),
                               w1_ref[half + l])
            pre = yt + pltpu.roll(yb, N_CMP_PAD - 1, 0)
            act = (pre * jax.nn.sigmoid(pre)).astype(BF16)
            kc_s[g] = _dot(act, w2k_ref[...]).astype(BF16)
            vct_s[g] = _dot(act, w2v_ref[...]).T[0:VROW].astype(BF16)

    qs = _rope_q(q_ref, rope_ref, 1, packed=True)
    fronts = [_nsa_front(i, g, qs[g * B_HPG:(g + 1) * B_HPG], kw_ref, kc_s, vct_s, ovt_ref)
              for g in range(B_GPS)]

    streams = []
    for g in range(B_GPS):
        sl = slice(g * LANE, (g + 1) * LANE)
        streams.append((fronts[g][0],
                        lambda j, sl=sl: _k_chunk(ks_ref, j, sl),
                        lambda j, sl=sl: _vt_upper(_k_chunk(vs_ref, j, sl)),
                        slc_scratch[N_PIPE * g:N_PIPE * (g + 1)]))
    accs = _causal_attention(streams, i)

    for g in range(B_GPS):
        _, o_cmp, o_win = fronts[g]
        o_slc = _normalise(accs[g])
        gate = gt_ref[g * GATE_ROWS:(g + 1) * GATE_ROWS, :]
        outs = []
        for h in range(B_HPG):
            sl = slice(h * TQ, (h + 1) * TQ)
            outs.append(gate[3 * h:3 * h + 1] * o_cmp[:, sl]
                        + gate[3 * h + 1:3 * h + 2] * o_slc[:, sl]
                        + gate[3 * h + 2:3 * h + 3] * o_win[:, sl])
        base = g * 2 * LANE
        o_ref[:, base:base + LANE] = jnp.concatenate(outs[0:2], axis=0).T.astype(BF16)
        o_ref[:, base + LANE:base + 2 * LANE] = jnp.concatenate(
            [outs[2], jnp.zeros((VROW, TQ), F32)], axis=0).T.astype(BF16)


def _nsa_front(i, g, q_heads, kw_ref, kc_s, vct_s, ovt_ref):
    r3 = B_HPG * TQ
    gl = slice(g * LANE, (g + 1) * LANE)
    q3 = jnp.concatenate(q_heads, axis=0)

    half = TQ // 2
    kq = lax.broadcasted_iota(jnp.int32, (half, 1), 0)
    tq = lax.broadcasted_iota(jnp.int32, (1, B_HPG * half), 1) & (half - 1)
    no_prev = jnp.where(i >= 1, 0, half)
    o_halves = []
    for hf in range(2):
        qh = jnp.concatenate([q[hf * half:(hf + 1) * half] for q in q_heads], axis=0)
        subs = [kw_ref[pl.ds(pl.multiple_of(jnp.maximum(2 * i - 2 + hf + d, 0) * half, half), half),
                       gl] for d in range(3)]
        s0 = jnp.where(tq + no_prev < kq, _dot_nt(subs[0], qh), MASKED)
        s1 = _dot_nt(subs[1], qh)
        if hf == 0:
            s1 = s1 + jnp.where(i >= 1, 0.0, MASKED)
        s2 = jnp.where(kq <= tq, _dot_nt(subs[2], qh), MASKED)
        scores = (s0, s1, s2)
        m_win = functools.reduce(jnp.maximum, [jnp.max(s, axis=0, keepdims=True) for s in scores])
        acc = sum(_dot(_vt_upper(k), jnp.exp2(s - m_win).astype(BF16))
                  for k, s in zip(subs, scores))
        o_halves.append(_normalise(acc))
    o_win = jnp.concatenate([o_halves[hf][:, h * half:(h + 1) * half]
                             for h in range(B_HPG) for hf in range(2)], axis=1)

    kk, tt = _tile_pos(r3)
    s = _dot_nt(kc_s[g], q3)
    c_end = lax.broadcasted_iota(jnp.int32, (N_CMP_PAD, 1), 0) * CMP_STRIDE + (CMP_LEN - 1)
    valid = c_end <= i * TQ + tt
    smax = jnp.max(jnp.where(valid, s, MASKED), axis=0, keepdims=True)
    e = jnp.where(valid, jnp.exp2(s - smax), 0.0)
    den = jnp.sum(e, axis=0, keepdims=True)
    p = e / jnp.where(den > 0.0, den, 1.0)
    o_cmp = _dot(vct_s[g], p.astype(BF16))

    psum = p[:, 0:TQ] + p[:, TQ:2 * TQ] + p[:, 2 * TQ:3 * TQ]
    p_hi = psum.astype(BF16)
    p_lo = (psum - p_hi.astype(F32)).astype(BF16)
    ovt = ovt_ref[...]
    imp = (_dot(ovt, p_hi) + _dot(ovt, p_lo))[64:128]
    m_idx = lax.broadcasted_iota(jnp.int32, (N_SLC, TQ), 0)
    cur = (i * TQ + lax.broadcasted_iota(jnp.int32, (N_SLC, TQ), 1)) // SLC_LEN
    forced = (m_idx == 0) | (m_idx == cur) | (m_idx == cur - 1)
    score = jnp.where(forced, 1e9, jnp.where(m_idx <= cur, imp, -1.0))
    sub = lax.broadcasted_iota(jnp.int32, (SUBLANES, TQ), 0)
    bias = [jnp.zeros((64, TQ), F32)]
    for v in range(N_SLC // SUBLANES):
        tile = score[v * SUBLANES:(v + 1) * SUBLANES, :]
        rank = jnp.zeros((SUBLANES, TQ), jnp.int32)
        for mp in range(N_SLC):
            srow = score[mp:mp + 1, :]
            if mp < v * SUBLANES:
                ahead = srow >= tile
            elif mp >= (v + 1) * SUBLANES:
                ahead = srow > tile
            else:
                ahead = (srow > tile) | ((srow == tile) & (sub > mp - v * SUBLANES))
            rank = rank + jnp.where(ahead, 1, 0)
        bias.append(jnp.where(rank < SLC_TOPN, 0.0, -SEL_BIG))
    bias = jnp.concatenate(bias, axis=0).T.astype(BF16)
    q_sel = q3 + jnp.concatenate([bias] * B_HPG, axis=0)
    return q_sel, o_cmp, o_win


def _nsa_call(l, qb, rope_tab, kvc, ks, vs, kw, gt, pe, w1, w2k, w2v, ovt):
    r3 = B_HPG * TQ
    return pl.pallas_call(
        _nsa_kernel,
        grid=(BATCH, B_KV_GROUPS // B_GPS, NQ),
        in_specs=[_q_spec(B_GPS * B_HPG // 2), _ROPE_SPEC,
                  pl.BlockSpec((B_GPS, None, SEQ, LANE), lambda b, g, i: (g, b, 0, 0),
                               pipeline_mode=pl.Buffered(1)),
                  _kv_spec(B_GPS, 1), _kv_spec(B_GPS, 1), _kv_spec(B_GPS, 1),
                  pl.BlockSpec((None, B_GPS * GATE_ROWS, TQ), lambda b, g, i: (b * NQ + i, g, 0)),
                  _layer_spec((CMP_LEN, LANE), l), _layer_spec((CMP_LEN, LANE, LANE), l),
                  _layer_spec((LANE, LANE), l), _layer_spec((LANE, LANE), l),
                  _const_spec((LANE, N_CMP_PAD))],
        out_specs=pl.BlockSpec((None, TQ, B_GPS * 2 * LANE), lambda b, g, i: (b, i, g)),
        out_shape=jax.ShapeDtypeStruct((BATCH, SEQ, B_KV_GROUPS * 2 * LANE), BF16),
        scratch_shapes=[pltpu.VMEM((B_GPS, N_CMP_PAD, LANE), BF16),
                        pltpu.VMEM((B_GPS, VROW, N_CMP_PAD), BF16)]
        + _pipe_scratch(r3) * B_GPS,
        compiler_params=_attn_params(),
        name="nsa_attn",
    )(qb, rope_tab, kvc, ks, vs, kw, gt, pe, w1, w2k, w2v, ovt)


def _expand_w_in(w):
    a = w[..., :A_COLS]
    b = w[..., A_COLS:A_COLS + B_COLS]
    c = w[..., A_COLS + B_COLS:]
    z = lambda n: jnp.zeros(w.shape[:-1] + (n,), w.dtype)
    gate = lambda g: [b[..., 1152 + g * 9:1161 + g * 9], z(GATE_ROWS - 9)]
    parts = [a[..., 0:192], a[..., 320:352]] + gate(0) + gate(1) + [a[..., 192:320]]
    parts.append(b[..., 0:QB_W])
    for base in (384, 640, 896):
        for g in range(B_KV_GROUPS):
            parts += [b[..., base + g * 64:base + g * 64 + 64],
                      b[..., base + 128 + g * 64:base + 192 + g * 64]]

    def rot_mid(t, o):
        return [t[..., o + 8:o + 24], t[..., o:o + 8], t[..., o + 24:o + 32]]

    for h in range(C_HEADS):
        parts += rot_mid(c, h * 64) + rot_mid(c, h * 64 + 32)
    for h in range(C_HEADS):
        parts += (rot_mid(c, 256 + h * 64) + rot_mid(c, 288 + h * 64)
                  + [c[..., 512 + h * 64:576 + h * 64]])
    return jnp.concatenate(parts, axis=-1).astype(BF16)


def _expand_w_out(w):
    z = lambda n: jnp.zeros((DEPTH, n, D_MODEL), w.dtype)
    parts = []
    aw = A_HPS * A_V
    for s in range(A_HEADS // A_HPS):
        parts.append(w[:, s * aw:(s + 1) * aw])
        if A_OW > aw:
            parts.append(z(A_OW - aw))
    b0 = A_HEADS * A_V
    gw = B_HPG * B_DH
    for g in range(B_KV_GROUPS):
        parts += [w[:, b0 + g * gw:b0 + (g + 1) * gw], z(2 * LANE - gw)]
    parts.append(w[:, b0 + B_KV_GROUPS * gw:])
    return jnp.concatenate(parts, axis=1).astype(BF16)


def _rope_table(positions):
    def inv(rot):
        return 1.0 / (ROPE_THETA ** (jnp.arange(0, rot, 2, dtype=F32) / rot))

    fa, fb, fc = inv(A_ROPE), inv(B_ROT), inv(C_ROT)
    z = lambda n: jnp.zeros((n,), F32)
    freq = jnp.concatenate([fb, fb, fc, fc, z(24), fc, fc, z(8), fa, fa, z(32)])
    sign = jnp.concatenate([-jnp.ones(8), jnp.ones(8), -jnp.ones(4), jnp.ones(4), z(24),
                            -jnp.ones(4), jnp.ones(4), z(8), -jnp.ones(16), jnp.ones(16),
                            z(32)]).astype(F32)
    ang = positions.astype(F32)[:, :, None] * freq
    return jnp.concatenate([jnp.cos(ang), sign * jnp.sin(ang)], axis=2)


def _overlap_t():
    c = jnp.arange(N_CMP_PAD)[None, :]
    m = jnp.arange(N_SLC)[:, None]
    ov = ((c * CMP_STRIDE < m * SLC_LEN + SLC_LEN)
          & (c * CMP_STRIDE + CMP_LEN - 1 >= m * SLC_LEN) & (c < N_CMP))
    return jnp.concatenate([jnp.zeros((64, N_CMP_PAD), BF16), ov.astype(BF16)], axis=0)


def kernel(x, positions, ffn1_norm, ffn1_wg, ffn1_wu, ffn1_wd, mix_norm, w_in, mla_q_norm, mla_kv_norm, mla_w_uq, mla_w_ukv, nsa_pe_k, nsa_pe_v, nsa_phi_k1, nsa_phi_k2, nsa_phi_v1, nsa_phi_v2, diff_lq1, diff_lk1, diff_lq2, diff_lk2, diff_sub_norm, w_out, ffn2_norm, ffn2_wg, ffn2_wu, ffn2_wd, final_norm):
    xf = x.reshape(N_TOK, D_MODEL)
    rope_tab = _rope_table(positions)
    ovt = _overlap_t()
    bf = lambda t: t.astype(BF16)
    row = lambda t: t.reshape(DEPTH, 1, t.shape[-1])
    pad_to = lambda t, axis, n: jnp.pad(
        t, [(0, n - t.shape[a]) if a == axis else (0, 0) for a in range(t.ndim)])

    ffn1 = (row(ffn1_norm), bf(ffn1_wg), bf(ffn1_wu), bf(ffn1_wd))
    ffn2 = (row(ffn2_norm), bf(ffn2_wg), bf(ffn2_wu), bf(ffn2_wd))
    w_e = _expand_w_in(w_in)
    w_o = _expand_w_out(w_out)
    wuq = pad_to(pad_to(mla_w_uq.reshape(DEPTH, A_Q_RANK, A_HEADS, A_NOPE + A_ROPE), 3, LANE), 1, 256)
    wuq = bf(wuq.reshape(DEPTH, 256, A_HEADS * LANE))
    wkv = mla_w_ukv.reshape(DEPTH, A_KV_RANK, A_HEADS, A_NOPE + A_V)
    wukv = bf(jnp.concatenate(
        [pad_to(wkv[..., :A_NOPE], 3, LANE).reshape(DEPTH, A_KV_RANK, A_HEADS * LANE),
         pad_to(wkv[..., A_NOPE:], 3, LANE).reshape(DEPTH, A_KV_RANK, A_HEADS * LANE)], axis=2))
    qn = pad_to(row(mla_q_norm), 2, 256)
    kvn = row(mla_kv_norm)

    k1 = nsa_phi_k1.reshape(DEPTH, CMP_LEN, B_DH, B_DH)
    v1 = nsa_phi_v1.reshape(DEPTH, CMP_LEN, B_DH, B_DH)
    w1 = bf(jnp.concatenate([pad_to(k1, 3, LANE), jnp.pad(v1, ((0, 0), (0, 0), (0, 0), (64, 0)))],
                            axis=2))
    pe = jnp.concatenate([nsa_pe_k, nsa_pe_v], axis=2)
    w2k = bf(pad_to(pad_to(nsa_phi_k2, 1, LANE), 2, LANE))
    w2v = bf(jnp.pad(nsa_phi_v2, ((0, 0), (64, 0), (0, 64))))
    lqk = pad_to(pad_to(jnp.stack([diff_lq1, diff_lk1, diff_lq2, diff_lk2], axis=1), 1, 8), 2, LANE)
    sn = jnp.tile(row(diff_sub_norm), (1, 1, C_HPS))

    b3 = lambda t: t.reshape(BATCH, SEQ, t.shape[-1])
    flat = lambda t: t.reshape(N_TOK, t.shape[-1])
    rope3 = rope_tab
    for l in range(DEPTH):
        xf = _ffn_call(l, xf, *ffn1)
        (qa, ka, va, qb, kvc, ks, vs, kw, gt, qc, kd) = _proj_call(
            l, xf, row(mix_norm), w_e, rope_tab, qn, kvn, wuq, wukv)
        oa = _mla_call(b3(qa), rope3, b3(ka), b3(va))
        ob = _nsa_call(l, b3(qb), rope3, kvc.reshape(B_KV_GROUPS, BATCH, SEQ, LANE),
                       b3(ks), b3(vs), b3(kw), gt, pe, w1, w2k, w2v, ovt)
        oc = _diff_call(l, b3(qc), rope3, b3(kd), lqk, sn)
        fg = final_norm.reshape(1, D_MODEL) if l == DEPTH - 1 else None
        xf = _ffn_call(l, xf, *ffn2, mix=(flat(oa), flat(ob), flat(oc), w_o), final_gain=fg)
    return xf.reshape(BATCH, SEQ, D_MODEL)
```

```python
import functools
import math

import jax
import jax.numpy as jnp
import numpy as np
from jax import lax
from jax.experimental import pallas as pl
from jax.experimental.pallas import tpu as pltpu

F32 = jnp.float32
BF16 = jnp.bfloat16

D_MODEL = 1024
BATCH = 4
SEQ = 4096
DEPTH = 2
N_TOK = BATCH * SEQ
EPS = 1e-6
ROPE_THETA = 500000.0
D_FF = 2816

A_HEADS = 6
A_Q_RANK = 192
A_KV_RANK = 128
A_NOPE = 64
A_ROPE = 32
A_V = 64

B_HEADS = 6
B_KV_GROUPS = 2
B_HPG = 3
B_DH = 64
B_ROT = 16
CMP_LEN = 32
CMP_STRIDE = 16
N_CMP = (SEQ - CMP_LEN) // CMP_STRIDE + 1
N_CMP_PAD = SEQ // CMP_STRIDE
SLC_LEN = 64
N_SLC = SEQ // SLC_LEN
SLC_TOPN = 16
WINDOW = 512

C_HEADS = 4
C_DH = 32
C_ROT = 8

A_COLS = A_Q_RANK + A_KV_RANK + A_ROPE
B_COLS = B_HEADS * B_DH + 6 * B_KV_GROUPS * B_DH + 3 * B_HEADS
C_COLS = 3 * C_HEADS * 2 * C_DH

LANE = 128
SUBLANES = 8
VROW = 64
VT_ROWS = 72
MASKED = -1e30
SEL_BIG = 1e9
LOG2E = 1.0 / math.log(2.0)

TM = 512
TF = 256
NFC = D_FF // TF
TQ = 512
TK = 512
NQ = SEQ // TQ
NCH = N_TOK // TK
GATE_ROWS = 16

CE_A = 0
CE_BQ = 384
CE_KVC = 768
CE_KSVS = 1024
CE_KWVW = 1280
CE_CQ = 1536
CE_CKV = 1792
CE = 2304
QB_W = B_HEADS * B_DH
QC_W = C_HEADS * 2 * C_DH

A_HPS = 6
A_OW = -(-A_HPS * A_V // LANE) * LANE

WO_B = A_HEADS // A_HPS * A_OW
WO_C = WO_B + B_KV_GROUPS * 2 * LANE
WO_ROWS = WO_C + C_HEADS * 2 * C_DH

V7X_VMEM_BYTES = 64 * 1024 * 1024
VMEM_LIMIT = V7X_VMEM_BYTES - 8 * 1024 * 1024


def _const_spec(shape):
    nd = len(shape)
    return pl.BlockSpec(shape, lambda *_: (0,) * nd, pipeline_mode=pl.Buffered(1))


def _layer_spec(shape, l):
    nd = len(shape)
    return pl.BlockSpec((None,) + tuple(shape), lambda *_: (l,) + (0,) * nd,
                        pipeline_mode=pl.Buffered(1))


def _rms(x, g, n):
    ms = jnp.sum(x * x, axis=-1, keepdims=True) * (1.0 / n)
    return x * lax.rsqrt(ms + EPS) * g


def _dot(a, b):
    return jnp.dot(a, b, preferred_element_type=F32)


def _ffn_kernel(*refs, mix, final):
    it = iter(refs)
    x_ref = next(it)
    if mix:
        oa_ref, ob_ref, oc_ref, wo_ref = next(it), next(it), next(it), next(it)
    g_ref, wg_ref, wu_ref, wd_ref = next(it), next(it), next(it), next(it)
    if final:
        fg_ref = next(it)
    o_ref = next(it)
    acc_ref = next(it)

    x = x_ref[...]
    if mix:
        x = x + _dot(oa_ref[...], wo_ref[0:WO_B, :])
        x = x + _dot(ob_ref[...], wo_ref[WO_B:WO_C, :])
        x = x + _dot(oc_ref[...], wo_ref[WO_C:WO_ROWS, :])
    xn = _rms(x, g_ref[...], D_MODEL).astype(BF16)
    for c in range(NFC):
        g = _dot(xn, wg_ref[:, c * TF:(c + 1) * TF])
        u = _dot(xn, wu_ref[:, c * TF:(c + 1) * TF])
        h = (g * jax.nn.sigmoid(g) * u).astype(BF16)
        d = _dot(h, wd_ref[c * TF:(c + 1) * TF, :])
        if c == 0:
            acc_ref[...] = d
        else:
            acc_ref[...] += d
    y = x + 0.5 * acc_ref[...]
    if final:
        y = _rms(y, fg_ref[...], D_MODEL)
    o_ref[...] = y


def _ffn_call(l, x, gain, wg, wu, wd, mix=None, final_gain=None):
    tok = lambda w: pl.BlockSpec((TM, w), lambda i: (i, 0))
    args = [x]
    specs = [tok(D_MODEL)]
    if mix is not None:
        oa, ob, oc, wo = mix
        args += [oa, ob, oc, wo]
        specs += [tok(WO_B), tok(WO_C - WO_B), tok(WO_ROWS - WO_C),
                  _layer_spec((WO_ROWS, D_MODEL), l)]
    args += [gain, wg, wu, wd]
    specs += [_layer_spec((1, D_MODEL), l), _layer_spec((D_MODEL, D_FF), l),
              _layer_spec((D_MODEL, D_FF), l), _layer_spec((D_FF, D_MODEL), l)]
    if final_gain is not None:
        args.append(final_gain)
        specs.append(_const_spec((1, D_MODEL)))
    kern = functools.partial(_ffn_kernel, mix=mix is not None, final=final_gain is not None)
    return pl.pallas_call(
        kern,
        grid=(N_TOK // TM,),
        in_specs=specs,
        out_specs=tok(D_MODEL),
        out_shape=jax.ShapeDtypeStruct((N_TOK, D_MODEL), F32),
        scratch_shapes=[pltpu.VMEM((TM, D_MODEL), F32)],
        compiler_params=pltpu.CompilerParams(
            dimension_semantics=("arbitrary",), vmem_limit_bytes=VMEM_LIMIT),
        name="ffn",
    )(*args)


ROPE_HALF = (16, 8, 4)


def _rope_fn(table):
    lane = lax.broadcasted_iota(jnp.int32, (table.shape[0], LANE), 1)
    between = lambda a, b: (lane >= a) & (lane < b)
    rot = (between(64, 96), lane < 16, between(16, 24) | between(48, 56))
    second = (between(80, 96), between(8, 16), between(20, 24) | between(52, 56))
    cache = {}

    def rope(v, typ):
        if typ not in cache:
            cache[typ] = (jnp.where(rot[typ], table[:, 0:LANE], 1.0),
                          jnp.where(rot[typ], table[:, LANE:2 * LANE], 0.0))
        cos, sin = cache[typ]
        r = ROPE_HALF[typ]
        partner = jnp.where(second[typ], pltpu.roll(v, r, 1), pltpu.roll(v, LANE - r, 1))
        return v * cos + partner * sin

    return rope


def _rope_q(q_ref, rope_ref, typ, packed=False):
    rope = _rope_fn(rope_ref[...])
    nb = q_ref.shape[1] // LANE
    blocks = [q_ref[:, b * LANE:(b + 1) * LANE].astype(F32) for b in range(nb)]
    if packed:
        lo = lax.broadcasted_iota(jnp.int32, (q_ref.shape[0], LANE), 1) < 64
        blocks = [h for b in blocks
                  for h in (jnp.where(lo, b, 0.0), pltpu.roll(jnp.where(lo, 0.0, b), 64, 1))]
    return [rope(b, typ).astype(BF16) for b in blocks]


def _proj_kernel(x_ref, g_ref, w_ref, rope_ref, qn_ref, kvn_ref, wuq_ref, wukv_ref,
                 qa_ref, ka_ref, va_ref, qb_ref, kvc_ref, ks_ref, vs_ref, kw_ref,
                 gt_ref, qc_ref, kd_ref):
    xn = _rms(x_ref[...], g_ref[...], D_MODEL).astype(BF16)
    lane = lax.broadcasted_iota(jnp.int32, (TM, LANE), 1)
    lo = lane < 64
    rope = _rope_fn(rope_ref[...])
    ones_row = (lane == VROW).astype(F32)

    def blk(v, b):
        return v[:, b * LANE:(b + 1) * LANE]

    ha = _dot(xn, w_ref[:, CE_A:CE_A + 384])
    a1 = blk(ha, 1)
    cq = jnp.concatenate([blk(ha, 0), jnp.where(lo, a1, 0.0)], axis=1)
    cqn = _rms(cq, qn_ref[...], A_Q_RANK).astype(BF16)
    ckvn = _rms(blk(ha, 2), kvn_ref[...], A_KV_RANK).astype(BF16)
    kpe = jnp.where((lane >= 64) & (lane < 96), rope(a1, 0), 0.0)
    gt_ref[0] = jax.nn.sigmoid(a1).T[96:128, :]
    q = _dot(cqn, wuq_ref[...])
    kv = _dot(ckvn, wukv_ref[...])
    qa_ref[...] = (q * ((A_NOPE + A_ROPE) ** -0.5 * LOG2E)).astype(BF16)
    for h in range(A_HEADS):
        ka_ref[:, h * LANE:(h + 1) * LANE] = (blk(kv, h) + kpe).astype(BF16)
        va_ref[:, h * LANE:(h + 1) * LANE] = (blk(kv, A_HEADS + h) + ones_row).astype(BF16)

    qb_ref[...] = (_dot(xn, w_ref[:, CE_BQ:CE_BQ + QB_W]) * (B_DH ** -0.5 * LOG2E)).astype(BF16)
    hk = _dot(xn, w_ref[:, CE_KVC:CE_KVC + 256])
    for g in range(B_KV_GROUPS):
        kvc_ref[g] = rope(blk(hk, g), 1)
    tpos = (pl.program_id(0) % (SEQ // TM)) * TM + lax.broadcasted_iota(jnp.int32, (TM, LANE), 0)
    onehot = ((lane - 64) == (tpos // SLC_LEN)).astype(F32)
    hk = _dot(xn, w_ref[:, CE_KSVS:CE_KSVS + 256])
    vs_ref[...] = hk.astype(BF16)
    for g in range(B_KV_GROUPS):
        ks_ref[:, g * LANE:(g + 1) * LANE] = jnp.where(lo, rope(blk(hk, g), 1), onehot).astype(BF16)
    hk = _dot(xn, w_ref[:, CE_KWVW:CE_KWVW + 256])
    for g in range(B_KV_GROUPS):
        kw_ref[:, g * LANE:(g + 1) * LANE] = rope(blk(hk, g), 1).astype(BF16)

    qc_ref[...] = (_dot(xn, w_ref[:, CE_CQ:CE_CQ + QC_W]) * (C_DH ** -0.5 * LOG2E)).astype(BF16)
    hc = _dot(xn, w_ref[:, CE_CKV:CE_CKV + 512])
    for h in range(C_HEADS):
        kd_ref[:, h * LANE:(h + 1) * LANE] = rope(blk(hc, h), 2).astype(BF16)


def _proj_call(l, x, gain, w_e, rope_tab, qn, kvn, wuq, wukv):
    tok = lambda w: pl.BlockSpec((TM, w), lambda i: (i, 0))
    bf = lambda w: jax.ShapeDtypeStruct((N_TOK, w), BF16)
    return pl.pallas_call(
        _proj_kernel,
        grid=(N_TOK // TM,),
        in_specs=[tok(D_MODEL), _layer_spec((1, D_MODEL), l), _layer_spec((D_MODEL, CE), l),
                  pl.BlockSpec((None, TM, 2 * LANE), lambda i: (i // (SEQ // TM), i % (SEQ // TM), 0)),
                  _layer_spec((1, 256), l), _layer_spec((1, LANE), l),
                  _layer_spec((256, 768), l), _layer_spec((LANE, 1536), l)],
        out_specs=[tok(768), tok(768), tok(768), tok(QB_W),
                   pl.BlockSpec((B_KV_GROUPS, TM, LANE), lambda i: (0, i, 0)),
                   tok(256), tok(256), tok(256),
                   pl.BlockSpec((1, B_KV_GROUPS * GATE_ROWS, TM), lambda i: (i, 0, 0)),
                   tok(QC_W), tok(512)],
        out_shape=[bf(768), bf(768), bf(768), bf(QB_W),
                   jax.ShapeDtypeStruct((B_KV_GROUPS, N_TOK, LANE), F32),
                   bf(256), bf(256), bf(256),
                   jax.ShapeDtypeStruct((NCH, B_KV_GROUPS * GATE_ROWS, TM), F32),
                   bf(QC_W), bf(512)],
        compiler_params=pltpu.CompilerParams(
            dimension_semantics=("arbitrary",), vmem_limit_bytes=VMEM_LIMIT),
        name="proj",
    )(x, gain, w_e, rope_tab, qn, kvn, wuq, wukv)


def _tile_pos(r):
    kk = lax.broadcasted_iota(jnp.int32, (TK, 1), 0)
    tt = lax.broadcasted_iota(jnp.int32, (1, r), 1) & (TQ - 1)
    return kk, tt


def _dot_nt(a, b):
    return lax.dot_general(a, b, (((1,), (1,)), ((), ())), preferred_element_type=F32)


def _score_stage(q, k, mask, m_prev, s_ref, m_ref):
    s = _dot_nt(k, q)
    if mask is not None:
        s = jnp.where(mask, s, MASKED)
    cm = jnp.max(s, axis=0, keepdims=True)
    m_ref[...] = cm if m_prev is None else jnp.maximum(m_prev[...], cm)
    s_ref[...] = s


def _value_stage(vt, s_ref, m_ref, mb_ref, acc_ref):
    m = m_ref[...]
    alpha = jnp.exp2(mb_ref[...] - m)
    p = jnp.exp2(s_ref[...] - m).astype(BF16)
    mb_ref[...] = m
    acc_ref[...] = alpha * acc_ref[...] + _dot(vt, p)


def _causal_attention(streams, i):
    for qt, k_of, vt_of, (s0, s1, m0, m1, mb, acc) in streams:
        kk, tt = _tile_pos(qt.shape[0])
        _score_stage(qt, k_of(i), kk <= tt, None, s0, m0)
        mb[...] = m0[...]
        acc[...] = jnp.zeros_like(acc)

    def step(score_j, pend_j, a, b):
        for qt, k_of, vt_of, sc in streams:
            bufs, maxs, mb, acc = (sc[0], sc[1]), (sc[2], sc[3]), sc[4], sc[5]
            _score_stage(qt, k_of(score_j), None, maxs[a], bufs[b], maxs[b])
            _value_stage(vt_of(pend_j), bufs[a], maxs[a], mb, acc)

    def fold(pend_j, a):
        for qt, k_of, vt_of, sc in streams:
            _value_stage(vt_of(pend_j), sc[a], sc[2 + a], sc[4], sc[5])

    def body(jj, carry):
        step(2 * jj, jnp.where(jj == 0, i, 2 * jj - 1), 0, 1)
        step(2 * jj + 1, 2 * jj, 1, 0)
        return carry

    npairs = i // 2
    lax.fori_loop(0, npairs, body, 0)
    pend = jnp.where(npairs == 0, i, 2 * npairs - 1)

    @pl.when(i % 2 == 1)
    def _():
        step(i - 1, pend, 0, 1)
        fold(i - 1, 1)

    @pl.when(i % 2 == 0)
    def _():
        fold(pend, 0)

    return [st[3][5][...] for st in streams]


def _normalise(acc):
    return acc[0:VROW] / acc[VROW:VROW + 1, :]


def _pipe_scratch(r):
    return ([pltpu.VMEM((TK, r), F32)] * 2 + [pltpu.VMEM((1, r), F32)] * 3
            + [pltpu.VMEM((VT_ROWS, r), F32)])


N_PIPE = 6


def _k_chunk(k_ref, j, sl=slice(None)):
    return k_ref[pl.ds(pl.multiple_of(j * TK, TK), TK), sl]


def _q_spec(nb):
    return pl.BlockSpec((None, TQ, nb * LANE), lambda b, h, i: (b, i, h))


def _kv_spec(nb, buffers=None):
    mode = {} if buffers is None else {"pipeline_mode": pl.Buffered(buffers)}
    return pl.BlockSpec((None, SEQ, nb * LANE), lambda b, h, i: (b, 0, h), **mode)


_ROPE_SPEC = pl.BlockSpec((None, TQ, 2 * LANE), lambda b, h, i: (b, i, 0))


def _vt_lower(chunk):
    return chunk.T[0:VT_ROWS]


def _vt_upper(chunk):
    ones = (lax.broadcasted_iota(jnp.int32, (VT_ROWS - VROW, chunk.shape[0]), 0) == 0).astype(BF16)
    return jnp.concatenate([chunk.T[VROW:2 * VROW], ones], axis=0)


def _attn_params():
    return pltpu.CompilerParams(dimension_semantics=("arbitrary",) * 3,
                                vmem_limit_bytes=VMEM_LIMIT)


def _mla_kernel(q_ref, rope_ref, k_ref, v_ref, o_ref, *scratch):
    i = pl.program_id(2)
    qs = _rope_q(q_ref, rope_ref, 0)
    streams = []
    for h in range(A_HPS):
        sl = slice(h * LANE, (h + 1) * LANE)
        streams.append((qs[h],
                        lambda j, sl=sl: _k_chunk(k_ref, j, sl),
                        lambda j, sl=sl: _vt_lower(_k_chunk(v_ref, j, sl)),
                        scratch[N_PIPE * h:N_PIPE * (h + 1)]))
    accs = _causal_attention(streams, i)
    outs = [_normalise(a) for a in accs]
    if A_OW > A_HPS * A_V:
        outs.append(jnp.zeros((A_OW - A_HPS * A_V, TQ), F32))
    o_ref[...] = jnp.concatenate(outs, axis=0).T.astype(BF16)


def _mla_call(qa, rope_tab, ka, va):
    return pl.pallas_call(
        _mla_kernel,
        grid=(BATCH, A_HEADS // A_HPS, NQ),
        in_specs=[_q_spec(A_HPS), _ROPE_SPEC, _kv_spec(A_HPS), _kv_spec(A_HPS)],
        out_specs=pl.BlockSpec((None, TQ, A_OW), lambda b, h, i: (b, i, h)),
        out_shape=jax.ShapeDtypeStruct((BATCH, SEQ, WO_B), BF16),
        scratch_shapes=_pipe_scratch(TQ) * A_HPS,
        compiler_params=_attn_params(),
        name="mla_attn",
    )(qa, rope_tab, ka, va)


C_HPS = 4


def _diff_kernel(q_ref, rope_ref, kv_ref, lqk_ref, sn_ref, o_ref, *scratch, lam_init):
    i = pl.program_id(2)
    lane = lax.broadcasted_iota(jnp.int32, (TQ, LANE), 1)
    qs = _rope_q(q_ref, rope_ref, 2, packed=True)
    streams = []
    for h in range(C_HPS):
        sl = slice(h * LANE, (h + 1) * LANE)
        q = qs[h]
        zero = jnp.zeros_like(q)
        q2 = jnp.concatenate([jnp.where(lane < C_DH, q, zero),
                              jnp.where((lane >= C_DH) & (lane < 2 * C_DH), q, zero)], axis=0)
        streams.append((q2,
                        lambda j, sl=sl: _k_chunk(kv_ref, j, sl),
                        lambda j, sl=sl: _vt_upper(_k_chunk(kv_ref, j, sl)),
                        scratch[N_PIPE * h:N_PIPE * (h + 1)]))
    accs = _causal_attention(streams, i)
    lqk = lqk_ref[...]
    lam = (jnp.exp(jnp.sum(lqk[0:1] * lqk[1:2], axis=-1, keepdims=True))
           - jnp.exp(jnp.sum(lqk[2:3] * lqk[3:4], axis=-1, keepdims=True)) + lam_init)
    diffs = []
    for a in accs:
        o = _normalise(a)
        diffs.append(o[:, 0:TQ] - lam * o[:, TQ:2 * TQ])
    d = jnp.concatenate(diffs, axis=0).T
    d2 = d * d
    hl = lax.broadcasted_iota(jnp.int32, d.shape, 1) // (2 * C_DH)
    ms = jnp.zeros_like(d)
    for h in range(C_HPS):
        ms_h = jnp.sum(jnp.where(hl == h, d2, 0.0), axis=-1, keepdims=True)
        ms = jnp.where(hl == h, ms_h, ms)
    y = d * lax.rsqrt(ms * (1.0 / (2 * C_DH)) + EPS) * sn_ref[...] * (1.0 - lam_init)
    o_ref[...] = y.astype(BF16)


def _diff_call(l, qc, rope_tab, kd, lqk, sn):
    lam_init = 0.8 - 0.6 * math.exp(-0.3 * l)
    return pl.pallas_call(
        functools.partial(_diff_kernel, lam_init=lam_init),
        grid=(BATCH, C_HEADS // C_HPS, NQ),
        in_specs=[_q_spec(C_HPS // 2), _ROPE_SPEC, _kv_spec(C_HPS),
                  _layer_spec((8, LANE), l), _layer_spec((1, C_HPS * 2 * C_DH), l)],
        out_specs=pl.BlockSpec((None, TQ, C_HPS * 2 * C_DH), lambda b, h, i: (b, i, h)),
        out_shape=jax.ShapeDtypeStruct((BATCH, SEQ, C_HEADS * 2 * C_DH), BF16),
        scratch_shapes=_pipe_scratch(2 * TQ) * C_HPS,
        compiler_params=_attn_params(),
        name="diff_attn",
    )(qc, rope_tab, kd, lqk, sn)


B_GPS = 2
assert (B_GPS * B_HPG) % 2 == 0 and C_HPS % 2 == 0


def _nsa_kernel(q_ref, rope_ref, kvc_ref, ks_ref, vs_ref, kw_ref, gt_ref,
                pe_ref, w1_ref, w2k_ref, w2v_ref, ovt_ref,
                o_ref, kc_s, vct_s, *slc_scratch):
    i = pl.program_id(2)

    @pl.when(i == 0)
    def _():
        half = CMP_LEN // 2
        for g in range(B_GPS):
            yt = jnp.zeros((N_CMP_PAD, LANE), F32)
            yb = jnp.zeros((N_CMP_PAD, LANE), F32)
            for l in range(half):
                x = kvc_ref[g, pl.ds(l, N_CMP_PAD, stride=CMP_STRIDE), :]
                yt = yt + _dot((x + pe_ref[l:l + 1, :]).astype(BF16), w1_ref[l])
                yb = yb + _dot((x + pe_ref[half + l:half + l + 1, :]).astype(BF16),
                               w1_ref[half + l])
            pre = yt + pltpu.roll(yb, N_CMP_PAD - 1, 0)
            act = (pre * jax.nn.sigmoid(pre)).astype(BF16)
            kc_s[g] = _dot(act, w2k_ref[...]).astype(BF16)
            vct_s[g] = _dot(act, w2v_ref[...]).T[0:VROW].astype(BF16)

    qs = _rope_q(q_ref, rope_ref, 1, packed=True)
    fronts = [_nsa_front(i, g, qs[g * B_HPG:(g + 1) * B_HPG], kw_ref, kc_s, vct_s, ovt_ref)
              for g in range(B_GPS)]

    streams = []
    for g in range(B_GPS):
        sl = slice(g * LANE, (g + 1) * LANE)
        streams.append((fronts[g][0],
                        lambda j, sl=sl: _k_chunk(ks_ref, j, sl),
                        lambda j, sl=sl: _vt_upper(_k_chunk(vs_ref, j, sl)),
                        slc_scratch[N_PIPE * g:N_PIPE * (g + 1)]))
    accs = _causal_attention(streams, i)

    for g in range(B_GPS):
        _, o_cmp, o_win = fronts[g]
        o_slc = _normalise(accs[g])
        gate = gt_ref[g * GATE_ROWS:(g + 1) * GATE_ROWS, :]
        outs = []
        for h in range(B_HPG):
            sl = slice(h * TQ, (h + 1) * TQ)
            outs.append(gate[3 * h:3 * h + 1] * o_cmp[:, sl]
                        + gate[3 * h + 1:3 * h + 2] * o_slc[:, sl]
                        + gate[3 * h + 2:3 * h + 3] * o_win[:, sl])
        base = g * 2 * LANE
        o_ref[:, base:base + LANE] = jnp.concatenate(outs[0:2], axis=0).T.astype(BF16)
        o_ref[:, base + LANE:base + 2 * LANE] = jnp.concatenate(
            [outs[2], jnp.zeros((VROW, TQ), F32)], axis=0).T.astype(BF16)


def _nsa_front(i, g, q_heads, kw_ref, kc_s, vct_s, ovt_ref):
    r3 = B_HPG * TQ
    gl = slice(g * LANE, (g + 1) * LANE)
    q3 = jnp.concatenate(q_heads, axis=0)

    half = TQ // 2
    kq = lax.broadcasted_iota(jnp.int32, (half, 1), 0)
    tq = lax.broadcasted_iota(jnp.int32, (1, B_HPG * half), 1) & (half - 1)
    no_prev = jnp.where(i >= 1, 0, half)
    o_halves = []
    for hf in range(2):
        qh = jnp.concatenate([q[hf * half:(hf + 1) * half] for q in q_heads], axis=0)
        subs = [kw_ref[pl.ds(pl.multiple_of(jnp.maximum(2 * i - 2 + hf + d, 0) * half, half), half),
                       gl] for d in range(3)]
        s0 = jnp.where(tq + no_prev < kq, _dot_nt(subs[0], qh), MASKED)
        s1 = _dot_nt(subs[1], qh)
        if hf == 0:
            s1 = s1 + jnp.where(i >= 1, 0.0, MASKED)
        s2 = jnp.where(kq <= tq, _dot_nt(subs[2], qh), MASKED)
        scores = (s0, s1, s2)
        m_win = functools.reduce(jnp.maximum, [jnp.max(s, axis=0, keepdims=True) for s in scores])
        acc = sum(_dot(_vt_upper(k), jnp.exp2(s - m_win).astype(BF16))
                  for k, s in zip(subs, scores))
        o_halves.append(_normalise(acc))
    o_win = jnp.concatenate([o_halves[hf][:, h * half:(h + 1) * half]
                             for h in range(B_HPG) for hf in range(2)], axis=1)

    kk, tt = _tile_pos(r3)
    s = _dot_nt(kc_s[g], q3)
    c_end = lax.broadcasted_iota(jnp.int32, (N_CMP_PAD, 1), 0) * CMP_STRIDE + (CMP_LEN - 1)
    valid = c_end <= i * TQ + tt
    smax = jnp.max(jnp.where(valid, s, MASKED), axis=0, keepdims=True)
    e = jnp.where(valid, jnp.exp2(s - smax), 0.0)
    den = jnp.sum(e, axis=0, keepdims=True)
    p = e / jnp.where(den > 0.0, den, 1.0)
    o_cmp = _dot(vct_s[g], p.astype(BF16))

    psum = p[:, 0:TQ] + p[:, TQ:2 * TQ] + p[:, 2 * TQ:3 * TQ]
    p_hi = psum.astype(BF16)
    p_lo = (psum - p_hi.astype(F32)).astype(BF16)
    ovt = ovt_ref[...]
    imp = (_dot(ovt, p_hi) + _dot(ovt, p_lo))[64:128]
    m_idx = lax.broadcasted_iota(jnp.int32, (N_SLC, TQ), 0)
    cur = (i * TQ + lax.broadcasted_iota(jnp.int32, (N_SLC, TQ), 1)) // SLC_LEN
    forced = (m_idx == 0) | (m_idx == cur) | (m_idx == cur - 1)
    score = jnp.where(forced, 1e9, jnp.where(m_idx <= cur, imp, -1.0))
    sub = lax.broadcasted_iota(jnp.int32, (SUBLANES, TQ), 0)
    bias = [jnp.zeros((64, TQ), F32)]
    for v in range(N_SLC // SUBLANES):
        tile = score[v * SUBLANES:(v + 1) * SUBLANES, :]
        rank = jnp.zeros((SUBLANES, TQ), jnp.int32)
        for mp in range(N_SLC):
            srow = score[mp:mp + 1, :]
            if mp < v * SUBLANES:
                ahead = srow >= tile
            elif mp >= (v + 1) * SUBLANES:
                ahead = srow > tile
            else:
                ahead = (srow > tile) | ((srow == tile) & (sub > mp - v * SUBLANES))
            rank = rank + jnp.where(ahead, 1, 0)
        bias.append(jnp.where(rank < SLC_TOPN, 0.0, -SEL_BIG))
    bias = jnp.concatenate(bias, axis=0).T.astype(BF16)
    q_sel = q3 + jnp.concatenate([bias] * B_HPG, axis=0)
    return q_sel, o_cmp, o_win


def _nsa_call(l, qb, rope_tab, kvc, ks, vs, kw, gt, pe, w1, w2k, w2v, ovt):
    r3 = B_HPG * TQ
    return pl.pallas_call(
        _nsa_kernel,
        grid=(BATCH, B_KV_GROUPS // B_GPS, NQ),
        in_specs=[_q_spec(B_GPS * B_HPG // 2), _ROPE_SPEC,
                  pl.BlockSpec((B_GPS, None, SEQ, LANE), lambda b, g, i: (g, b, 0, 0),
                               pipeline_mode=pl.Buffered(1)),
                  _kv_spec(B_GPS, 1), _kv_spec(B_GPS, 1), _kv_spec(B_GPS, 1),
                  pl.BlockSpec((None, B_GPS * GATE_ROWS, TQ), lambda b, g, i: (b * NQ + i, g, 0)),
                  _layer_spec((CMP_LEN, LANE), l), _layer_spec((CMP_LEN, LANE, LANE), l),
                  _layer_spec((LANE, LANE), l), _layer_spec((LANE, LANE), l),
                  _const_spec((LANE, N_CMP_PAD))],
        out_specs=pl.BlockSpec((None, TQ, B_GPS * 2 * LANE), lambda b, g, i: (b, i, g)),
        out_shape=jax.ShapeDtypeStruct((BATCH, SEQ, B_KV_GROUPS * 2 * LANE), BF16),
        scratch_shapes=[pltpu.VMEM((B_GPS, N_CMP_PAD, LANE), BF16),
                        pltpu.VMEM((B_GPS, VROW, N_CMP_PAD), BF16)]
        + _pipe_scratch(r3) * B_GPS,
        compiler_params=_attn_params(),
        name="nsa_attn",
    )(qb, rope_tab, kvc, ks, vs, kw, gt, pe, w1, w2k, w2v, ovt)


def _expand_w_in(w):
    a = w[..., :A_COLS]
    b = w[..., A_COLS:A_COLS + B_COLS]
    c = w[..., A_COLS + B_COLS:]
    z = lambda n: jnp.zeros(w.shape[:-1] + (n,), w.dtype)
    gate = lambda g: [b[..., 1152 + g * 9:1161 + g * 9], z(GATE_ROWS - 9)]
    parts = [a[..., 0:192], a[..., 320:352]] + gate(0) + gate(1) + [a[..., 192:320]]
    parts.append(b[..., 0:QB_W])
    for base in (384, 640, 896):
        for g in range(B_KV_GROUPS):
            parts += [b[..., base + g * 64:base + g * 64 + 64],
                      b[..., base + 128 + g * 64:base + 192 + g * 64]]

    def rot_mid(t, o):
        return [t[..., o + 8:o + 24], t[..., o:o + 8], t[..., o + 24:o + 32]]

    for h in range(C_HEADS):
        parts += rot_mid(c, h * 64) + rot_mid(c, h * 64 + 32)
    for h in range(C_HEADS):
        parts += (rot_mid(c, 256 + h * 64) + rot_mid(c, 288 + h * 64)
                  + [c[..., 512 + h * 64:576 + h * 64]])
    return jnp.concatenate(parts, axis=-1).astype(BF16)


def _expand_w_out(w):
    z = lambda n: jnp.zeros((DEPTH, n, D_MODEL), w.dtype)
    parts = []
    aw = A_HPS * A_V
    for s in range(A_HEADS // A_HPS):
        parts.append(w[:, s * aw:(s + 1) * aw])
        if A_OW > aw:
            parts.append(z(A_OW - aw))
    b0 = A_HEADS * A_V
    gw = B_HPG * B_DH
    for g in range(B_KV_GROUPS):
        parts += [w[:, b0 + g * gw:b0 + (g + 1) * gw], z(2 * LANE - gw)]
    parts.append(w[:, b0 + B_KV_GROUPS * gw:])
    return jnp.concatenate(parts, axis=1).astype(BF16)


def _rope_table(positions):
    def inv(rot):
        return 1.0 / (ROPE_THETA ** (jnp.arange(0, rot, 2, dtype=F32) / rot))

    freqs = jnp.concatenate([inv(B_ROT), inv(C_ROT), inv(A_ROPE)])
    n_ang = B_ROT // 2 + C_ROT // 2 + A_ROPE // 2
    ang = positions.astype(F32)[:, :, None] * freqs
    feats = jnp.concatenate([jnp.cos(ang), jnp.sin(ang),
                             jnp.ones(ang.shape[:2] + (1,), F32)], axis=2)
    place = np.zeros((2 * n_ang + 1, 2 * LANE), np.float32)
    place[2 * n_ang, 0:LANE] = 1.0
    groups = [(0, 0, 8), (16, 8, 4), (48, 8, 4), (64, 12, 16)]
    for lane0, ang0, half in groups:
        for k in range(half):
            for lane, sgn in ((lane0 + k, -1.0), (lane0 + half + k, 1.0)):
                place[:, lane] = 0.0
                place[ang0 + k, lane] = 1.0
                place[n_ang + ang0 + k, LANE + lane] = sgn
    return jnp.einsum("btk,kl->btl", feats, jnp.asarray(place), precision=lax.Precision.HIGHEST)


def _overlap_t():
    c = jnp.arange(N_CMP_PAD)[None, :]
    m = jnp.arange(N_SLC)[:, None]
    ov = ((c * CMP_STRIDE < m * SLC_LEN + SLC_LEN)
          & (c * CMP_STRIDE + CMP_LEN - 1 >= m * SLC_LEN) & (c < N_CMP))
    return jnp.concatenate([jnp.zeros((64, N_CMP_PAD), BF16), ov.astype(BF16)], axis=0)


def kernel(x, positions, ffn1_norm, ffn1_wg, ffn1_wu, ffn1_wd, mix_norm, w_in, mla_q_norm, mla_kv_norm, mla_w_uq, mla_w_ukv, nsa_pe_k, nsa_pe_v, nsa_phi_k1, nsa_phi_k2, nsa_phi_v1, nsa_phi_v2, diff_lq1, diff_lk1, diff_lq2, diff_lk2, diff_sub_norm, w_out, ffn2_norm, ffn2_wg, ffn2_wu, ffn2_wd, final_norm):
    xf = x.reshape(N_TOK, D_MODEL)
    rope_tab = _rope_table(positions)
    ovt = _overlap_t()
    bf = lambda t: t.astype(BF16)
    row = lambda t: t.reshape(DEPTH, 1, t.shape[-1])
    pad_to = lambda t, axis, n: jnp.pad(
        t, [(0, n - t.shape[a]) if a == axis else (0, 0) for a in range(t.ndim)])

    ffn1 = (row(ffn1_norm), bf(ffn1_wg), bf(ffn1_wu), bf(ffn1_wd))
    ffn2 = (row(ffn2_norm), bf(ffn2_wg), bf(ffn2_wu), bf(ffn2_wd))
    w_e = _expand_w_in(w_in)
    w_o = _expand_w_out(w_out)
    wuq = pad_to(pad_to(mla_w_uq.reshape(DEPTH, A_Q_RANK, A_HEADS, A_NOPE + A_ROPE), 3, LANE), 1, 256)
    wuq = bf(wuq.reshape(DEPTH, 256, A_HEADS * LANE))
    wkv = mla_w_ukv.reshape(DEPTH, A_KV_RANK, A_HEADS, A_NOPE + A_V)
    wukv = bf(jnp.concatenate(
        [pad_to(wkv[..., :A_NOPE], 3, LANE).reshape(DEPTH, A_KV_RANK, A_HEADS * LANE),
         pad_to(wkv[..., A_NOPE:], 3, LANE).reshape(DEPTH, A_KV_RANK, A_HEADS * LANE)], axis=2))
    qn = pad_to(row(mla_q_norm), 2, 256)
    kvn = row(mla_kv_norm)

    k1 = nsa_phi_k1.reshape(DEPTH, CMP_LEN, B_DH, B_DH)
    v1 = nsa_phi_v1.reshape(DEPTH, CMP_LEN, B_DH, B_DH)
    w1 = bf(jnp.concatenate([pad_to(k1, 3, LANE), jnp.pad(v1, ((0, 0), (0, 0), (0, 0), (64, 0)))],
                            axis=2))
    pe = jnp.concatenate([nsa_pe_k, nsa_pe_v], axis=2)
    w2k = bf(pad_to(pad_to(nsa_phi_k2, 1, LANE), 2, LANE))
    w2v = bf(jnp.pad(nsa_phi_v2, ((0, 0), (64, 0), (0, 64))))
    lqk = pad_to(pad_to(jnp.stack([diff_lq1, diff_lk1, diff_lq2, diff_lk2], axis=1), 1, 8), 2, LANE)
    sn = jnp.tile(row(diff_sub_norm), (1, 1, C_HPS))

    b3 = lambda t: t.reshape(BATCH, SEQ, t.shape[-1])
    flat = lambda t: t.reshape(N_TOK, t.shape[-1])
    rope3 = rope_tab
    for l in range(DEPTH):
        xf = _ffn_call(l, xf, *ffn1)
        (qa, ka, va, qb, kvc, ks, vs, kw, gt, qc, kd) = _proj_call(
            l, xf, row(mix_norm), w_e, rope_tab, qn, kvn, wuq, wukv)
        oa = _mla_call(b3(qa), rope3, b3(ka), b3(va))
        ob = _nsa_call(l, b3(qb), rope3, kvc.reshape(B_KV_GROUPS, BATCH, SEQ, LANE),
                       b3(ks), b3(vs), b3(kw), gt, pe, w1, w2k, w2v, ovt)
        oc = _diff_call(l, b3(qc), rope3, b3(kd), lqk, sn)
        fg = final_norm.reshape(1, D_MODEL) if l == DEPTH - 1 else None
        xf = _ffn_call(l, xf, *ffn2, mix=(flat(oa), flat(ob), flat(oc), w_o), final_gain=fg)
    return xf.reshape(BATCH, SEQ, D_MODEL)
```
